```python
import math
import jax, jax.numpy as jnp
from jax import lax
import numpy as np

D_MODEL = 1024
BATCH = 1
SEQ = 16384
DEPTH = 2
DEC_BATCH = 8
DEC_SEQ = 32
PAST_LEN = 1024

CHUNK = 64
EPS = 1e-6
LRU_W = D_MODEL // 2
LRU_HEADS = 8
LRU_HD = LRU_W // LRU_HEADS
CONV_W = 4
LRU_C = 8.0
SSM_W = D_MODEL - LRU_W
SSM_H = 16
SSM_G = SSM_W // SSM_H
SSM_P = 64
MIX_W = LRU_W + SSM_W
IN_W = 2 * LRU_W + SSM_W
D_FF = -(-8 * D_MODEL // (3 * 256)) * 256

kernel_name = "hymba_rglru_s5_streaming_step"


def _rmsnorm(x, g):
    xf = x.astype(jnp.float32)
    var = jnp.mean(xf * xf, axis=-1, keepdims=True)
    return (xf * lax.rsqrt(var + EPS) * g.astype(jnp.float32)).astype(x.dtype)


def _causal_conv(x, buf, w, b):
    T = x.shape[1]
    xp = jnp.concatenate([buf.astype(x.dtype), x], axis=1)
    y = b.astype(x.dtype)
    for k in range(CONV_W):
        y = y + xp[:, k:k + T] * w[k].astype(x.dtype)
    return y, xp[:, T:]


def _lin_combine(e1, e2):
    a1, b1 = e1
    a2, b2 = e2
    return (a1 * a2, a2 * b1 + b2)


def _rglru(xc, h0, wa, ba, wx, bx, lam):
    B_, T, W = xc.shape
    xf = xc.astype(jnp.float32)
    xh = xf.reshape(B_, T, LRU_HEADS, LRU_HD)
    r = jax.nn.sigmoid(jnp.einsum('bthi,hij->bthj', xh, wa.astype(jnp.float32)).reshape(B_, T, W) + ba.astype(jnp.float32))
    ig = jax.nn.sigmoid(jnp.einsum('bthi,hij->bthj', xh, wx.astype(jnp.float32)).reshape(B_, T, W) + bx.astype(jnp.float32))
    log_a = -LRU_C * r * jax.nn.softplus(-lam.astype(jnp.float32))
    a = jnp.exp(log_a)
    b = jnp.sqrt(-jnp.expm1(2.0 * log_a)) * (ig * xf)
    b = b.at[:, 0].add(a[:, 0] * h0.astype(jnp.float32))
    _, h = lax.associative_scan(_lin_combine, (a, b), axis=1)
    return h, h[:, -1]


def _cplx_combine(e1, e2):
    a1r, a1i, b1r, b1i = e1
    a2r, a2i, b2r, b2i = e2
    return (a2r * a1r - a2i * a1i, a2r * a1i + a2i * a1r,
            a2r * b1r - a2i * b1i + b2r, a2r * b1i + a2i * b1r + b2i)


def _s5(u, h_re, h_im, lam_re, lam_im, b_re, b_im, c_re, c_im, d, log_dt):
    B_, T, _ = u.shape
    f32 = jnp.float32
    uf = u.astype(f32)
    lr, li = lam_re.astype(f32), lam_im.astype(f32)
    dt = jnp.exp(log_dt.astype(f32))[:, None]
    mag = jnp.exp(lr * dt)
    lbr, lbi = mag * jnp.cos(li * dt), mag * jnp.sin(li * dt)
    nr, ni = lbr - 1.0, lbi
    den = lr * lr + li * li
    gr, gi = (nr * lr + ni * li) / den, (ni * lr - nr * li) / den
    br, bi = b_re.astype(f32), b_im.astype(f32)
    bbr = gr[..., None] * br - gi[..., None] * bi
    bbi = gr[..., None] * bi + gi[..., None] * br
    cr, ci = c_re.astype(f32), c_im.astype(f32)
    blk = CHUNK if T >= CHUNK else T
    n = T // blk
    uc = uf.reshape(B_, n, blk, SSM_G, SSM_H).transpose(1, 0, 2, 3, 4)

    def step(carry, u_blk):
        hr, hi = carry
        bur = jnp.einsum('blgh,gph->blgp', u_blk, bbr)
        bui = jnp.einsum('blgh,gph->blgp', u_blk, bbi)
        bur = bur.at[:, 0].add(lbr * hr - lbi * hi)
        bui = bui.at[:, 0].add(lbr * hi + lbi * hr)
        ar = jnp.broadcast_to(lbr, bur.shape)
        ai = jnp.broadcast_to(lbi, bui.shape)
        _, _, xr, xi = lax.associative_scan(_cplx_combine, (ar, ai, bur, bui), axis=1)
        y = jnp.einsum('ghp,blgp->blgh', cr, xr) - jnp.einsum('ghp,blgp->blgh', ci, xi)
        return (xr[:, -1], xi[:, -1]), y

    (hr, hi), ys = lax.scan(step, (h_re.astype(f32), h_im.astype(f32)), uc)
    y = ys.transpose(1, 0, 2, 3, 4).reshape(B_, T, SSM_W) + d.astype(f32) * uf
    return y, hr, hi


def _layer(x, conv_buf, lru_h, ssm_re, ssm_im, p):
    dt = x.dtype
    h = _rmsnorm(x, p['norm_mix'])
    z = jnp.einsum('btd,de->bte', h, p['w_in'])
    xl = z[..., :LRU_W]
    gl = z[..., LRU_W:2 * LRU_W]
    us = z[..., 2 * LRU_W:]
    xc, new_conv = _causal_conv(xl, conv_buf, p['conv_w'], p['conv_b'])
    hl, new_lru = _rglru(xc, lru_h, p['lru_wa'], p['lru_ba'], p['lru_wx'], p['lru_bx'], p['lru_lambda'])
    y_lru = (hl * jax.nn.gelu(gl.astype(jnp.float32))).astype(dt)
    ys, new_re, new_im = _s5(us, ssm_re, ssm_im, p['ssm_lambda_re'], p['ssm_lambda_im'], p['ssm_b_re'], p['ssm_b_im'],
                             p['ssm_c_re'], p['ssm_c_im'], p['ssm_d'], p['ssm_log_dt'])
    g = jax.nn.gelu(ys)
    y_ssm = (g * jax.nn.sigmoid(jnp.einsum('bte,ef->btf', g, p['ssm_w_glu'].astype(jnp.float32)) + p['ssm_b_glu'].astype(jnp.float32))).astype(dt)
    mix = jnp.concatenate([_rmsnorm(y_lru, p['norm_lru_out']), _rmsnorm(y_ssm, p['norm_ssm_out'])], axis=-1)
    x = x + jnp.einsum('bte,ed->btd', mix, p['w_out'])
    h2 = _rmsnorm(x, p['norm_ffn'])
    ff = jax.nn.silu(jnp.einsum('btd,df->btf', h2, p['w_gate'])) * jnp.einsum('btd,df->btf', h2, p['w_up'])
    x = x + jnp.einsum('btf,fd->btd', ff, p['w_down'])
    return x, new_conv, new_lru, new_re, new_im


def setup_inputs(seed: int = 0) -> dict:
    key = jax.random.key(seed)
    ks = jax.random.split(key, 40)
    f32 = jnp.float32

    def nrm(k, shape, scale):
        return jax.random.normal(k, shape, f32) * scale

    def gain(k, shape):
        return 1.0 + 0.02 * jax.random.normal(k, shape, f32)

    a0 = jax.random.uniform(ks[14], (DEPTH, LRU_W), f32, 0.9, 0.999)
    lam_im_base = jnp.pi * jnp.arange(SSM_P, dtype=f32)
    return {
        "x_prompt": nrm(ks[0], (BATCH, SEQ, D_MODEL), 1.0),
        "x_sample": nrm(ks[1], (DEC_BATCH, DEC_SEQ, D_MODEL), 1.0),
        "state_conv": nrm(ks[2], (DEPTH, DEC_BATCH, CONV_W - 1, LRU_W), 1.0),
        "state_lru": nrm(ks[3], (DEPTH, DEC_BATCH, LRU_W), 0.5),
        "state_ssm_re": nrm(ks[4], (DEPTH, DEC_BATCH, SSM_G, SSM_P), 0.5),
        "state_ssm_im": nrm(ks[5], (DEPTH, DEC_BATCH, SSM_G, SSM_P), 0.5),
        "norm_mix": gain(ks[6], (DEPTH, D_MODEL)),
        "w_in": nrm(ks[7], (DEPTH, D_MODEL, IN_W), D_MODEL ** -0.5),
        "conv_w": nrm(ks[8], (DEPTH, CONV_W, LRU_W), CONV_W ** -0.5),
        "conv_b": nrm(ks[9], (DEPTH, LRU_W), 0.01),
        "lru_wa": nrm(ks[10], (DEPTH, LRU_HEADS, LRU_HD, LRU_HD), LRU_HD ** -0.5),
        "lru_ba": nrm(ks[11], (DEPTH, LRU_W), 0.01),
        "lru_wx": nrm(ks[12], (DEPTH, LRU_HEADS, LRU_HD, LRU_HD), LRU_HD ** -0.5),
        "lru_bx": nrm(ks[13], (DEPTH, LRU_W), 0.01),
        "lru_lambda": jnp.log(a0) - jnp.log1p(-a0),
        "ssm_lambda_re": -0.5 + nrm(ks[15], (DEPTH, SSM_G, SSM_P), 0.01),
        "ssm_lambda_im": lam_im_base + nrm(ks[16], (DEPTH, SSM_G, SSM_P), 0.01),
        "ssm_b_re": nrm(ks[17], (DEPTH, SSM_G, SSM_P, SSM_H), (2 * SSM_H) ** -0.5),
        "ssm_b_im": nrm(ks[18], (DEPTH, SSM_G, SSM_P, SSM_H), (2 * SSM_H) ** -0.5),
        "ssm_c_re": nrm(ks[19], (DEPTH, SSM_G, SSM_H, SSM_P), SSM_P ** -0.5),
        "ssm_c_im": nrm(ks[20], (DEPTH, SSM_G, SSM_H, SSM_P), SSM_P ** -0.5),
        "ssm_d": nrm(ks[21], (DEPTH, SSM_W), 0.5),
        "ssm_log_dt": jax.random.uniform(ks[22], (DEPTH, SSM_G), f32, math.log(0.001), math.log(0.1)),
        "ssm_w_glu": nrm(ks[23], (DEPTH, SSM_W, SSM_W), SSM_W ** -0.5),
        "ssm_b_glu": nrm(ks[24], (DEPTH, SSM_W), 0.01),
        "norm_lru_out": gain(ks[25], (DEPTH, LRU_W)),
        "norm_ssm_out": gain(ks[26], (DEPTH, SSM_W)),
        "w_out": nrm(ks[27], (DEPTH, MIX_W, D_MODEL), MIX_W ** -0.5),
        "norm_ffn": gain(ks[28], (DEPTH, D_MODEL)),
        "w_gate": nrm(ks[29], (DEPTH, D_MODEL, D_FF), D_MODEL ** -0.5),
        "w_up": nrm(ks[30], (DEPTH, D_MODEL, D_FF), D_MODEL ** -0.5),
        "w_down": nrm(ks[31], (DEPTH, D_FF, D_MODEL), D_FF ** -0.5),
        "norm_final": gain(ks[32], (D_MODEL,)),
    }


def reference(x_prompt, x_sample, state_conv, state_lru, state_ssm_re, state_ssm_im,
              norm_mix, w_in, conv_w, conv_b, lru_wa, lru_ba, lru_wx, lru_bx, lru_lambda,
              ssm_lambda_re, ssm_lambda_im, ssm_b_re, ssm_b_im, ssm_c_re, ssm_c_im, ssm_d, ssm_log_dt,
              ssm_w_glu, ssm_b_glu, norm_lru_out, norm_ssm_out, w_out, norm_ffn, w_gate, w_up, w_down,
              norm_final):
    bp = x_prompt.shape[0]
    conv0 = jnp.zeros((bp, CONV_W - 1, LRU_W), x_prompt.dtype)
    lru0 = jnp.zeros((bp, LRU_W), jnp.float32)
    re0 = jnp.zeros((bp, SSM_G, SSM_P), jnp.float32)
    im0 = jnp.zeros((bp, SSM_G, SSM_P), jnp.float32)
    yp, ys = x_prompt, x_sample
    pc, pl, pr, pi_, sc, sl, sr, si = [], [], [], [], [], [], [], []
    for l in range(DEPTH):
        p = {
            'norm_mix': norm_mix[l], 'w_in': w_in[l], 'conv_w': conv_w[l], 'conv_b': conv_b[l],
            'lru_wa': lru_wa[l], 'lru_ba': lru_ba[l], 'lru_wx': lru_wx[l], 'lru_bx': lru_bx[l],
            'lru_lambda': lru_lambda[l], 'ssm_lambda_re': ssm_lambda_re[l], 'ssm_lambda_im': ssm_lambda_im[l],
            'ssm_b_re': ssm_b_re[l], 'ssm_b_im': ssm_b_im[l], 'ssm_c_re': ssm_c_re[l], 'ssm_c_im': ssm_c_im[l],
            'ssm_d': ssm_d[l], 'ssm_log_dt': ssm_log_dt[l], 'ssm_w_glu': ssm_w_glu[l], 'ssm_b_glu': ssm_b_glu[l],
            'norm_lru_out': norm_lru_out[l], 'norm_ssm_out': norm_ssm_out[l], 'w_out': w_out[l],
            'norm_ffn': norm_ffn[l], 'w_gate': w_gate[l], 'w_up': w_up[l], 'w_down': w_down[l],
        }
        yp, c1, h1, r1, i1 = _layer(yp, conv0, lru0, re0, im0, p)
        ys, c2, h2, r2, i2 = _layer(ys, state_conv[l], state_lru[l], state_ssm_re[l], state_ssm_im[l], p)
        pc.append(c1); pl.append(h1); pr.append(r1); pi_.append(i1)
        sc.append(c2.astype(state_conv.dtype)); sl.append(h2.astype(state_lru.dtype))
        sr.append(r2.astype(state_ssm_re.dtype)); si.append(i2.astype(state_ssm_im.dtype))
    y_prompt = _rmsnorm(yp, norm_final)
    y_sample = _rmsnorm(ys, norm_final)
    return (y_prompt, y_sample,
            jnp.stack(pc), jnp.stack(pl), jnp.stack(pr), jnp.stack(pi_),
            jnp.stack(sc), jnp.stack(sl), jnp.stack(sr), jnp.stack(si))
```

```python
import functools
import math

import jax
import jax.numpy as jnp
from jax import lax
from jax.experimental import pallas as pl
from jax.experimental.pallas import tpu as pltpu

D_MODEL = 1024
LRU_W = 512
LRU_HEADS = 8
LRU_HD = LRU_W // LRU_HEADS
CONV_W = 4
LRU_C = 8.0
SSM_W = 512
SSM_H = 16
SSM_G = SSM_W // SSM_H
SSM_P = 64
IN_W = 2 * LRU_W + SSM_W
D_FF = 2816
EPS = 1e-6

SUBLANES = 8
LANES = 128
MXU_DIM = 256
VMEM_LIMIT_BYTES = 56 * 1024 * 1024

SSM_SLABS = 4
GATE_HALVES = LRU_W // MXU_DIM
FF_CHUNK = MXU_DIM
BF16 = jnp.bfloat16
F32 = jnp.float32


def _rmsnorm(x, gain):
    var = jnp.mean(x * x, axis=-1, keepdims=True)
    return x * lax.rsqrt(var + EPS) * gain


def _dot(a, b):
    return jnp.dot(a, b, preferred_element_type=F32)


def _s5_prep_kernel(lr_ref, li_ref, ldt_ref, br_ref, bi_ref, lbr_ref, lbi_ref, bbr_ref, bbi_ref):
    lr = lr_ref[...]
    li = li_ref[...]
    dt = jnp.exp(ldt_ref[...])
    mag = jnp.exp(lr * dt)
    lbr = mag * jnp.cos(li * dt)
    lbi = mag * jnp.sin(li * dt)
    nr, ni = lbr - 1.0, lbi
    den = lr * lr + li * li
    gr = (nr * lr + ni * li) / den
    gi = (ni * lr - nr * li) / den
    br = br_ref[...]
    bi = bi_ref[...]
    lbr_ref[...] = lbr
    lbi_ref[...] = lbi
    bbr_ref[...] = gr * br - gi * bi
    bbi_ref[...] = gr * bi + gi * br


def _s5_prep(lam_re, lam_im, log_dt, b_re, b_im):
    depth = lam_re.shape[0]
    rows = depth * SSM_G * SSM_H

    def expand(a):
        return jnp.broadcast_to(a[:, :, None, :], (depth, SSM_G, SSM_H, SSM_P)).reshape(rows, SSM_P)

    ldt = jnp.broadcast_to(log_dt[:, :, None, None], (depth, SSM_G, SSM_H, SSM_P)).reshape(rows, SSM_P)
    br = jnp.swapaxes(b_re, 2, 3).reshape(rows, SSM_P)
    bi = jnp.swapaxes(b_im, 2, 3).reshape(rows, SSM_P)
    sds = jax.ShapeDtypeStruct((rows, SSM_P), F32)
    lbr, lbi, bbr, bbi = pl.pallas_call(
        _s5_prep_kernel, out_shape=(sds, sds, sds, sds), name="s5_prep",
    )(expand(lam_re), expand(lam_im), ldt, br, bi)
    shape4 = (depth, SSM_G, SSM_H, SSM_P)
    lbr = lbr.reshape(shape4)[:, :, 0, :]
    lbi = lbi.reshape(shape4)[:, :, 0, :]
    return lbr, lbi, bbr.reshape(shape4), bbi.reshape(shape4)


def _within_tile_shift(v3, shift, fill):
    rolled = pltpu.roll(v3, shift, 1)
    row = lax.broadcasted_iota(jnp.int32, v3.shape, 1)
    return jnp.where(row >= shift, rolled, fill)


def _mixer_kernel(nb, seq,
                  x_ref, conv0_ref, lru0_ref, ssm0_ref,
                  gmix_ref, win_ref, convw_ref, convb_ref, wgate_ref, bgate_ref, lam_ref,
                  lbr_ref, lbi_ref, bw_ref, cw_ref, dskip_ref, wglu_ref, bglu_ref,
                  glru_ref, gssm_ref, wout_ref,
                  y_ref, convc_ref, lruc_ref, ssmc_ref,
                  xp_ref, a_ref, b_ref, xs_ref):
    rows = nb * seq
    step = pl.program_id(0)

    @pl.when(step == 0)
    def _():
        convc_ref[...] = conv0_ref[...]
        lruc_ref[...] = lru0_ref[...]
        ssmc_ref[...] = ssm0_ref[...]

    x = x_ref[...]
    h = _rmsnorm(x, gmix_ref[...]).astype(BF16)
    z = _dot(h, win_ref[...])
    xl = z[:, :LRU_W]
    gl = z[:, LRU_W:2 * LRU_W]
    us = z[:, 2 * LRU_W:]

    for j in range(nb):
        xp_ref[j, 0:SUBLANES, :] = convc_ref[j]
        xp_ref[j, SUBLANES:SUBLANES + seq, :] = xl[j * seq:(j + 1) * seq, :]
    cw = convw_ref[...]
    xc_parts = []
    for j in range(nb):
        acc = convb_ref[...] + xl[j * seq:(j + 1) * seq, :] * cw[CONV_W - 1:CONV_W, :]
        for k in range(CONV_W - 1):
            off = SUBLANES - (CONV_W - 1) + k
            acc = acc + xp_ref[j, off:off + seq, :] * cw[k:k + 1, :]
        xc_parts.append(acc)
        convc_ref[j] = xp_ref[j, seq:seq + SUBLANES, :]
    xc = xc_parts[0] if nb == 1 else jnp.concatenate(xc_parts, axis=0)

    lam = lam_ref[...]
    neg_c_softplus = -LRU_C * (jnp.maximum(-lam, 0.0) + jnp.log1p(jnp.exp(-jnp.abs(lam))))
    xc_bf = xc.astype(BF16)
    for hf in range(GATE_HALVES):
        lo, hi = hf * MXU_DIM, (hf + 1) * MXU_DIM
        pre = _dot(xc_bf[:, lo:hi], wgate_ref[hf])
        r = jax.nn.sigmoid(pre[:, :MXU_DIM] + bgate_ref[0:1, lo:hi])
        ig = jax.nn.sigmoid(pre[:, MXU_DIM:] + bgate_ref[1:2, lo:hi])
        log_a = neg_c_softplus[:, lo:hi] * r
        a = jnp.exp(log_a)
        a_ref[:, lo:hi] = a
        one_minus_a2 = -jnp.tanh(log_a) * (a * a + 1.0)
        b_ref[:, lo:hi] = jnp.sqrt(one_minus_a2) * (ig * xc[:, lo:hi])

    tiles = rows // SUBLANES
    a3 = a_ref[...].reshape(tiles, SUBLANES, LRU_W)
    b3 = b_ref[...].reshape(tiles, SUBLANES, LRU_W)
    shift = 1
    while shift < SUBLANES:
        a_sh = _within_tile_shift(a3, shift, 1.0)
        b_sh = _within_tile_shift(b3, shift, 0.0)
        b3 = b3 + a3 * b_sh
        a3 = a3 * a_sh
        shift *= 2
    a_ref[...] = a3.reshape(rows, LRU_W)
    b_ref[...] = b3.reshape(rows, LRU_W)

    for j in range(nb):
        carry0 = jnp.broadcast_to(lruc_ref[j:j + 1, :], (SUBLANES, LRU_W))

        def lru_body(t, carry, j=j):
            r0 = pl.multiple_of(j * seq + t * SUBLANES, SUBLANES)
            hh = b_ref[pl.ds(r0, SUBLANES), :] + a_ref[pl.ds(r0, SUBLANES), :] * carry
            b_ref[pl.ds(r0, SUBLANES), :] = hh
            return jnp.broadcast_to(hh[SUBLANES - 1:SUBLANES, :], (SUBLANES, LRU_W))

        carry = lax.fori_loop(0, seq // SUBLANES, lru_body, carry0, unroll=4)
        lruc_ref[j:j + 1, :] = carry[0:1, :]

    y_lru = b_ref[...] * jax.nn.gelu(gl)
    n_lru = _rmsnorm(y_lru, glru_ref[...]).astype(BF16)

    us_bf = us.astype(BF16)
    per_chunk = LANES // (2 * SSM_H)
    for c in range(SSM_W // LANES):
        bu = _dot(us_bf[:, c * LANES:(c + 1) * LANES], bw_ref[c])
        half = (c * per_chunk) // SUBLANES
        for sl in range(per_chunk):
            srow = (c * per_chunk + sl) % SUBLANES
            for reim in range(2):
                col = (sl * 2 + reim) * LANES
                xs_ref[reim * 2 + half, pl.ds(srow, rows, stride=SUBLANES), :] = bu[:, col:col + LANES]

    lbr0, lbr1 = lbr_ref[0], lbr_ref[1]
    lbi0, lbi1 = lbi_ref[0], lbi_ref[1]
    for j in range(nb):
        init = tuple(ssmc_ref[j, k] for k in range(SSM_SLABS))

        def s5_body(t, carry, j=j):
            xr0, xr1, xi0, xi1 = carry
            r0 = pl.multiple_of((j * seq + t) * SUBLANES, SUBLANES)
            sl = pl.ds(r0, SUBLANES)
            nr0 = lbr0 * xr0 - lbi0 * xi0 + xs_ref[0, sl, :]
            nr1 = lbr1 * xr1 - lbi1 * xi1 + xs_ref[1, sl, :]
            ni0 = lbr0 * xi0 + lbi0 * xr0 + xs_ref[2, sl, :]
            ni1 = lbr1 * xi1 + lbi1 * xr1 + xs_ref[3, sl, :]
            xs_ref[0, sl, :] = nr0
            xs_ref[1, sl, :] = nr1
            xs_ref[2, sl, :] = ni0
            xs_ref[3, sl, :] = ni1
            return nr0, nr1, ni0, ni1

        fin = lax.fori_loop(0, seq, s5_body, init, unroll=8)
        for k in range(SSM_SLABS):
            ssmc_ref[j, k] = fin[k]

    ys_parts = []
    for c in range(SSM_W // LANES):
        acc = None
        for sl in range(per_chunk):
            s = c * per_chunk + sl
            half, srow = s // SUBLANES, s % SUBLANES
            xr = xs_ref[half, pl.ds(srow, rows, stride=SUBLANES), :]
            xi = xs_ref[2 + half, pl.ds(srow, rows, stride=SUBLANES), :]
            xcat = jnp.concatenate([xr, xi], axis=1).astype(BF16)
            part = _dot(xcat, cw_ref[s])
            acc = part if acc is None else acc + part
        ys_parts.append(acc)
    ys = jnp.concatenate(ys_parts, axis=1) + dskip_ref[...] * us
    g = jax.nn.gelu(ys)
    glu = _dot(g.astype(BF16), wglu_ref[...]) + bglu_ref[...]
    y_ssm = g * jax.nn.sigmoid(glu)
    n_ssm = _rmsnorm(y_ssm, gssm_ref[...]).astype(BF16)

    y_ref[...] = x + _dot(n_lru, wout_ref[0:LRU_W, :]) + _dot(n_ssm, wout_ref[LRU_W:, :])


def _full_spec(shape):
    zeros = (0,) * len(shape)
    return pl.BlockSpec(shape, lambda i, _z=zeros: _z)


def _mixer_call(x2d, conv0, lru0, ssm0, w, nb, seq):
    total = x2d.shape[0]
    rows = nb * seq
    grid = (total // rows,)
    weights = (w["gmix"], w["win"], w["convw"], w["convb"], w["wgate"], w["bgate"], w["lam"],
               w["lbr"], w["lbi"], w["bw"], w["cw"], w["dskip"], w["wglu"], w["bglu"],
               w["glru"], w["gssm"], w["wout"])
    in_specs = [pl.BlockSpec((rows, D_MODEL), lambda i: (i, 0)),
                _full_spec(conv0.shape), _full_spec(lru0.shape), _full_spec(ssm0.shape)]
    in_specs += [_full_spec(a.shape) for a in weights]
    out_shape = (jax.ShapeDtypeStruct((total, D_MODEL), F32),
                 jax.ShapeDtypeStruct(conv0.shape, F32),
                 jax.ShapeDtypeStruct(lru0.shape, F32),
                 jax.ShapeDtypeStruct(ssm0.shape, F32))
    out_specs = (pl.BlockSpec((rows, D_MODEL), lambda i: (i, 0)),
                 _full_spec(conv0.shape), _full_spec(lru0.shape), _full_spec(ssm0.shape))
    scratch = [pltpu.VMEM((nb, SUBLANES + seq, LRU_W), F32),
               pltpu.VMEM((rows, LRU_W), F32),
               pltpu.VMEM((rows, LRU_W), F32),
               pltpu.VMEM((SSM_SLABS, rows * SUBLANES, LANES), F32)]
    return pl.pallas_call(
        functools.partial(_mixer_kernel, nb, seq),
        grid=grid, in_specs=in_specs, out_specs=out_specs, out_shape=out_shape,
        scratch_shapes=scratch, name="mixer",
        compiler_params=pltpu.CompilerParams(dimension_semantics=("arbitrary",),
                                             vmem_limit_bytes=VMEM_LIMIT_BYTES),
    )(x2d, conv0, lru0, ssm0, *weights)


def _ffn_kernel(final_norm, x_ref, gffn_ref, wg_ref, wu_ref, wd_ref, gfin_ref, o_ref):
    x = x_ref[...]
    h = _rmsnorm(x, gffn_ref[...]).astype(BF16)
    acc = x
    for c in range(D_FF // FF_CHUNK):
        lo, hi = c * FF_CHUNK, (c + 1) * FF_CHUNK
        gate = _dot(h, wg_ref[:, lo:hi])
        up = _dot(h, wu_ref[:, lo:hi])
        act = (gate * jax.nn.sigmoid(gate) * up).astype(BF16)
        acc = acc + _dot(act, wd_ref[lo:hi, :])
    if final_norm:
        acc = _rmsnorm(acc, gfin_ref[...])
    o_ref[...] = acc


def _ffn_call(x2d, w, gfin, final_norm, tile):
    total = x2d.shape[0]
    weights = (w["gffn"], w["wg"], w["wu"], w["wd"], gfin)
    in_specs = [pl.BlockSpec((tile, D_MODEL), lambda i: (i, 0))]
    in_specs += [_full_spec(a.shape) for a in weights]
    return pl.pallas_call(
        functools.partial(_ffn_kernel, final_norm),
        grid=(total // tile,), in_specs=in_specs,
        out_specs=pl.BlockSpec((tile, D_MODEL), lambda i: (i, 0)),
        out_shape=jax.ShapeDtypeStruct((total, D_MODEL), F32), name="ffn",
        compiler_params=pltpu.CompilerParams(dimension_semantics=("arbitrary",),
                                             vmem_limit_bytes=VMEM_LIMIT_BYTES),
    )(x2d, *weights)


def _gate_weights(wa, wx):
    per = LRU_HEADS // GATE_HALVES
    eye = jnp.eye(per, dtype=F32)

    def blockdiag(w):
        w4 = w.reshape(GATE_HALVES, per, LRU_HD, LRU_HD)
        return jnp.einsum("zhij,hk->zhikj", w4, eye).reshape(GATE_HALVES, MXU_DIM, MXU_DIM)

    return jnp.concatenate([blockdiag(wa), blockdiag(wx)], axis=2).astype(BF16)


def _s5_in_weights(bbr, bbi):
    per_chunk = LANES // (2 * SSM_H)
    chunks = SSM_W // LANES
    bb = jnp.stack([bbr, bbi], axis=0).reshape(2, chunks, per_chunk, 2, SSM_H, SSM_P)
    eye_s = jnp.eye(per_chunk, dtype=F32)
    eye_g = jnp.eye(2, dtype=F32)
    w = jnp.einsum("rcsghp,sS,gG->csghSrGp", bb, eye_s, eye_g)
    return w.reshape(chunks, LANES, per_chunk * 2 * 2 * SSM_P).astype(BF16)


def _s5_out_weights(c_re, c_im):
    per_chunk = LANES // (2 * SSM_H)
    nrows = SSM_G // 2
    cc = jnp.stack([c_re, -c_im], axis=0).reshape(2, nrows // per_chunk, per_chunk, 2, SSM_H, SSM_P)
    eye_s = jnp.eye(per_chunk, dtype=F32)
    eye_g = jnp.eye(2, dtype=F32)
    w = jnp.einsum("rcsghp,sS,gG->csrgpSGh", cc, eye_s, eye_g)
    return w.reshape(nrows, 2 * 2 * SSM_P, LANES).astype(BF16)


def _ssm_to_slabs(re, im):
    b = re.shape[0]
    return jnp.concatenate([re.reshape(b, 2, SUBLANES, LANES), im.reshape(b, 2, SUBLANES, LANES)], axis=1)


def _slabs_to_ssm(slabs):
    b = slabs.shape[0]
    return (slabs[:, 0:2].reshape(b, SSM_G, SSM_P), slabs[:, 2:4].reshape(b, SSM_G, SSM_P))


def _conv_to_tile(conv):
    return jnp.pad(conv, ((0, 0), (SUBLANES - (CONV_W - 1), 0), (0, 0)))


def _slab_lambda(a):
    return a.reshape(2, SUBLANES, LANES)


def kernel(x_prompt, x_sample, state_conv, state_lru, state_ssm_re, state_ssm_im, norm_mix, w_in, conv_w, conv_b, lru_wa, lru_ba, lru_wx, lru_bx, lru_lambda, ssm_lambda_re, ssm_lambda_im, ssm_b_re, ssm_b_im, ssm_c_re, ssm_c_im, ssm_d, ssm_log_dt, ssm_w_glu, ssm_b_glu, norm_lru_out, norm_ssm_out, w_out, norm_ffn, w_gate, w_up, w_down, norm_final):
    depth = w_in.shape[0]
    bp, tp, _ = x_prompt.shape
    bs, ts, _ = x_sample.shape
    assert bp == 1

    lbr, lbi, bbr, bbi = _s5_prep(ssm_lambda_re, ssm_lambda_im, ssm_log_dt, ssm_b_re, ssm_b_im)

    def row(v):
        return v.reshape(1, -1)

    layers = []
    for l in range(depth):
        layers.append(dict(
            gmix=row(norm_mix[l]), win=w_in[l].astype(BF16), convw=conv_w[l], convb=row(conv_b[l]),
            wgate=_gate_weights(lru_wa[l], lru_wx[l]),
            bgate=jnp.stack([lru_ba[l], lru_bx[l]], axis=0), lam=row(lru_lambda[l]),
            lbr=_slab_lambda(lbr[l]), lbi=_slab_lambda(lbi[l]),
            bw=_s5_in_weights(bbr[l], bbi[l]),
            cw=_s5_out_weights(ssm_c_re[l], ssm_c_im[l]),
            dskip=row(ssm_d[l]), wglu=ssm_w_glu[l].astype(BF16), bglu=row(ssm_b_glu[l]),
            glru=row(norm_lru_out[l]), gssm=row(norm_ssm_out[l]), wout=w_out[l].astype(BF16),
            gffn=row(norm_ffn[l]), wg=w_gate[l].astype(BF16), wu=w_up[l].astype(BF16),
            wd=w_down[l].astype(BF16)))
    gfin = row(norm_final)

    yp = x_prompt.reshape(bp * tp, D_MODEL)
    ys = x_sample.reshape(bs * ts, D_MODEL)
    conv_p0 = jnp.zeros((bp, SUBLANES, LRU_W), F32)
    lru_p0 = jnp.zeros((bp, LRU_W), F32)
    ssm_p0 = jnp.zeros((bp, SSM_SLABS, SUBLANES, LANES), F32)

    prompt_rows = 512
    ffn_tile = 512
    outs = [[] for _ in range(8)]
    for l in range(depth):
        w = layers[l]
        last = l == depth - 1
        yp, c1, h1, s1 = _mixer_call(yp, conv_p0, lru_p0, ssm_p0, w, 1, prompt_rows)
        yp = _ffn_call(yp, w, gfin, last, ffn_tile)
        ys, c2, h2, s2 = _mixer_call(ys, _conv_to_tile(state_conv[l]), state_lru[l],
                                     _ssm_to_slabs(state_ssm_re[l], state_ssm_im[l]), w, bs, ts)
        ys = _ffn_call(ys, w, gfin, last, bs * ts)
        r1, i1 = _slabs_to_ssm(s1)
        r2, i2 = _slabs_to_ssm(s2)
        for lst, v in zip(outs, (c1[:, SUBLANES - (CONV_W - 1):], h1, r1, i1,
                                 c2[:, SUBLANES - (CONV_W - 1):], h2, r2, i2)):
            lst.append(v)

    return (yp.reshape(bp, tp, D_MODEL), ys.reshape(bs, ts, D_MODEL)) + tuple(jnp.stack(v) for v in outs)
```

```python
import collections
import functools
import math

import jax
import jax.numpy as jnp
from jax import lax
from jax.experimental import pallas as pl
from jax.experimental.pallas import tpu as pltpu

D_MODEL = 1024
LRU_W = 512
LRU_HEADS = 8
LRU_HD = LRU_W // LRU_HEADS
CONV_W = 4
LRU_C = 8.0
SSM_W = 512
SSM_H = 16
SSM_G = SSM_W // SSM_H
SSM_P = 64
IN_W = 2 * LRU_W + SSM_W
D_FF = 2816
EPS = 1e-6

SUBLANES = 8
LANES = 128
MXU_DIM = 256
VMEM_LIMIT_BYTES = 56 * 1024 * 1024

SSM_SLABS = 4
S5_ROWS_PER_CHUNK = LANES // (2 * SSM_H)
S5_TIME_PITCH = 9
S5_ROW_PITCH = 2
LRU_CHUNKS = LRU_W // LANES
CONV_PITCH = 2
CONV_BASE = CONV_PITCH * SUBLANES
GATE_HALVES = LRU_W // MXU_DIM
FF_CHUNK = MXU_DIM
PROMPT_CHUNK = 256
FFN_TILE = 512
BF16 = jnp.bfloat16
F32 = jnp.float32

MixerWeights = collections.namedtuple(
    "MixerWeights",
    "gmix win convw convb wgate bgate lam lbr lbi bw cw dskip wglu bglu glru gssm wout")
ChunkBufs = collections.namedtuple("ChunkBufs", "xs a b gl us")
Task = collections.namedtuple("Task", "fn mxu vpu")


def _rmsnorm(x, gain):
    var = jnp.mean(x * x, axis=-1, keepdims=True)
    return x * lax.rsqrt(var + EPS) * gain


def _dot(a, b):
    return jnp.dot(a, b, preferred_element_type=F32)


_GELU_K1 = -2.0 * math.sqrt(2.0 / math.pi) * math.log2(math.e)
_GELU_K2 = _GELU_K1 * 0.044715


def _gelu(x):
    return x / (1.0 + jnp.exp2(x * (_GELU_K1 + _GELU_K2 * (x * x))))


def _keep_if(pred, new, old):
    return new if pred is None else jnp.where(pred, new, old)


def _lanes(c):
    return slice(c * LANES, (c + 1) * LANES)


def _s5_prep_kernel(lr_ref, li_ref, ldt_ref, br_ref, bi_ref, lbr_ref, lbi_ref, bbr_ref, bbi_ref):
    lr = lr_ref[...]
    li = li_ref[...]
    dt = jnp.exp(ldt_ref[...])
    mag = jnp.exp(lr * dt)
    lbr = mag * jnp.cos(li * dt)
    lbi = mag * jnp.sin(li * dt)
    nr, ni = lbr - 1.0, lbi
    den = lr * lr + li * li
    gr = (nr * lr + ni * li) / den
    gi = (ni * lr - nr * li) / den
    br = br_ref[...]
    bi = bi_ref[...]
    lbr_ref[...] = lbr
    lbi_ref[...] = lbi
    bbr_ref[...] = gr * br - gi * bi
    bbi_ref[...] = gr * bi + gi * br


def _s5_prep(lam_re, lam_im, log_dt, b_re, b_im):
    depth = lam_re.shape[0]
    rows = depth * SSM_G * SSM_H

    def expand(a):
        return jnp.broadcast_to(a[:, :, None, :], (depth, SSM_G, SSM_H, SSM_P)).reshape(rows, SSM_P)

    ldt = jnp.broadcast_to(log_dt[:, :, None, None], (depth, SSM_G, SSM_H, SSM_P)).reshape(rows, SSM_P)
    br = jnp.swapaxes(b_re, 2, 3).reshape(rows, SSM_P)
    bi = jnp.swapaxes(b_im, 2, 3).reshape(rows, SSM_P)
    sds = jax.ShapeDtypeStruct((rows, SSM_P), F32)
    lbr, lbi, bbr, bbi = pl.pallas_call(
        _s5_prep_kernel, out_shape=(sds, sds, sds, sds), name="s5_prep",
    )(expand(lam_re), expand(lam_im), ldt, br, bi)
    shape4 = (depth, SSM_G, SSM_H, SSM_P)
    lbr = lbr.reshape(shape4)[:, :, 0, :]
    lbi = lbi.reshape(shape4)[:, :, 0, :]
    return lbr, lbi, bbr.reshape(shape4), bbi.reshape(shape4)


def _s5_time_rows(srow, rows):
    return pl.ds(S5_ROW_PITCH * srow, rows, stride=S5_TIME_PITCH)


def _s5_step_rows(r):
    return pl.ds(S5_TIME_PITCH * r, SUBLANES, stride=S5_ROW_PITCH)


def _stage_a_tasks(nb, seq, x_ref, w, convc_ref, xp_ref, bufs, live):
    rows = nb * seq
    vregs = rows // SUBLANES
    st = {}
    tasks = []

    def norm():
        st["h"] = _rmsnorm(x_ref[...], w.gmix[...]).astype(BF16)
    tasks.append(Task(norm, 0, 14 * vregs))

    def in_proj(k):
        st["z", k] = _dot(st["h"], w.win[:, k * MXU_DIM:(k + 1) * MXU_DIM])
        if k in (2, 3):
            bufs.gl[:, (k - 2) * MXU_DIM:(k - 1) * MXU_DIM] = st["z", k]
        if k in (4, 5):
            bufs.us[:, (k - 4) * MXU_DIM:(k - 3) * MXU_DIM] = st["z", k]
    for k in range(IN_W // MXU_DIM):
        tasks.append(Task(functools.partial(in_proj, k), D_MODEL // MXU_DIM * vregs // 2 * 4, 0))

    def conv(c):
        zk, off = divmod(c * LANES, MXU_DIM)
        xl = st["z", zk][:, off:off + LANES]
        cw = w.convw[:, _lanes(c)]
        parts = []
        for j in range(nb):
            xl_j = xl[j * seq:(j + 1) * seq, :]
            xp_ref[j, c, pl.ds(0, SUBLANES, stride=CONV_PITCH), :] = convc_ref[j, :, _lanes(c)]
            xp_ref[j, c, pl.ds(CONV_BASE, seq, stride=CONV_PITCH), :] = xl_j
            acc = w.convb[:, _lanes(c)] + xl_j * cw[CONV_W - 1:CONV_W, :]
            for k in range(CONV_W - 1):
                start = CONV_BASE - CONV_PITCH * (CONV_W - 1 - k)
                acc = acc + xp_ref[j, c, pl.ds(start, seq, stride=CONV_PITCH), :] * cw[k:k + 1, :]
            parts.append(acc)
            convc_ref[j, :, _lanes(c)] = _keep_if(live, xl_j[seq - SUBLANES:, :], convc_ref[j, :, _lanes(c)])
        st["xc", c] = parts[0] if nb == 1 else jnp.concatenate(parts, axis=0)
    for c in range(LRU_CHUNKS):
        tasks.append(Task(functools.partial(conv, c), 0, 2 * vregs))

    def gates(hf):
        lo, hi = hf * MXU_DIM, (hf + 1) * MXU_DIM
        chunks = range(hf * MXU_DIM // LANES, (hf + 1) * MXU_DIM // LANES)
        xc = jnp.concatenate([st["xc", c] for c in chunks], axis=1)
        lam = w.lam[:, lo:hi]
        neg_c_softplus = -LRU_C * (jnp.maximum(-lam, 0.0) + jnp.log1p(jnp.exp(-jnp.abs(lam))))
        pre = _dot(xc.astype(BF16), w.wgate[hf])
        r = jax.nn.sigmoid(pre[:, :MXU_DIM] + w.bgate[0:1, lo:hi])
        ig = jax.nn.sigmoid(pre[:, MXU_DIM:] + w.bgate[1:2, lo:hi])
        log_a = neg_c_softplus * r
        a = jnp.exp(log_a)
        one_minus_a2 = -jnp.tanh(log_a) * (a * a + 1.0)
        b = jnp.sqrt(one_minus_a2) * (ig * xc)
        for i, c in enumerate(chunks):
            bufs.a[pl.ds(c, rows, stride=LRU_CHUNKS), :] = a[:, _lanes(i)]
            bufs.b[pl.ds(c, rows, stride=LRU_CHUNKS), :] = b[:, _lanes(i)]
    for hf in range(GATE_HALVES):
        tasks.append(Task(functools.partial(gates, hf), vregs // 2 * 2 * 4, 11 * vregs))

    def s5_in(c):
        zk, off = divmod(c * LANES, MXU_DIM)
        us_bf = st["z", 4 + zk][:, off:off + LANES].astype(BF16)
        bu = _dot(us_bf, w.bw[c])
        half = (c * S5_ROWS_PER_CHUNK) // SUBLANES
        for sl in range(S5_ROWS_PER_CHUNK):
            srow = (c * S5_ROWS_PER_CHUNK + sl) % SUBLANES
            for reim in range(2):
                col = (sl * 2 + reim) * LANES
                bufs.xs[reim * 2 + half, _s5_time_rows(srow, rows), :] = bu[:, col:col + LANES]
    for c in range(SSM_W // LANES):
        tasks.append(Task(functools.partial(s5_in, c), vregs // 2 * 4 * 4, vregs // 2))
    return tasks


def _stage_l_tasks(nb, seq, w, lruc_ref, ssmc_ref, bufs, live):
    st = {}
    tasks = []
    row = lambda: lax.broadcasted_iota(jnp.int32, (SUBLANES, LANES), 0)

    def begin(j):
        st["lb"] = ((w.lbr[0], w.lbr[1]), (w.lbi[0], w.lbi[1]))
        st["xr"] = [ssmc_ref[j, 0], ssmc_ref[j, 1]]
        st["xi"] = [ssmc_ref[j, 2], ssmc_ref[j, 3]]
        st["h"] = lruc_ref[j]
        st["old"] = (tuple(st["xr"]), tuple(st["xi"]), st["h"])

    def steps(j, tile):
        lbr, lbi = st["lb"]
        xr, xi = st["xr"], st["xi"]
        for t in range(tile * SUBLANES, (tile + 1) * SUBLANES):
            rows_t = _s5_step_rows(j * seq + t)
            for hv in range(2):
                nr = lbr[hv] * xr[hv] - lbi[hv] * xi[hv] + bufs.xs[hv, rows_t, :]
                ni = lbr[hv] * xi[hv] + lbi[hv] * xr[hv] + bufs.xs[2 + hv, rows_t, :]
                bufs.xs[hv, rows_t, :] = nr
                bufs.xs[2 + hv, rows_t, :] = ni
                xr[hv], xi[hv] = nr, ni
        first_half = row() < LRU_CHUNKS
        h = st["h"]
        for pair in range(tile * SUBLANES // 2, (tile + 1) * SUBLANES // 2):
            r0 = (j * seq + 2 * pair) * LRU_CHUNKS
            a2 = bufs.a[r0:r0 + SUBLANES, :]
            b2 = bufs.b[r0:r0 + SUBLANES, :]
            h_even = a2 * h + b2
            h_odd = a2 * pltpu.roll(h_even, LRU_CHUNKS, 0) + b2
            bufs.b[r0:r0 + SUBLANES, :] = jnp.where(first_half, h_even, h_odd)
            h = pltpu.roll(h_odd, LRU_CHUNKS, 0)
        st["h"] = h

    def end(j):
        old = st["old"]
        for hv in range(2):
            ssmc_ref[j, hv] = _keep_if(live, st["xr"][hv], old[0][hv])
            ssmc_ref[j, 2 + hv] = _keep_if(live, st["xi"][hv], old[1][hv])
        lruc_ref[j] = _keep_if(live, st["h"], old[2])

    for j in range(nb):
        tasks.append(Task(functools.partial(begin, j), 0, 1))
        for tile in range(seq // SUBLANES):
            tasks.append(Task(functools.partial(steps, j, tile), 0, 56))
        tasks.append(Task(functools.partial(end, j), 0, 1))
    return tasks


def _stage_c_tasks(rows, x_ref, y_ref, w, bufs):
    vregs = rows // SUBLANES
    st = {}
    tasks = []

    def lru_out():
        h = jnp.concatenate([bufs.b[pl.ds(c, rows, stride=LRU_CHUNKS), :] for c in range(LRU_CHUNKS)], axis=1)
        st["n_lru"] = _rmsnorm(h * _gelu(bufs.gl[...]), w.glru[...]).astype(BF16)

    def s5_out(c):
        acc = None
        for sl in range(S5_ROWS_PER_CHUNK):
            s = c * S5_ROWS_PER_CHUNK + sl
            half, srow = s // SUBLANES, s % SUBLANES
            xr = bufs.xs[half, _s5_time_rows(srow, rows), :]
            xi = bufs.xs[2 + half, _s5_time_rows(srow, rows), :]
            xcat = jnp.concatenate([xr, xi], axis=1).astype(BF16)
            part = _dot(xcat, w.cw[s])
            acc = part if acc is None else acc + part
        st["g", c] = _gelu(acc + w.dskip[:, _lanes(c)] * bufs.us[:, _lanes(c)])
    for c in range(SSM_W // LANES):
        tasks.append(Task(functools.partial(s5_out, c), vregs // 2 * 4 * 4, 4 * vregs))
    tasks.append(Task(lru_out, 0, 14 * vregs))

    def glu():
        g = jnp.concatenate([st["g", c] for c in range(SSM_W // LANES)], axis=1)
        gate = _dot(g.astype(BF16), w.wglu[...]) + w.bglu[...]
        st["n_ssm"] = _rmsnorm(g * jax.nn.sigmoid(gate), w.gssm[...]).astype(BF16)
    tasks.append(Task(glu, vregs // 2 * 4 * 4, 12 * vregs))

    def out_proj(k):
        cols = slice(k * MXU_DIM, (k + 1) * MXU_DIM)
        y_ref[:, cols] = (x_ref[:, cols] + _dot(st["n_lru"], w.wout[0:LRU_W, cols])
                          + _dot(st["n_ssm"], w.wout[LRU_W:, cols]))
    for k in range(D_MODEL // MXU_DIM):
        tasks.append(Task(functools.partial(out_proj, k), vregs // 2 * 4 * 4, vregs))
    return tasks


INTERLEAVE_BAND = 0.12


def _run_interleaved(*queues):
    queues = [list(q) for q in queues]
    cost = lambda t: max(t.mxu, t.vpu)
    totals = [float(sum(cost(t) for t in q)) for q in queues]
    total_mxu = float(sum(t.mxu for q in queues for t in q))
    total_vpu = float(sum(t.vpu for q in queues for t in q))
    done = [0.0] * len(queues)
    run_mxu = run_vpu = 0.0
    while any(queues):
        overall = sum(done) / sum(totals)
        live = [i for i, q in enumerate(queues) if q]
        pool = ([i for i in live if done[i] / totals[i] < overall - INTERLEAVE_BAND]
                or [i for i in live if done[i] / totals[i] <= overall + INTERLEAVE_BAND] or live)
        pick = min(pool, key=lambda i: abs((run_mxu + queues[i][0].mxu) / total_mxu
                                           - (run_vpu + queues[i][0].vpu) / total_vpu))
        task = queues[pick].pop(0)
        done[pick] += cost(task)
        run_mxu += task.mxu
        run_vpu += task.vpu
        task.fn()


N_WEIGHTS = len(MixerWeights._fields)
N_BUFS = len(ChunkBufs._fields)


def _mixer_plain_kernel(nb, seq, x_ref, conv0_ref, lru0_ref, ssm0_ref, *rest):
    w = MixerWeights(*rest[:N_WEIGHTS])
    y_ref, convc_ref, lruc_ref, ssmc_ref, xp_ref = rest[N_WEIGHTS:N_WEIGHTS + 5]
    bufs = ChunkBufs(*rest[N_WEIGHTS + 5:])
    convc_ref[...] = conv0_ref[...]
    lruc_ref[...] = lru0_ref[...]
    ssmc_ref[...] = ssm0_ref[...]
    for task in (_stage_a_tasks(nb, seq, x_ref, w, convc_ref, xp_ref, bufs, None)
                 + _stage_l_tasks(nb, seq, w, lruc_ref, ssmc_ref, bufs, None)
                 + _stage_c_tasks(nb * seq, x_ref, y_ref, w, bufs)):
        task.fn()


def _mixer_pipelined_kernel(n_chunks, seq, x_cur_ref, x_prev_ref, conv0_ref, lru0_ref, ssm0_ref, *rest):
    w = MixerWeights(*rest[:N_WEIGHTS])
    y_ref, convc_ref, lruc_ref, ssmc_ref, xp_ref = rest[N_WEIGHTS:N_WEIGHTS + 5]
    sets = (ChunkBufs(*rest[N_WEIGHTS + 5:N_WEIGHTS + 5 + N_BUFS]),
            ChunkBufs(*rest[N_WEIGHTS + 5 + N_BUFS:]))
    step = pl.program_id(0)

    @pl.when(step == 0)
    def _():
        convc_ref[...] = conv0_ref[...]
        lruc_ref[...] = lru0_ref[...]
        ssmc_ref[...] = ssm0_ref[...]
        for ref in sets[1]:
            ref[...] = jnp.zeros(ref.shape, ref.dtype)

    def body(write, read):
        _run_interleaved(
            _stage_l_tasks(1, seq, w, lruc_ref, ssmc_ref, read, step >= 1)
            + _stage_c_tasks(seq, x_prev_ref, y_ref, w, read),
            _stage_a_tasks(1, seq, x_cur_ref, w, convc_ref, xp_ref, write, step < n_chunks))

    @pl.when(step % 2 == 0)
    def _():
        body(sets[0], sets[1])

    @pl.when(step % 2 == 1)
    def _():
        body(sets[1], sets[0])


def _full_spec(shape):
    zeros = (0,) * len(shape)
    return pl.BlockSpec(shape, lambda i, _z=zeros: _z)


def _chunk_scratch(rows):
    xs_rows = -(-(S5_TIME_PITCH * rows + S5_ROW_PITCH * SUBLANES) // SUBLANES) * SUBLANES
    return [pltpu.VMEM((SSM_SLABS, xs_rows, LANES), F32),
            pltpu.VMEM((rows * LRU_CHUNKS, LANES), F32), pltpu.VMEM((rows * LRU_CHUNKS, LANES), F32),
            pltpu.VMEM((rows, LRU_W), F32), pltpu.VMEM((rows, SSM_W), F32)]


def _mixer_call(x2d, conv0, lru0, ssm0, w, nb, seq, pipelined):
    total = x2d.shape[0]
    rows = nb * seq
    state_specs = [_full_spec(conv0.shape), _full_spec(lru0.shape), _full_spec(ssm0.shape)]
    weight_specs = [_full_spec(a.shape) for a in w]
    out_shape = (jax.ShapeDtypeStruct((total, D_MODEL), F32),
                 jax.ShapeDtypeStruct(conv0.shape, F32),
                 jax.ShapeDtypeStruct(lru0.shape, F32),
                 jax.ShapeDtypeStruct(ssm0.shape, F32))
    conv_scratch = pltpu.VMEM((nb, LRU_CHUNKS, CONV_PITCH * seq + CONV_BASE, LANES), F32)
    scratch = [conv_scratch] + _chunk_scratch(rows)
    params = pltpu.CompilerParams(dimension_semantics=("arbitrary",), vmem_limit_bytes=VMEM_LIMIT_BYTES)
    if not pipelined:
        assert total == rows
        x_spec = pl.BlockSpec((rows, D_MODEL), lambda i: (0, 0))
        return pl.pallas_call(
            functools.partial(_mixer_plain_kernel, nb, seq),
            grid=(1,), in_specs=[x_spec] + state_specs + weight_specs,
            out_specs=(x_spec,) + tuple(state_specs), out_shape=out_shape,
            scratch_shapes=scratch, name="mixer_plain", compiler_params=params,
        )(x2d, conv0, lru0, ssm0, *w)
    assert nb == 1 and total % rows == 0
    n_chunks = total // rows
    cur_spec = pl.BlockSpec((rows, D_MODEL), lambda i: (jnp.minimum(i, n_chunks - 1), 0))
    prev_spec = pl.BlockSpec((rows, D_MODEL), lambda i: (jnp.maximum(i - 1, 0), 0))
    return pl.pallas_call(
        functools.partial(_mixer_pipelined_kernel, n_chunks, seq),
        grid=(n_chunks + 1,), in_specs=[cur_spec, prev_spec] + state_specs + weight_specs,
        out_specs=(prev_spec,) + tuple(state_specs), out_shape=out_shape,
        scratch_shapes=scratch + _chunk_scratch(rows), name="mixer", compiler_params=params,
    )(x2d, x2d, conv0, lru0, ssm0, *w)


def _ffn_kernel(final_norm, x_ref, gffn_ref, wg_ref, wu_ref, wd_ref, gfin_ref, o_ref):
    x = x_ref[...]
    h = _rmsnorm(x, gffn_ref[...]).astype(BF16)
    acc = x
    for c in range(D_FF // FF_CHUNK):
        lo, hi = c * FF_CHUNK, (c + 1) * FF_CHUNK
        gate = _dot(h, wg_ref[:, lo:hi])
        up = _dot(h, wu_ref[:, lo:hi])
        act = (gate * jax.nn.sigmoid(gate) * up).astype(BF16)
        acc = acc + _dot(act, wd_ref[lo:hi, :])
    if final_norm:
        acc = _rmsnorm(acc, gfin_ref[...])
    o_ref[...] = acc


def _ffn_call(x2d, w, gfin, final_norm, tile):
    total = x2d.shape[0]
    weights = (w["gffn"], w["wg"], w["wu"], w["wd"], gfin)
    in_specs = [pl.BlockSpec((tile, D_MODEL), lambda i: (i, 0))]
    in_specs += [_full_spec(a.shape) for a in weights]
    return pl.pallas_call(
        functools.partial(_ffn_kernel, final_norm),
        grid=(total // tile,), in_specs=in_specs,
        out_specs=pl.BlockSpec((tile, D_MODEL), lambda i: (i, 0)),
        out_shape=jax.ShapeDtypeStruct((total, D_MODEL), F32), name="ffn",
        compiler_params=pltpu.CompilerParams(dimension_semantics=("arbitrary",),
                                             vmem_limit_bytes=VMEM_LIMIT_BYTES),
    )(x2d, *weights)


def _gate_weights(wa, wx):
    per = LRU_HEADS // GATE_HALVES
    eye = jnp.eye(per, dtype=F32)

    def blockdiag(w):
        w4 = w.reshape(GATE_HALVES, per, LRU_HD, LRU_HD)
        return jnp.einsum("zhij,hk->zhikj", w4, eye).reshape(GATE_HALVES, MXU_DIM, MXU_DIM)

    return jnp.concatenate([blockdiag(wa), blockdiag(wx)], axis=2).astype(BF16)


def _s5_in_weights(bbr, bbi):
    chunks = SSM_W // LANES
    bb = jnp.stack([bbr, bbi], axis=0).reshape(2, chunks, S5_ROWS_PER_CHUNK, 2, SSM_H, SSM_P)
    eye_s = jnp.eye(S5_ROWS_PER_CHUNK, dtype=F32)
    eye_g = jnp.eye(2, dtype=F32)
    w = jnp.einsum("rcsghp,sS,gG->csghSrGp", bb, eye_s, eye_g)
    return w.reshape(chunks, LANES, S5_ROWS_PER_CHUNK * 2 * 2 * SSM_P).astype(BF16)


def _s5_out_weights(c_re, c_im):
    nrows = SSM_G // 2
    cc = jnp.stack([c_re, -c_im], axis=0).reshape(
        2, nrows // S5_ROWS_PER_CHUNK, S5_ROWS_PER_CHUNK, 2, SSM_H, SSM_P)
    eye_s = jnp.eye(S5_ROWS_PER_CHUNK, dtype=F32)
    eye_g = jnp.eye(2, dtype=F32)
    w = jnp.einsum("rcsghp,sS,gG->csrgpSGh", cc, eye_s, eye_g)
    return w.reshape(nrows, 2 * 2 * SSM_P, LANES).astype(BF16)


def _ssm_to_slabs(re, im):
    b = re.shape[0]
    return jnp.concatenate([re.reshape(b, 2, SUBLANES, LANES), im.reshape(b, 2, SUBLANES, LANES)], axis=1)


def _slabs_to_ssm(slabs):
    b = slabs.shape[0]
    return (slabs[:, 0:2].reshape(b, SSM_G, SSM_P), slabs[:, 2:4].reshape(b, SSM_G, SSM_P))


def _conv_to_tile(conv):
    return jnp.pad(conv, ((0, 0), (SUBLANES - (CONV_W - 1), 0), (0, 0)))


def _lru_to_tile(h):
    b = h.shape[0]
    return jnp.pad(h.reshape(b, LRU_CHUNKS, LANES), ((0, 0), (0, SUBLANES - LRU_CHUNKS), (0, 0)))


def _tile_to_lru(tile):
    return tile[:, :LRU_CHUNKS].reshape(tile.shape[0], LRU_W)


def _slab_lambda(a):
    return a.reshape(2, SUBLANES, LANES)


def kernel(x_prompt, x_sample, state_conv, state_lru, state_ssm_re, state_ssm_im, norm_mix, w_in, conv_w, conv_b, lru_wa, lru_ba, lru_wx, lru_bx, lru_lambda, ssm_lambda_re, ssm_lambda_im, ssm_b_re, ssm_b_im, ssm_c_re, ssm_c_im, ssm_d, ssm_log_dt, ssm_w_glu, ssm_b_glu, norm_lru_out, norm_ssm_out, w_out, norm_ffn, w_gate, w_up, w_down, norm_final):
    depth = w_in.shape[0]
    bp, tp, _ = x_prompt.shape
    bs, ts, _ = x_sample.shape
    assert bp == 1

    lbr, lbi, bbr, bbi = _s5_prep(ssm_lambda_re, ssm_lambda_im, ssm_log_dt, ssm_b_re, ssm_b_im)

    def row(v):
        return v.reshape(1, -1)

    mixers, ffns = [], []
    for l in range(depth):
        mixers.append(MixerWeights(
            gmix=row(norm_mix[l]), win=w_in[l].astype(BF16), convw=conv_w[l], convb=row(conv_b[l]),
            wgate=_gate_weights(lru_wa[l], lru_wx[l]),
            bgate=jnp.stack([lru_ba[l], lru_bx[l]], axis=0), lam=row(lru_lambda[l]),
            lbr=_slab_lambda(lbr[l]), lbi=_slab_lambda(lbi[l]),
            bw=_s5_in_weights(bbr[l], bbi[l]),
            cw=_s5_out_weights(ssm_c_re[l], ssm_c_im[l]),
            dskip=row(ssm_d[l]), wglu=ssm_w_glu[l].astype(BF16), bglu=row(ssm_b_glu[l]),
            glru=row(norm_lru_out[l]), gssm=row(norm_ssm_out[l]), wout=w_out[l].astype(BF16)))
        ffns.append(dict(gffn=row(norm_ffn[l]), wg=w_gate[l].astype(BF16), wu=w_up[l].astype(BF16),
                         wd=w_down[l].astype(BF16)))
    gfin = row(norm_final)

    yp = x_prompt.reshape(bp * tp, D_MODEL)
    ys = x_sample.reshape(bs * ts, D_MODEL)
    conv_p0 = jnp.zeros((bp, SUBLANES, LRU_W), F32)
    lru_p0 = jnp.zeros((bp, SUBLANES, LANES), F32)
    ssm_p0 = jnp.zeros((bp, SSM_SLABS, SUBLANES, LANES), F32)

    outs = [[] for _ in range(8)]
    for l in range(depth):
        last = l == depth - 1
        yp, c1, h1, s1 = _mixer_call(yp, conv_p0, lru_p0, ssm_p0, mixers[l], 1, PROMPT_CHUNK, True)
        yp = _ffn_call(yp, ffns[l], gfin, last, FFN_TILE)
        ys, c2, h2, s2 = _mixer_call(ys, _conv_to_tile(state_conv[l]), _lru_to_tile(state_lru[l]),
                                     _ssm_to_slabs(state_ssm_re[l], state_ssm_im[l]), mixers[l], bs, ts, False)
        ys = _ffn_call(ys, ffns[l], gfin, last, bs * ts)
        r1, i1 = _slabs_to_ssm(s1)
        r2, i2 = _slabs_to_ssm(s2)
        for lst, v in zip(outs, (c1[:, SUBLANES - (CONV_W - 1):], _tile_to_lru(h1), r1, i1,
                                 c2[:, SUBLANES - (CONV_W - 1):], _tile_to_lru(h2), r2, i2)):
            lst.append(v)

    return (yp.reshape(bp, tp, D_MODEL), ys.reshape(bs, ts, D_MODEL)) + tuple(jnp.stack(v) for v in outs)
```

```python
import collections
import functools
import math

import jax
import jax.numpy as jnp
from jax import lax
from jax.experimental import pallas as pl
from jax.experimental.pallas import tpu as pltpu

D_MODEL = 1024
LRU_W = 512
LRU_HEADS = 8
LRU_HD = LRU_W // LRU_HEADS
CONV_W = 4
LRU_C = 8.0
SSM_W = 512
SSM_H = 16
SSM_G = SSM_W // SSM_H
SSM_P = 64
IN_W = 2 * LRU_W + SSM_W
D_FF = 2816
EPS = 1e-6

SUBLANES = 8
LANES = 128
MXU_DIM = 256
VMEM_LIMIT_BYTES = 56 * 1024 * 1024

SSM_SLABS = 4
S5_ROWS_PER_CHUNK = LANES // (2 * SSM_H)
S5_TIME_PITCH = 9
S5_ROW_PITCH = 2
LRU_CHUNKS = LRU_W // LANES
CONV_PITCH = 2
CONV_BASE = CONV_PITCH * SUBLANES
GATE_HALVES = LRU_W // MXU_DIM
FF_CHUNK = MXU_DIM
PROMPT_CHUNK = 256
FFN_TILE = 512
BF16 = jnp.bfloat16
F32 = jnp.float32

MixerWeights = collections.namedtuple(
    "MixerWeights",
    "gmix win convw convb wgate bgate lam lbr lbi bw cw dskip wglu bglu glru gssm wout")
ChunkBufs = collections.namedtuple("ChunkBufs", "xs a b gl us")
Task = collections.namedtuple("Task", "fn mxu vpu")


def _rmsnorm(x, gain):
    var = jnp.mean(x * x, axis=-1, keepdims=True)
    return x * lax.rsqrt(var + EPS) * gain


def _dot(a, b):
    return jnp.dot(a, b, preferred_element_type=F32)


_GELU_K1 = -2.0 * math.sqrt(2.0 / math.pi) * math.log2(math.e)
_GELU_K2 = _GELU_K1 * 0.044715


def _gelu(x):
    return x / (1.0 + jnp.exp2(x * (_GELU_K1 + _GELU_K2 * (x * x))))


def _keep_if(pred, new, old):
    return new if pred is None else jnp.where(pred, new, old)


def _lanes(c):
    return slice(c * LANES, (c + 1) * LANES)


def _s5_prep_kernel(lr_ref, li_ref, ldt_ref, br_ref, bi_ref, lbr_ref, lbi_ref, bbr_ref, bbi_ref):
    lr = lr_ref[...]
    li = li_ref[...]
    dt = jnp.exp(ldt_ref[...])
    mag = jnp.exp(lr * dt)
    lbr = mag * jnp.cos(li * dt)
    lbi = mag * jnp.sin(li * dt)
    nr, ni = lbr - 1.0, lbi
    den = lr * lr + li * li
    gr = (nr * lr + ni * li) / den
    gi = (ni * lr - nr * li) / den
    br = br_ref[...]
    bi = bi_ref[...]
    lbr_ref[...] = lbr
    lbi_ref[...] = lbi
    bbr_ref[...] = gr * br - gi * bi
    bbi_ref[...] = gr * bi + gi * br


def _s5_prep(lam_re, lam_im, log_dt, b_re, b_im):
    depth = lam_re.shape[0]
    rows = depth * SSM_G * SSM_H

    def expand(a):
        return jnp.broadcast_to(a[:, :, None, :], (depth, SSM_G, SSM_H, SSM_P)).reshape(rows, SSM_P)

    ldt = jnp.broadcast_to(log_dt[:, :, None, None], (depth, SSM_G, SSM_H, SSM_P)).reshape(rows, SSM_P)
    br = jnp.swapaxes(b_re, 2, 3).reshape(rows, SSM_P)
    bi = jnp.swapaxes(b_im, 2, 3).reshape(rows, SSM_P)
    sds = jax.ShapeDtypeStruct((rows, SSM_P), F32)
    lbr, lbi, bbr, bbi = pl.pallas_call(
        _s5_prep_kernel, out_shape=(sds, sds, sds, sds), name="s5_prep",
    )(expand(lam_re), expand(lam_im), ldt, br, bi)
    shape4 = (depth, SSM_G, SSM_H, SSM_P)
    lbr = lbr.reshape(shape4)[:, :, 0, :]
    lbi = lbi.reshape(shape4)[:, :, 0, :]
    return lbr, lbi, bbr.reshape(shape4), bbi.reshape(shape4)


def _s5_time_rows(srow, rows):
    return pl.ds(S5_ROW_PITCH * srow, rows, stride=S5_TIME_PITCH)


def _s5_step_rows(r):
    return pl.ds(S5_TIME_PITCH * r, SUBLANES, stride=S5_ROW_PITCH)


def _stage_a_tasks(nb, seq, x_ref, w, convc_ref, xp_ref, bufs, live):
    rows = nb * seq
    vregs = rows // SUBLANES
    st = {}
    tasks = []

    def norm():
        st["h"] = _rmsnorm(x_ref[...], w.gmix[...]).astype(BF16)
    tasks.append(Task(norm, 0, 14 * vregs))

    def in_proj(k):
        st["z", k] = _dot(st["h"], w.win[:, k * MXU_DIM:(k + 1) * MXU_DIM])
        if k in (2, 3):
            bufs.gl[:, (k - 2) * MXU_DIM:(k - 1) * MXU_DIM] = st["z", k]
        if k in (4, 5):
            bufs.us[:, (k - 4) * MXU_DIM:(k - 3) * MXU_DIM] = st["z", k]
    for k in range(IN_W // MXU_DIM):
        tasks.append(Task(functools.partial(in_proj, k), D_MODEL // MXU_DIM * vregs // 2 * 4, 0))

    def conv(c):
        zk, off = divmod(c * LANES, MXU_DIM)
        xl = st["z", zk][:, off:off + LANES]
        cw = w.convw[:, _lanes(c)]
        parts = []
        for j in range(nb):
            xl_j = xl[j * seq:(j + 1) * seq, :]
            xp_ref[j, c, pl.ds(0, SUBLANES, stride=CONV_PITCH), :] = convc_ref[j, :, _lanes(c)]
            xp_ref[j, c, pl.ds(CONV_BASE, seq, stride=CONV_PITCH), :] = xl_j
            acc = w.convb[:, _lanes(c)] + xl_j * cw[CONV_W - 1:CONV_W, :]
            for k in range(CONV_W - 1):
                start = CONV_BASE - CONV_PITCH * (CONV_W - 1 - k)
                acc = acc + xp_ref[j, c, pl.ds(start, seq, stride=CONV_PITCH), :] * cw[k:k + 1, :]
            parts.append(acc)
            convc_ref[j, :, _lanes(c)] = _keep_if(live, xl_j[seq - SUBLANES:, :], convc_ref[j, :, _lanes(c)])
        st["xc", c] = parts[0] if nb == 1 else jnp.concatenate(parts, axis=0)
    for c in range(LRU_CHUNKS):
        tasks.append(Task(functools.partial(conv, c), 0, 2 * vregs))

    def gates(hf):
        lo, hi = hf * MXU_DIM, (hf + 1) * MXU_DIM
        chunks = range(hf * MXU_DIM // LANES, (hf + 1) * MXU_DIM // LANES)
        xc = jnp.concatenate([st["xc", c] for c in chunks], axis=1)
        lam = w.lam[:, lo:hi]
        neg_c_softplus = -LRU_C * (jnp.maximum(-lam, 0.0) + jnp.log1p(jnp.exp(-jnp.abs(lam))))
        pre = _dot(xc.astype(BF16), w.wgate[hf])
        r = jax.nn.sigmoid(pre[:, :MXU_DIM] + w.bgate[0:1, lo:hi])
        ig = jax.nn.sigmoid(pre[:, MXU_DIM:] + w.bgate[1:2, lo:hi])
        log_a = neg_c_softplus * r
        a = jnp.exp(log_a)
        one_minus_a2 = -jnp.tanh(log_a) * (a * a + 1.0)
        b = jnp.sqrt(one_minus_a2) * (ig * xc)
        for i, c in enumerate(chunks):
            bufs.a[pl.ds(c, rows, stride=LRU_CHUNKS), :] = a[:, _lanes(i)]
            bufs.b[pl.ds(c, rows, stride=LRU_CHUNKS), :] = b[:, _lanes(i)]
    for hf in range(GATE_HALVES):
        tasks.append(Task(functools.partial(gates, hf), vregs // 2 * 2 * 4, 11 * vregs))

    def s5_in(c):
        zk, off = divmod(c * LANES, MXU_DIM)
        us_bf = st["z", 4 + zk][:, off:off + LANES].astype(BF16)
        bu = _dot(us_bf, w.bw[c])
        half = (c * S5_ROWS_PER_CHUNK) // SUBLANES
        for sl in range(S5_ROWS_PER_CHUNK):
            srow = (c * S5_ROWS_PER_CHUNK + sl) % SUBLANES
            for reim in range(2):
                col = (sl * 2 + reim) * LANES
                bufs.xs[reim * 2 + half, _s5_time_rows(srow, rows), :] = bu[:, col:col + LANES]
    for c in range(SSM_W // LANES):
        tasks.append(Task(functools.partial(s5_in, c), vregs // 2 * 4 * 4, vregs // 2))
    return tasks


def _stage_l_tasks(nb, seq, w, lruc_ref, ssmc_ref, bufs, live):
    st = {}
    tasks = []
    row = lambda: lax.broadcasted_iota(jnp.int32, (SUBLANES, LANES), 0)

    def begin(j):
        st["lb"] = ((w.lbr[0], w.lbr[1]), (w.lbi[0], w.lbi[1]))
        st["xr"] = [ssmc_ref[j, 0], ssmc_ref[j, 1]]
        st["xi"] = [ssmc_ref[j, 2], ssmc_ref[j, 3]]
        st["h"] = lruc_ref[j]
        st["old"] = (tuple(st["xr"]), tuple(st["xi"]), st["h"])

    def steps(j, tile):
        lbr, lbi = st["lb"]
        xr, xi = st["xr"], st["xi"]
        for t in range(tile * SUBLANES, (tile + 1) * SUBLANES):
            rows_t = _s5_step_rows(j * seq + t)
            for hv in range(2):
                nr = lbr[hv] * xr[hv] - lbi[hv] * xi[hv] + bufs.xs[hv, rows_t, :]
                ni = lbr[hv] * xi[hv] + lbi[hv] * xr[hv] + bufs.xs[2 + hv, rows_t, :]
                bufs.xs[hv, rows_t, :] = nr
                bufs.xs[2 + hv, rows_t, :] = ni
                xr[hv], xi[hv] = nr, ni
        first_half = row() < LRU_CHUNKS
        h = st["h"]
        for pair in range(tile * SUBLANES // 2, (tile + 1) * SUBLANES // 2):
            r0 = (j * seq + 2 * pair) * LRU_CHUNKS
            a2 = bufs.a[r0:r0 + SUBLANES, :]
            b2 = bufs.b[r0:r0 + SUBLANES, :]
            h_even = a2 * h + b2
            h_odd = a2 * pltpu.roll(h_even, LRU_CHUNKS, 0) + b2
            bufs.b[r0:r0 + SUBLANES, :] = jnp.where(first_half, h_even, h_odd)
            h = pltpu.roll(h_odd, LRU_CHUNKS, 0)
        st["h"] = h

    def end(j):
        old = st["old"]
        for hv in range(2):
            ssmc_ref[j, hv] = _keep_if(live, st["xr"][hv], old[0][hv])
            ssmc_ref[j, 2 + hv] = _keep_if(live, st["xi"][hv], old[1][hv])
        lruc_ref[j] = _keep_if(live, st["h"], old[2])

    for j in range(nb):
        tasks.append(Task(functools.partial(begin, j), 0, 1))
        for tile in range(seq // SUBLANES):
            tasks.append(Task(functools.partial(steps, j, tile), 0, 56))
        tasks.append(Task(functools.partial(end, j), 0, 1))
    return tasks


def _stage_c_tasks(rows, x_ref, y_ref, w, bufs):
    vregs = rows // SUBLANES
    st = {}
    tasks = []

    def lru_out():
        h = jnp.concatenate([bufs.b[pl.ds(c, rows, stride=LRU_CHUNKS), :] for c in range(LRU_CHUNKS)], axis=1)
        st["n_lru"] = _rmsnorm(h * _gelu(bufs.gl[...]), w.glru[...]).astype(BF16)

    def s5_out(c):
        acc = None
        for sl in range(S5_ROWS_PER_CHUNK):
            s = c * S5_ROWS_PER_CHUNK + sl
            half, srow = s // SUBLANES, s % SUBLANES
            xr = bufs.xs[half, _s5_time_rows(srow, rows), :]
            xi = bufs.xs[2 + half, _s5_time_rows(srow, rows), :]
            xcat = jnp.concatenate([xr, xi], axis=1).astype(BF16)
            part = _dot(xcat, w.cw[s])
            acc = part if acc is None else acc + part
        st["g", c] = _gelu(acc + w.dskip[:, _lanes(c)] * bufs.us[:, _lanes(c)])
    for c in range(SSM_W // LANES):
        tasks.append(Task(functools.partial(s5_out, c), vregs // 2 * 4 * 4, 4 * vregs))
    tasks.append(Task(lru_out, 0, 14 * vregs))

    def glu():
        g = jnp.concatenate([st["g", c] for c in range(SSM_W // LANES)], axis=1)
        gate = _dot(g.astype(BF16), w.wglu[...]) + w.bglu[...]
        st["n_ssm"] = _rmsnorm(g * jax.nn.sigmoid(gate), w.gssm[...]).astype(BF16)
    tasks.append(Task(glu, vregs // 2 * 4 * 4, 12 * vregs))

    def out_proj(k):
        cols = slice(k * MXU_DIM, (k + 1) * MXU_DIM)
        y_ref[:, cols] = (x_ref[:, cols] + _dot(st["n_lru"], w.wout[0:LRU_W, cols])
                          + _dot(st["n_ssm"], w.wout[LRU_W:, cols]))
    for k in range(D_MODEL // MXU_DIM):
        tasks.append(Task(functools.partial(out_proj, k), vregs // 2 * 4 * 4, vregs))
    return tasks


INTERLEAVE_BAND = 0.12


def _run_interleaved(*queues):
    queues = [list(q) for q in queues]
    cost = lambda t: max(t.mxu, t.vpu)
    totals = [float(sum(cost(t) for t in q)) for q in queues]
    total_mxu = float(sum(t.mxu for q in queues for t in q))
    total_vpu = float(sum(t.vpu for q in queues for t in q))
    done = [0.0] * len(queues)
    run_mxu = run_vpu = 0.0
    while any(queues):
        overall = sum(done) / sum(totals)
        live = [i for i, q in enumerate(queues) if q]
        pool = ([i for i in live if done[i] / totals[i] < overall - INTERLEAVE_BAND]
                or [i for i in live if done[i] / totals[i] <= overall + INTERLEAVE_BAND] or live)
        pick = min(pool, key=lambda i: abs((run_mxu + queues[i][0].mxu) / total_mxu
                                           - (run_vpu + queues[i][0].vpu) / total_vpu))
        task = queues[pick].pop(0)
        done[pick] += cost(task)
        run_mxu += task.mxu
        run_vpu += task.vpu
        task.fn()


N_WEIGHTS = len(MixerWeights._fields)
N_BUFS = len(ChunkBufs._fields)


def _mixer_plain_kernel(nb, seq, x_ref, conv0_ref, lru0_ref, ssm0_ref, *rest):
    w = MixerWeights(*rest[:N_WEIGHTS])
    y_ref, convc_ref, lruc_ref, ssmc_ref, xp_ref = rest[N_WEIGHTS:N_WEIGHTS + 5]
    bufs = ChunkBufs(*rest[N_WEIGHTS + 5:])
    convc_ref[...] = conv0_ref[...]
    lruc_ref[...] = lru0_ref[...]
    ssmc_ref[...] = ssm0_ref[...]
    for task in (_stage_a_tasks(nb, seq, x_ref, w, convc_ref, xp_ref, bufs, None)
                 + _stage_l_tasks(nb, seq, w, lruc_ref, ssmc_ref, bufs, None)
                 + _stage_c_tasks(nb * seq, x_ref, y_ref, w, bufs)):
        task.fn()


def _mixer_pipelined_kernel(n_chunks, seq, x_cur_ref, x_prev_ref, conv0_ref, lru0_ref, ssm0_ref, *rest):
    w = MixerWeights(*rest[:N_WEIGHTS])
    y_ref, convc_ref, lruc_ref, ssmc_ref, xp_ref = rest[N_WEIGHTS:N_WEIGHTS + 5]
    sets = (ChunkBufs(*rest[N_WEIGHTS + 5:N_WEIGHTS + 5 + N_BUFS]),
            ChunkBufs(*rest[N_WEIGHTS + 5 + N_BUFS:]))
    step = pl.program_id(0)

    @pl.when(step == 0)
    def _():
        convc_ref[...] = conv0_ref[...]
        lruc_ref[...] = lru0_ref[...]
        ssmc_ref[...] = ssm0_ref[...]
        for ref in sets[1]:
            ref[...] = jnp.zeros(ref.shape, ref.dtype)

    def body(write, read):
        _run_interleaved(
            _stage_l_tasks(1, seq, w, lruc_ref, ssmc_ref, read, step >= 1)
            + _stage_c_tasks(seq, x_prev_ref, y_ref, w, read),
            _stage_a_tasks(1, seq, x_cur_ref, w, convc_ref, xp_ref, write, step < n_chunks))

    @pl.when(step % 2 == 0)
    def _():
        body(sets[0], sets[1])

    @pl.when(step % 2 == 1)
    def _():
        body(sets[1], sets[0])


def _full_spec(shape):
    zeros = (0,) * len(shape)
    return pl.BlockSpec(shape, lambda i, _z=zeros: _z)


def _layer_spec(shape, layer):
    index = (layer,) + (0,) * (len(shape) - 1)
    return pl.BlockSpec((None,) + tuple(shape[1:]), lambda i, _x=index: _x)


def _chunk_scratch(rows):
    xs_rows = -(-(S5_TIME_PITCH * rows + S5_ROW_PITCH * SUBLANES) // SUBLANES) * SUBLANES
    return [pltpu.VMEM((SSM_SLABS, xs_rows, LANES), F32),
            pltpu.VMEM((rows * LRU_CHUNKS, LANES), F32), pltpu.VMEM((rows * LRU_CHUNKS, LANES), F32),
            pltpu.VMEM((rows, LRU_W), F32), pltpu.VMEM((rows, SSM_W), F32)]


def _mixer_call(x2d, states, state_layer, w, layer, nb, seq, pipelined):
    total = x2d.shape[0]
    rows = nb * seq
    conv0, lru0, ssm0 = states
    state_in_specs = [_layer_spec(s.shape, state_layer) for s in states]
    state_specs = [_full_spec(s.shape[1:]) for s in states]
    weight_specs = [_layer_spec(a.shape, layer) for a in w]
    out_shape = (jax.ShapeDtypeStruct((total, D_MODEL), F32),) + tuple(
        jax.ShapeDtypeStruct(s.shape[1:], F32) for s in states)
    conv_scratch = pltpu.VMEM((nb, LRU_CHUNKS, CONV_PITCH * seq + CONV_BASE, LANES), F32)
    scratch = [conv_scratch] + _chunk_scratch(rows)
    params = pltpu.CompilerParams(dimension_semantics=("arbitrary",), vmem_limit_bytes=VMEM_LIMIT_BYTES)
    if not pipelined:
        assert total == rows
        x_spec = pl.BlockSpec((rows, D_MODEL), lambda i: (0, 0))
        return pl.pallas_call(
            functools.partial(_mixer_plain_kernel, nb, seq),
            grid=(1,), in_specs=[x_spec] + state_in_specs + weight_specs,
            out_specs=(x_spec,) + tuple(state_specs), out_shape=out_shape,
            scratch_shapes=scratch, name="mixer_plain", compiler_params=params,
        )(x2d, conv0, lru0, ssm0, *w)
    assert nb == 1 and total % rows == 0
    n_chunks = total // rows
    cur_spec = pl.BlockSpec((rows, D_MODEL), lambda i: (jnp.minimum(i, n_chunks - 1), 0))
    prev_spec = pl.BlockSpec((rows, D_MODEL), lambda i: (jnp.maximum(i - 1, 0), 0))
    return pl.pallas_call(
        functools.partial(_mixer_pipelined_kernel, n_chunks, seq),
        grid=(n_chunks + 1,), in_specs=[cur_spec, prev_spec] + state_in_specs + weight_specs,
        out_specs=(prev_spec,) + tuple(state_specs), out_shape=out_shape,
        scratch_shapes=scratch + _chunk_scratch(rows), name="mixer", compiler_params=params,
    )(x2d, x2d, conv0, lru0, ssm0, *w)


def _ffn_kernel(final_norm, x_ref, gffn_ref, wg_ref, wu_ref, wd_ref, gfin_ref, o_ref):
    x = x_ref[...]
    h = _rmsnorm(x, gffn_ref[...]).astype(BF16)
    acc = x
    for c in range(D_FF // FF_CHUNK):
        lo, hi = c * FF_CHUNK, (c + 1) * FF_CHUNK
        gate = _dot(h, wg_ref[:, lo:hi])
        up = _dot(h, wu_ref[:, lo:hi])
        act = (gate * jax.nn.sigmoid(gate) * up).astype(BF16)
        acc = acc + _dot(act, wd_ref[lo:hi, :])
    if final_norm:
        acc = _rmsnorm(acc, gfin_ref[...])
    o_ref[...] = acc


def _ffn_call(x2d, w, layer, gfin, final_norm, tile):
    total = x2d.shape[0]
    weights = tuple(w) + (gfin,)
    in_specs = [pl.BlockSpec((tile, D_MODEL), lambda i: (i, 0))]
    in_specs += [_layer_spec(a.shape, layer) for a in w] + [_full_spec(gfin.shape)]
    return pl.pallas_call(
        functools.partial(_ffn_kernel, final_norm),
        grid=(total // tile,), in_specs=in_specs,
        out_specs=pl.BlockSpec((tile, D_MODEL), lambda i: (i, 0)),
        out_shape=jax.ShapeDtypeStruct((total, D_MODEL), F32), name="ffn",
        compiler_params=pltpu.CompilerParams(dimension_semantics=("arbitrary",),
                                             vmem_limit_bytes=VMEM_LIMIT_BYTES),
    )(x2d, *weights)


def _gate_weights(wa, wx):
    depth = wa.shape[0]
    per = LRU_HEADS // GATE_HALVES
    mask = jnp.eye(per, dtype=F32).reshape(1, 1, per, 1, per, 1)

    def blockdiag(w):
        w6 = w.reshape(depth, GATE_HALVES, per, LRU_HD, 1, LRU_HD)
        return (w6 * mask).reshape(depth, GATE_HALVES, MXU_DIM, MXU_DIM)

    return jnp.concatenate([blockdiag(wa), blockdiag(wx)], axis=3).astype(BF16)


def _s5_in_weights(bbr, bbi):
    depth = bbr.shape[0]
    chunks = SSM_W // LANES
    bb = jnp.stack([bbr, bbi], axis=3)
    bb = bb.reshape(depth, chunks, S5_ROWS_PER_CHUNK, 2, SSM_H, 1, 2, 1, SSM_P)
    eye_s = jnp.eye(S5_ROWS_PER_CHUNK, dtype=F32).reshape(1, 1, S5_ROWS_PER_CHUNK, 1, 1, S5_ROWS_PER_CHUNK, 1, 1, 1)
    eye_g = jnp.eye(2, dtype=F32).reshape(1, 1, 1, 2, 1, 1, 1, 2, 1)
    w = bb * eye_s * eye_g
    return w.reshape(depth, chunks, LANES, S5_ROWS_PER_CHUNK * 2 * 2 * SSM_P).astype(BF16)


def _s5_out_weights(c_re, c_im):
    depth = c_re.shape[0]
    nrows = SSM_G // 2
    cc = jnp.stack([jnp.swapaxes(c_re, 2, 3), -jnp.swapaxes(c_im, 2, 3)], axis=1)
    cc = cc.reshape(depth, 2, nrows // S5_ROWS_PER_CHUNK, S5_ROWS_PER_CHUNK, 2, SSM_P, 1, 1, SSM_H)
    cc = jnp.moveaxis(cc, 1, 3)
    eye_s = jnp.eye(S5_ROWS_PER_CHUNK, dtype=F32).reshape(1, 1, S5_ROWS_PER_CHUNK, 1, 1, 1, S5_ROWS_PER_CHUNK, 1, 1)
    eye_g = jnp.eye(2, dtype=F32).reshape(1, 1, 1, 1, 2, 1, 1, 2, 1)
    w = cc * eye_s * eye_g
    return w.reshape(depth, nrows, 2 * 2 * SSM_P, LANES).astype(BF16)


def _ssm_to_slabs(re, im):
    lead = re.shape[:-2]
    return jnp.concatenate([re.reshape(lead + (2, SUBLANES, LANES)), im.reshape(lead + (2, SUBLANES, LANES))],
                           axis=len(lead))


def _slabs_to_ssm(slabs):
    lead = slabs.shape[:-3]
    return (slabs[..., 0:2, :, :].reshape(lead + (SSM_G, SSM_P)), slabs[..., 2:4, :, :].reshape(lead + (SSM_G, SSM_P)))


def _conv_to_tile(conv):
    pad = [(0, 0)] * (conv.ndim - 2) + [(SUBLANES - (CONV_W - 1), 0), (0, 0)]
    return jnp.pad(conv, pad)


def _lru_to_tile(h):
    lead = h.shape[:-1]
    pad = [(0, 0)] * len(lead) + [(0, SUBLANES - LRU_CHUNKS), (0, 0)]
    return jnp.pad(h.reshape(lead + (LRU_CHUNKS, LANES)), pad)


def _tile_to_lru(tile):
    return tile[..., :LRU_CHUNKS, :].reshape(tile.shape[:-2] + (LRU_W,))


def kernel(x_prompt, x_sample, state_conv, state_lru, state_ssm_re, state_ssm_im, norm_mix, w_in, conv_w, conv_b, lru_wa, lru_ba, lru_wx, lru_bx, lru_lambda, ssm_lambda_re, ssm_lambda_im, ssm_b_re, ssm_b_im, ssm_c_re, ssm_c_im, ssm_d, ssm_log_dt, ssm_w_glu, ssm_b_glu, norm_lru_out, norm_ssm_out, w_out, norm_ffn, w_gate, w_up, w_down, norm_final):
    depth = w_in.shape[0]
    bp, tp, _ = x_prompt.shape
    bs, ts, _ = x_sample.shape
    assert bp == 1

    lbr, lbi, bbr, bbi = _s5_prep(ssm_lambda_re, ssm_lambda_im, ssm_log_dt, ssm_b_re, ssm_b_im)

    def rows(v):
        return v.reshape(depth, 1, -1)

    mixer_w = MixerWeights(
        gmix=rows(norm_mix), win=w_in.astype(BF16), convw=conv_w, convb=rows(conv_b),
        wgate=_gate_weights(lru_wa, lru_wx), bgate=jnp.stack([lru_ba, lru_bx], axis=1), lam=rows(lru_lambda),
        lbr=lbr.reshape(depth, 2, SUBLANES, LANES), lbi=lbi.reshape(depth, 2, SUBLANES, LANES),
        bw=_s5_in_weights(bbr, bbi), cw=_s5_out_weights(ssm_c_re, ssm_c_im),
        dskip=rows(ssm_d), wglu=ssm_w_glu.astype(BF16), bglu=rows(ssm_b_glu),
        glru=rows(norm_lru_out), gssm=rows(norm_ssm_out), wout=w_out.astype(BF16))
    ffn_w = (rows(norm_ffn), w_gate.astype(BF16), w_up.astype(BF16), w_down.astype(BF16))
    gfin = norm_final.reshape(1, -1)

    yp = x_prompt.reshape(bp * tp, D_MODEL)
    ys = x_sample.reshape(bs * ts, D_MODEL)
    prompt_states = (jnp.zeros((1, bp, SUBLANES, LRU_W), F32), jnp.zeros((1, bp, SUBLANES, LANES), F32),
                     jnp.zeros((1, bp, SSM_SLABS, SUBLANES, LANES), F32))
    sample_states = (_conv_to_tile(state_conv), _lru_to_tile(state_lru), _ssm_to_slabs(state_ssm_re, state_ssm_im))

    prompt_out, sample_out = [], []
    for l in range(depth):
        last = l == depth - 1
        yp, *st_p = _mixer_call(yp, prompt_states, 0, mixer_w, l, 1, PROMPT_CHUNK, True)
        yp = _ffn_call(yp, ffn_w, l, gfin, last, FFN_TILE)
        ys, *st_s = _mixer_call(ys, sample_states, l, mixer_w, l, bs, ts, False)
        ys = _ffn_call(ys, ffn_w, l, gfin, last, bs * ts)
        prompt_out.append(st_p)
        sample_out.append(st_s)

    def unpack(per_layer):
        conv, lru, ssm = (jnp.stack(v) for v in zip(*per_layer))
        re, im = _slabs_to_ssm(ssm)
        return conv[:, :, SUBLANES - (CONV_W - 1):], _tile_to_lru(lru), re, im

    return (yp.reshape(bp, tp, D_MODEL), ys.reshape(bs, ts, D_MODEL)) + unpack(prompt_out) + unpack(sample_out)
```

```python
import collections
import functools
import math

import jax
import jax.numpy as jnp
from jax import lax
from jax.experimental import pallas as pl
from jax.experimental.pallas import tpu as pltpu

D_MODEL = 1024
LRU_W = 512
LRU_HEADS = 8
LRU_HD = LRU_W // LRU_HEADS
CONV_W = 4
LRU_C = 8.0
SSM_W = 512
SSM_H = 16
SSM_G = SSM_W // SSM_H
SSM_P = 64
IN_W = 2 * LRU_W + SSM_W
D_FF = 2816
EPS = 1e-6

SUBLANES = 8
LANES = 128
MXU_DIM = 256
VMEM_LIMIT_BYTES = 56 * 1024 * 1024

SSM_SLABS = 4
S5_ROWS_PER_CHUNK = LANES // (2 * SSM_H)
S5_TIME_PITCH = 9
S5_ROW_PITCH = 2
LRU_CHUNKS = LRU_W // LANES
CONV_PITCH = 2
CONV_BASE = CONV_PITCH * SUBLANES
GATE_HALVES = LRU_W // MXU_DIM
FF_CHUNK = MXU_DIM
PROMPT_CHUNK = 256
FFN_TILE = 512
BF16 = jnp.bfloat16
F32 = jnp.float32

MixerWeights = collections.namedtuple(
    "MixerWeights",
    "gmix win convw convb wgate bgate lam lbr lbi bw cwt dskip wglu bglu glru gssm wout")
ChunkBufs = collections.namedtuple("ChunkBufs", "xs a b gl us")
Task = collections.namedtuple("Task", "fn mxu vpu")


def _rmsnorm(x, gain):
    var = jnp.mean(x * x, axis=-1, keepdims=True)
    return x * lax.rsqrt(var + EPS) * gain


def _dot(a, b):
    return jnp.dot(a, b, preferred_element_type=F32)


_GELU_K1 = -2.0 * math.sqrt(2.0 / math.pi) * math.log2(math.e)
_GELU_K2 = _GELU_K1 * 0.044715


def _gelu(x):
    return x / (1.0 + jnp.exp2(x * (_GELU_K1 + _GELU_K2 * (x * x))))


def _keep_if(pred, new, old):
    return new if pred is None else jnp.where(pred, new, old)


def _lanes(c):
    return slice(c * LANES, (c + 1) * LANES)


def _prep_kernel(depth, lr_ref, li_ref, ldt_ref, br_ref, bi_ref, cr_ref, ci_ref, wa_ref, wx_ref,
                 lbr_ref, lbi_ref, bw_ref, cwt_ref, wgate_ref):
    lr = lr_ref[...]
    li = li_ref[...]
    dt = jnp.exp(ldt_ref[...])
    mag = jnp.exp(lr * dt)
    lbr = mag * jnp.cos(li * dt)
    lbi = mag * jnp.sin(li * dt)
    nr, ni = lbr - 1.0, lbi
    den = lr * lr + li * li
    gr = (nr * lr + ni * li) / den
    gi = (ni * lr - nr * li) / den
    br = br_ref[...]
    bi = bi_ref[...]
    lbr_ref[...] = lbr
    lbi_ref[...] = lbi
    bb = ((gr * br - gi * bi).astype(BF16), (gr * bi + gi * br).astype(BF16))
    cc = (cr_ref[...].astype(BF16), (-ci_ref[...]).astype(BF16))
    bw_ref[...] = jnp.zeros(bw_ref.shape, BF16)
    cwt_ref[...] = jnp.zeros(cwt_ref.shape, BF16)
    wgate_ref[...] = jnp.zeros(wgate_ref.shape, BF16)
    chunks = SSM_W // LANES
    for l in range(depth):
        for g in range(SSM_G):
            s, gg = divmod(g, 2)
            c, sl = divmod(s, S5_ROWS_PER_CHUNK)
            src = slice((l * SSM_G + g) * SSM_H, (l * SSM_G + g + 1) * SSM_H)
            lanes_u = slice((sl * 2 + gg) * SSM_H, (sl * 2 + gg + 1) * SSM_H)
            for reim in range(2):
                col = (sl * 2 + reim) * LANES + gg * SSM_P
                bw_ref[l * chunks + c, lanes_u, col:col + SSM_P] = bb[reim][src, :]
                col = reim * LANES + gg * SSM_P
                cwt_ref[l * (SSM_G // 2) + s, lanes_u, col:col + SSM_P] = cc[reim][src, :]
        per = LRU_HEADS // GATE_HALVES
        for head in range(LRU_HEADS):
            hf, hh = divmod(head, per)
            src = slice((l * LRU_HEADS + head) * LRU_HD, (l * LRU_HEADS + head + 1) * LRU_HD)
            blk = slice(hh * LRU_HD, (hh + 1) * LRU_HD)
            wgate_ref[l * GATE_HALVES + hf, blk, blk] = wa_ref[src, :].astype(BF16)
            wgate_ref[l * GATE_HALVES + hf, blk, MXU_DIM + hh * LRU_HD:MXU_DIM + (hh + 1) * LRU_HD] = (
                wx_ref[src, :].astype(BF16))


def _prep(lam_re, lam_im, log_dt, b_re, b_im, c_re, c_im, wa, wx):
    depth = lam_re.shape[0]
    rows = depth * SSM_G * SSM_H
    shape4 = (depth, SSM_G, SSM_H, SSM_P)

    def expand(a):
        return jnp.broadcast_to(a[:, :, None, :], shape4).reshape(rows, SSM_P)

    ldt = jnp.broadcast_to(log_dt[:, :, None, None], shape4).reshape(rows, SSM_P)
    br = jnp.swapaxes(b_re, 2, 3).reshape(rows, SSM_P)
    bi = jnp.swapaxes(b_im, 2, 3).reshape(rows, SSM_P)
    chunks = SSM_W // LANES
    sds = jax.ShapeDtypeStruct((rows, SSM_P), F32)
    out_shape = (sds, sds,
                 jax.ShapeDtypeStruct((depth * chunks, LANES, S5_ROWS_PER_CHUNK * 2 * LANES), BF16),
                 jax.ShapeDtypeStruct((depth * SSM_G // 2, LANES, 2 * LANES), BF16),
                 jax.ShapeDtypeStruct((depth * GATE_HALVES, MXU_DIM, 2 * MXU_DIM), BF16))
    lbr, lbi, bw, cwt, wgate = pl.pallas_call(
        functools.partial(_prep_kernel, depth), out_shape=out_shape, name="prep",
    )(expand(lam_re), expand(lam_im), ldt, br, bi, c_re.reshape(rows, SSM_P), c_im.reshape(rows, SSM_P),
      wa.reshape(depth * LRU_W, LRU_HD), wx.reshape(depth * LRU_W, LRU_HD))
    lbr = lbr.reshape(shape4)[:, :, 0, :].reshape(depth, 2, SUBLANES, LANES)
    lbi = lbi.reshape(shape4)[:, :, 0, :].reshape(depth, 2, SUBLANES, LANES)
    return (lbr, lbi, bw.reshape((depth, chunks) + bw.shape[1:]), cwt.reshape((depth, SSM_G // 2) + cwt.shape[1:]),
            wgate.reshape((depth, GATE_HALVES) + wgate.shape[1:]))


def _s5_time_rows(srow, rows):
    return pl.ds(S5_ROW_PITCH * srow, rows, stride=S5_TIME_PITCH)


def _s5_step_rows(r):
    return pl.ds(S5_TIME_PITCH * r, SUBLANES, stride=S5_ROW_PITCH)


def _stage_a_tasks(nb, seq, x_ref, w, convc_ref, xp_ref, bufs, live):
    rows = nb * seq
    vregs = rows // SUBLANES
    st = {}
    tasks = []

    def norm():
        st["h"] = _rmsnorm(x_ref[...], w.gmix[...]).astype(BF16)
    tasks.append(Task(norm, 0, 14 * vregs))

    def in_proj(k):
        st["z", k] = _dot(st["h"], w.win[:, k * MXU_DIM:(k + 1) * MXU_DIM])
        if k in (2, 3):
            bufs.gl[:, (k - 2) * MXU_DIM:(k - 1) * MXU_DIM] = st["z", k]
        if k in (4, 5):
            bufs.us[:, (k - 4) * MXU_DIM:(k - 3) * MXU_DIM] = st["z", k]
    for k in range(IN_W // MXU_DIM):
        tasks.append(Task(functools.partial(in_proj, k), D_MODEL // MXU_DIM * vregs // 2 * 4, 0))

    def conv(c):
        zk, off = divmod(c * LANES, MXU_DIM)
        xl = st["z", zk][:, off:off + LANES]
        cw = w.convw[:, _lanes(c)]
        parts = []
        for j in range(nb):
            xl_j = xl[j * seq:(j + 1) * seq, :]
            xp_ref[j, c, pl.ds(0, SUBLANES, stride=CONV_PITCH), :] = convc_ref[j, :, _lanes(c)]
            xp_ref[j, c, pl.ds(CONV_BASE, seq, stride=CONV_PITCH), :] = xl_j
            acc = w.convb[:, _lanes(c)] + xl_j * cw[CONV_W - 1:CONV_W, :]
            for k in range(CONV_W - 1):
                start = CONV_BASE - CONV_PITCH * (CONV_W - 1 - k)
                acc = acc + xp_ref[j, c, pl.ds(start, seq, stride=CONV_PITCH), :] * cw[k:k + 1, :]
            parts.append(acc)
            convc_ref[j, :, _lanes(c)] = _keep_if(live, xl_j[seq - SUBLANES:, :], convc_ref[j, :, _lanes(c)])
        st["xc", c] = parts[0] if nb == 1 else jnp.concatenate(parts, axis=0)
    for c in range(LRU_CHUNKS):
        tasks.append(Task(functools.partial(conv, c), 0, 2 * vregs))

    def gates(hf):
        lo, hi = hf * MXU_DIM, (hf + 1) * MXU_DIM
        chunks = range(hf * MXU_DIM // LANES, (hf + 1) * MXU_DIM // LANES)
        xc = jnp.concatenate([st["xc", c] for c in chunks], axis=1)
        lam = w.lam[:, lo:hi]
        neg_c_softplus = -LRU_C * (jnp.maximum(-lam, 0.0) + jnp.log1p(jnp.exp(-jnp.abs(lam))))
        pre = _dot(xc.astype(BF16), w.wgate[hf])
        r = jax.nn.sigmoid(pre[:, :MXU_DIM] + w.bgate[0:1, lo:hi])
        ig = jax.nn.sigmoid(pre[:, MXU_DIM:] + w.bgate[1:2, lo:hi])
        log_a = neg_c_softplus * r
        a = jnp.exp(log_a)
        one_minus_a2 = -jnp.tanh(log_a) * (a * a + 1.0)
        b = jnp.sqrt(one_minus_a2) * (ig * xc)
        for i, c in enumerate(chunks):
            bufs.a[pl.ds(c, rows, stride=LRU_CHUNKS), :] = a[:, _lanes(i)]
            bufs.b[pl.ds(c, rows, stride=LRU_CHUNKS), :] = b[:, _lanes(i)]
    for hf in range(GATE_HALVES):
        tasks.append(Task(functools.partial(gates, hf), vregs // 2 * 2 * 4, 11 * vregs))

    def s5_in(c):
        zk, off = divmod(c * LANES, MXU_DIM)
        us_bf = st["z", 4 + zk][:, off:off + LANES].astype(BF16)
        bu = _dot(us_bf, w.bw[c])
        half = (c * S5_ROWS_PER_CHUNK) // SUBLANES
        for sl in range(S5_ROWS_PER_CHUNK):
            srow = (c * S5_ROWS_PER_CHUNK + sl) % SUBLANES
            for reim in range(2):
                col = (sl * 2 + reim) * LANES
                bufs.xs[reim * 2 + half, _s5_time_rows(srow, rows), :] = bu[:, col:col + LANES]
    for c in range(SSM_W // LANES):
        tasks.append(Task(functools.partial(s5_in, c), vregs // 2 * 4 * 4, vregs // 2))
    return tasks


def _stage_l_tasks(nb, seq, w, lruc_ref, ssmc_ref, bufs, live):
    st = {}
    tasks = []
    row = lambda: lax.broadcasted_iota(jnp.int32, (SUBLANES, LANES), 0)

    def begin(j):
        st["lb"] = ((w.lbr[0], w.lbr[1]), (w.lbi[0], w.lbi[1]))
        st["xr"] = [ssmc_ref[j, 0], ssmc_ref[j, 1]]
        st["xi"] = [ssmc_ref[j, 2], ssmc_ref[j, 3]]
        st["h"] = lruc_ref[j]
        st["old"] = (tuple(st["xr"]), tuple(st["xi"]), st["h"])

    def steps(j, tile):
        lbr, lbi = st["lb"]
        xr, xi = st["xr"], st["xi"]
        for t in range(tile * SUBLANES, (tile + 1) * SUBLANES):
            rows_t = _s5_step_rows(j * seq + t)
            for hv in range(2):
                nr = lbr[hv] * xr[hv] - lbi[hv] * xi[hv] + bufs.xs[hv, rows_t, :]
                ni = lbr[hv] * xi[hv] + lbi[hv] * xr[hv] + bufs.xs[2 + hv, rows_t, :]
                bufs.xs[hv, rows_t, :] = nr
                bufs.xs[2 + hv, rows_t, :] = ni
                xr[hv], xi[hv] = nr, ni
        first_half = row() < LRU_CHUNKS
        h = st["h"]
        for pair in range(tile * SUBLANES // 2, (tile + 1) * SUBLANES // 2):
            r0 = (j * seq + 2 * pair) * LRU_CHUNKS
            a2 = bufs.a[r0:r0 + SUBLANES, :]
            b2 = bufs.b[r0:r0 + SUBLANES, :]
            h_even = a2 * h + b2
            h_odd = a2 * pltpu.roll(h_even, LRU_CHUNKS, 0) + b2
            bufs.b[r0:r0 + SUBLANES, :] = jnp.where(first_half, h_even, h_odd)
            h = pltpu.roll(h_odd, LRU_CHUNKS, 0)
        st["h"] = h

    def end(j):
        old = st["old"]
        for hv in range(2):
            ssmc_ref[j, hv] = _keep_if(live, st["xr"][hv], old[0][hv])
            ssmc_ref[j, 2 + hv] = _keep_if(live, st["xi"][hv], old[1][hv])
        lruc_ref[j] = _keep_if(live, st["h"], old[2])

    for j in range(nb):
        tasks.append(Task(functools.partial(begin, j), 0, 1))
        for tile in range(seq // SUBLANES):
            tasks.append(Task(functools.partial(steps, j, tile), 0, 56))
        tasks.append(Task(functools.partial(end, j), 0, 1))
    return tasks


def _stage_c_tasks(rows, x_ref, y_ref, w, bufs):
    vregs = rows // SUBLANES
    st = {}
    tasks = []

    def lru_out():
        h = jnp.concatenate([bufs.b[pl.ds(c, rows, stride=LRU_CHUNKS), :] for c in range(LRU_CHUNKS)], axis=1)
        st["n_lru"] = _rmsnorm(h * _gelu(bufs.gl[...]), w.glru[...]).astype(BF16)

    def s5_out(c):
        acc = None
        for sl in range(S5_ROWS_PER_CHUNK):
            s = c * S5_ROWS_PER_CHUNK + sl
            half, srow = s // SUBLANES, s % SUBLANES
            xr = bufs.xs[half, _s5_time_rows(srow, rows), :]
            xi = bufs.xs[2 + half, _s5_time_rows(srow, rows), :]
            xcat = jnp.concatenate([xr, xi], axis=1).astype(BF16)
            part = lax.dot_general(xcat, w.cwt[s], (((1,), (1,)), ((), ())),
                                   preferred_element_type=F32)
            acc = part if acc is None else acc + part
        st["g", c] = _gelu(acc + w.dskip[:, _lanes(c)] * bufs.us[:, _lanes(c)])
    for c in range(SSM_W // LANES):
        tasks.append(Task(functools.partial(s5_out, c), vregs // 2 * 4 * 4, 4 * vregs))
    tasks.append(Task(lru_out, 0, 14 * vregs))

    def glu():
        g = jnp.concatenate([st["g", c] for c in range(SSM_W // LANES)], axis=1)
        gate = _dot(g.astype(BF16), w.wglu[...]) + w.bglu[...]
        st["n_ssm"] = _rmsnorm(g * jax.nn.sigmoid(gate), w.gssm[...]).astype(BF16)
    tasks.append(Task(glu, vregs // 2 * 4 * 4, 12 * vregs))

    def out_proj(k):
        cols = slice(k * MXU_DIM, (k + 1) * MXU_DIM)
        y_ref[:, cols] = (x_ref[:, cols] + _dot(st["n_lru"], w.wout[0:LRU_W, cols])
                          + _dot(st["n_ssm"], w.wout[LRU_W:, cols]))
    for k in range(D_MODEL // MXU_DIM):
        tasks.append(Task(functools.partial(out_proj, k), vregs // 2 * 4 * 4, vregs))
    return tasks


INTERLEAVE_BAND = 0.12


def _run_interleaved(*queues):
    queues = [list(q) for q in queues]
    cost = lambda t: max(t.mxu, t.vpu)
    totals = [float(sum(cost(t) for t in q)) for q in queues]
    total_mxu = float(sum(t.mxu for q in queues for t in q))
    total_vpu = float(sum(t.vpu for q in queues for t in q))
    done = [0.0] * len(queues)
    run_mxu = run_vpu = 0.0
    while any(queues):
        overall = sum(done) / sum(totals)
        live = [i for i, q in enumerate(queues) if q]
        pool = ([i for i in live if done[i] / totals[i] < overall - INTERLEAVE_BAND]
                or [i for i in live if done[i] / totals[i] <= overall + INTERLEAVE_BAND] or live)
        pick = min(pool, key=lambda i: abs((run_mxu + queues[i][0].mxu) / total_mxu
                                           - (run_vpu + queues[i][0].vpu) / total_vpu))
        task = queues[pick].pop(0)
        done[pick] += cost(task)
        run_mxu += task.mxu
        run_vpu += task.vpu
        task.fn()


N_WEIGHTS = len(MixerWeights._fields)
N_BUFS = len(ChunkBufs._fields)


def _mixer_plain_kernel(nb, seq, x_ref, conv0_ref, lru0_ref, ssm0_ref, *rest):
    w = MixerWeights(*rest[:N_WEIGHTS])
    y_ref, convc_ref, lruc_ref, ssmc_ref, xp_ref = rest[N_WEIGHTS:N_WEIGHTS + 5]
    bufs = ChunkBufs(*rest[N_WEIGHTS + 5:])
    convc_ref[...] = conv0_ref[...]
    lruc_ref[...] = lru0_ref[...]
    ssmc_ref[...] = ssm0_ref[...]
    for task in (_stage_a_tasks(nb, seq, x_ref, w, convc_ref, xp_ref, bufs, None)
                 + _stage_l_tasks(nb, seq, w, lruc_ref, ssmc_ref, bufs, None)
                 + _stage_c_tasks(nb * seq, x_ref, y_ref, w, bufs)):
        task.fn()


def _mixer_pipelined_kernel(n_chunks, seq, x_cur_ref, x_prev_ref, conv0_ref, lru0_ref, ssm0_ref, *rest):
    w = MixerWeights(*rest[:N_WEIGHTS])
    y_ref, convc_ref, lruc_ref, ssmc_ref, xp_ref = rest[N_WEIGHTS:N_WEIGHTS + 5]
    sets = (ChunkBufs(*rest[N_WEIGHTS + 5:N_WEIGHTS + 5 + N_BUFS]),
            ChunkBufs(*rest[N_WEIGHTS + 5 + N_BUFS:]))
    step = pl.program_id(0)

    @pl.when(step == 0)
    def _():
        convc_ref[...] = conv0_ref[...]
        lruc_ref[...] = lru0_ref[...]
        ssmc_ref[...] = ssm0_ref[...]
        for ref in sets[1]:
            ref[...] = jnp.zeros(ref.shape, ref.dtype)

    def body(write, read):
        _run_interleaved(
            _stage_l_tasks(1, seq, w, lruc_ref, ssmc_ref, read, step >= 1)
            + _stage_c_tasks(seq, x_prev_ref, y_ref, w, read),
            _stage_a_tasks(1, seq, x_cur_ref, w, convc_ref, xp_ref, write, step < n_chunks))

    @pl.when(step % 2 == 0)
    def _():
        body(sets[0], sets[1])

    @pl.when(step % 2 == 1)
    def _():
        body(sets[1], sets[0])


def _full_spec(shape):
    zeros = (0,) * len(shape)
    return pl.BlockSpec(shape, lambda i, _z=zeros: _z)


def _layer_spec(shape, layer):
    index = (layer,) + (0,) * (len(shape) - 1)
    return pl.BlockSpec((None,) + tuple(shape[1:]), lambda i, _x=index: _x)


def _chunk_scratch(rows):
    xs_rows = -(-(S5_TIME_PITCH * rows + S5_ROW_PITCH * SUBLANES) // SUBLANES) * SUBLANES
    return [pltpu.VMEM((SSM_SLABS, xs_rows, LANES), F32),
            pltpu.VMEM((rows * LRU_CHUNKS, LANES), F32), pltpu.VMEM((rows * LRU_CHUNKS, LANES), F32),
            pltpu.VMEM((rows, LRU_W), F32), pltpu.VMEM((rows, SSM_W), F32)]


def _mixer_call(x2d, states, state_layer, w, layer, nb, seq, pipelined):
    total = x2d.shape[0]
    rows = nb * seq
    conv0, lru0, ssm0 = states
    state_in_specs = [_layer_spec(s.shape, state_layer) for s in states]
    state_specs = [_full_spec(s.shape[1:]) for s in states]
    weight_specs = [_layer_spec(a.shape, layer) for a in w]
    out_shape = (jax.ShapeDtypeStruct((total, D_MODEL), F32),) + tuple(
        jax.ShapeDtypeStruct(s.shape[1:], F32) for s in states)
    conv_scratch = pltpu.VMEM((nb, LRU_CHUNKS, CONV_PITCH * seq + CONV_BASE, LANES), F32)
    scratch = [conv_scratch] + _chunk_scratch(rows)
    params = pltpu.CompilerParams(dimension_semantics=("arbitrary",), vmem_limit_bytes=VMEM_LIMIT_BYTES)
    if not pipelined:
        assert total == rows
        x_spec = pl.BlockSpec((rows, D_MODEL), lambda i: (0, 0))
        return pl.pallas_call(
            functools.partial(_mixer_plain_kernel, nb, seq),
            grid=(1,), in_specs=[x_spec] + state_in_specs + weight_specs,
            out_specs=(x_spec,) + tuple(state_specs), out_shape=out_shape,
            scratch_shapes=scratch, name="mixer_plain", compiler_params=params,
        )(x2d, conv0, lru0, ssm0, *w)
    assert nb == 1 and total % rows == 0
    n_chunks = total // rows
    cur_spec = pl.BlockSpec((rows, D_MODEL), lambda i: (jnp.minimum(i, n_chunks - 1), 0))
    prev_spec = pl.BlockSpec((rows, D_MODEL), lambda i: (jnp.maximum(i - 1, 0), 0))
    return pl.pallas_call(
        functools.partial(_mixer_pipelined_kernel, n_chunks, seq),
        grid=(n_chunks + 1,), in_specs=[cur_spec, prev_spec] + state_in_specs + weight_specs,
        out_specs=(prev_spec,) + tuple(state_specs), out_shape=out_shape,
        scratch_shapes=scratch + _chunk_scratch(rows), name="mixer", compiler_params=params,
    )(x2d, x2d, conv0, lru0, ssm0, *w)


def _ffn_kernel(final_norm, x_ref, gffn_ref, wg_ref, wu_ref, wd_ref, gfin_ref, o_ref):
    x = x_ref[...]
    h = _rmsnorm(x, gffn_ref[...]).astype(BF16)
    acc = x
    for c in range(D_FF // FF_CHUNK):
        lo, hi = c * FF_CHUNK, (c + 1) * FF_CHUNK
        gate = _dot(h, wg_ref[:, lo:hi])
        up = _dot(h, wu_ref[:, lo:hi])
        act = (gate * jax.nn.sigmoid(gate) * up).astype(BF16)
        acc = acc + _dot(act, wd_ref[lo:hi, :])
    if final_norm:
        acc = _rmsnorm(acc, gfin_ref[...])
    o_ref[...] = acc


def _ffn_call(x2d, w, layer, gfin, final_norm, tile):
    total = x2d.shape[0]
    weights = tuple(w) + (gfin,)
    in_specs = [pl.BlockSpec((tile, D_MODEL), lambda i: (i, 0))]
    in_specs += [_layer_spec(a.shape, layer) for a in w] + [_full_spec(gfin.shape)]
    return pl.pallas_call(
        functools.partial(_ffn_kernel, final_norm),
        grid=(total // tile,), in_specs=in_specs,
        out_specs=pl.BlockSpec((tile, D_MODEL), lambda i: (i, 0)),
        out_shape=jax.ShapeDtypeStruct((total, D_MODEL), F32), name="ffn",
        compiler_params=pltpu.CompilerParams(dimension_semantics=("arbitrary",),
                                             vmem_limit_bytes=VMEM_LIMIT_BYTES),
    )(x2d, *weights)


def _ssm_to_slabs(re, im):
    lead = re.shape[:-2]
    return jnp.concatenate([re.reshape(lead + (2, SUBLANES, LANES)), im.reshape(lead + (2, SUBLANES, LANES))],
                           axis=len(lead))


def _slabs_to_ssm(slabs):
    lead = slabs.shape[:-3]
    return (slabs[..., 0:2, :, :].reshape(lead + (SSM_G, SSM_P)), slabs[..., 2:4, :, :].reshape(lead + (SSM_G, SSM_P)))


def _conv_to_tile(conv):
    pad = [(0, 0)] * (conv.ndim - 2) + [(SUBLANES - (CONV_W - 1), 0), (0, 0)]
    return jnp.pad(conv, pad)


def _lru_to_tile(h):
    lead = h.shape[:-1]
    pad = [(0, 0)] * len(lead) + [(0, SUBLANES - LRU_CHUNKS), (0, 0)]
    return jnp.pad(h.reshape(lead + (LRU_CHUNKS, LANES)), pad)


def _tile_to_lru(tile):
    return tile[..., :LRU_CHUNKS, :].reshape(tile.shape[:-2] + (LRU_W,))


def kernel(x_prompt, x_sample, state_conv, state_lru, state_ssm_re, state_ssm_im, norm_mix, w_in, conv_w, conv_b, lru_wa, lru_ba, lru_wx, lru_bx, lru_lambda, ssm_lambda_re, ssm_lambda_im, ssm_b_re, ssm_b_im, ssm_c_re, ssm_c_im, ssm_d, ssm_log_dt, ssm_w_glu, ssm_b_glu, norm_lru_out, norm_ssm_out, w_out, norm_ffn, w_gate, w_up, w_down, norm_final):
    depth = w_in.shape[0]
    bp, tp, _ = x_prompt.shape
    bs, ts, _ = x_sample.shape
    assert bp == 1

    lbr, lbi, bw, cwt, wgate = _prep(ssm_lambda_re, ssm_lambda_im, ssm_log_dt, ssm_b_re, ssm_b_im,
                                     ssm_c_re, ssm_c_im, lru_wa, lru_wx)

    def rows(v):
        return v.reshape(depth, 1, -1)

    mixer_w = MixerWeights(
        gmix=rows(norm_mix), win=w_in.astype(BF16), convw=conv_w, convb=rows(conv_b),
        wgate=wgate, bgate=jnp.stack([lru_ba, lru_bx], axis=1), lam=rows(lru_lambda),
        lbr=lbr, lbi=lbi, bw=bw, cwt=cwt,
        dskip=rows(ssm_d), wglu=ssm_w_glu.astype(BF16), bglu=rows(ssm_b_glu),
        glru=rows(norm_lru_out), gssm=rows(norm_ssm_out), wout=w_out.astype(BF16))
    ffn_w = (rows(norm_ffn), w_gate.astype(BF16), w_up.astype(BF16), w_down.astype(BF16))
    gfin = norm_final.reshape(1, -1)

    yp = x_prompt.reshape(bp * tp, D_MODEL)
    ys = x_sample.reshape(bs * ts, D_MODEL)
    prompt_states = (jnp.zeros((1, bp, SUBLANES, LRU_W), F32), jnp.zeros((1, bp, SUBLANES, LANES), F32),
                     jnp.zeros((1, bp, SSM_SLABS, SUBLANES, LANES), F32))
    sample_states = (_conv_to_tile(state_conv), _lru_to_tile(state_lru), _ssm_to_slabs(state_ssm_re, state_ssm_im))

    prompt_out, sample_out = [], []
    for l in range(depth):
        last = l == depth - 1
        yp, *st_p = _mixer_call(yp, prompt_states, 0, mixer_w, l, 1, PROMPT_CHUNK, True)
        yp = _ffn_call(yp, ffn_w, l, gfin, last, FFN_TILE)
        ys, *st_s = _mixer_call(ys, sample_states, l, mixer_w, l, bs, ts, False)
        ys = _ffn_call(ys, ffn_w, l, gfin, last, bs * ts)
        prompt_out.append(st_p)
        sample_out.append(st_s)

    def unpack(per_layer):
        conv, lru, ssm = (jnp.stack(v) for v in zip(*per_layer))
        re, im = _slabs_to_ssm(ssm)
        return conv[:, :, SUBLANES - (CONV_W - 1):], _tile_to_lru(lru), re, im

    return (yp.reshape(bp, tp, D_MODEL), ys.reshape(bs, ts, D_MODEL)) + unpack(prompt_out) + unpack(sample_out)
```

```python
import collections
import functools
import math

import jax
import jax.numpy as jnp
from jax import lax
from jax.experimental import pallas as pl
from jax.experimental.pallas import tpu as pltpu

D_MODEL = 1024
LRU_W = 512
LRU_HEADS = 8
LRU_HD = LRU_W // LRU_HEADS
CONV_W = 4
LRU_C = 8.0
SSM_W = 512
SSM_H = 16
SSM_G = SSM_W // SSM_H
SSM_P = 64
IN_W = 2 * LRU_W + SSM_W
D_FF = 2816
EPS = 1e-6

SUBLANES = 8
LANES = 128
MXU_DIM = 256
VMEM_LIMIT_BYTES = 56 * 1024 * 1024

SSM_SLABS = 4
S5_ROWS_PER_CHUNK = LANES // (2 * SSM_H)
S5_TIME_PITCH = 9
S5_ROW_PITCH = 2
LRU_CHUNKS = LRU_W // LANES
CONV_PITCH = 2
CONV_BASE = CONV_PITCH * SUBLANES
GATE_HALVES = LRU_W // MXU_DIM
FF_CHUNK = MXU_DIM
PROMPT_CHUNK = 512
FFN_TILE = 512
BF16 = jnp.bfloat16
F32 = jnp.float32

MixerWeights = collections.namedtuple(
    "MixerWeights",
    "gmix win convw convb wgate bgate lam lbr lbi bw cwt dskip wglu bglu glru gssm wout")
ChunkBufs = collections.namedtuple("ChunkBufs", "xs a b gl us")
Task = collections.namedtuple("Task", "fn mxu vpu")


def _rmsnorm(x, gain):
    var = jnp.mean(x * x, axis=-1, keepdims=True)
    return x * lax.rsqrt(var + EPS) * gain


def _dot(a, b):
    return jnp.dot(a, b, preferred_element_type=F32)


_GELU_K1 = -2.0 * math.sqrt(2.0 / math.pi) * math.log2(math.e)
_GELU_K2 = _GELU_K1 * 0.044715


def _gelu(x):
    return x / (1.0 + jnp.exp2(x * (_GELU_K1 + _GELU_K2 * (x * x))))


def _keep_if(pred, new, old):
    return new if pred is None else jnp.where(pred, new, old)


def _lanes(c):
    return slice(c * LANES, (c + 1) * LANES)


def _prep_kernel(depth, lr_ref, li_ref, ldt_ref, br_ref, bi_ref, cr_ref, ci_ref, wa_ref, wx_ref,
                 lbr_ref, lbi_ref, bw_ref, cwt_ref, wgate_ref):
    lr = lr_ref[...]
    li = li_ref[...]
    dt = jnp.exp(ldt_ref[...])
    mag = jnp.exp(lr * dt)
    lbr = mag * jnp.cos(li * dt)
    lbi = mag * jnp.sin(li * dt)
    nr, ni = lbr - 1.0, lbi
    den = lr * lr + li * li
    gr = (nr * lr + ni * li) / den
    gi = (ni * lr - nr * li) / den
    br = br_ref[...]
    bi = bi_ref[...]
    lbr_ref[...] = lbr
    lbi_ref[...] = lbi
    bb = ((gr * br - gi * bi).astype(BF16), (gr * bi + gi * br).astype(BF16))
    cc = (cr_ref[...].astype(BF16), (-ci_ref[...]).astype(BF16))
    bw_ref[...] = jnp.zeros(bw_ref.shape, BF16)
    cwt_ref[...] = jnp.zeros(cwt_ref.shape, BF16)
    wgate_ref[...] = jnp.zeros(wgate_ref.shape, BF16)
    chunks = SSM_W // LANES
    for l in range(depth):
        for g in range(SSM_G):
            s, gg = divmod(g, 2)
            c, sl = divmod(s, S5_ROWS_PER_CHUNK)
            src = slice((l * SSM_G + g) * SSM_H, (l * SSM_G + g + 1) * SSM_H)
            lanes_u = slice((sl * 2 + gg) * SSM_H, (sl * 2 + gg + 1) * SSM_H)
            for reim in range(2):
                col = (sl * 2 + reim) * LANES + gg * SSM_P
                bw_ref[l * chunks + c, lanes_u, col:col + SSM_P] = bb[reim][src, :]
                cwt_ref[l * chunks + c, lanes_u, col:col + SSM_P] = cc[reim][src, :]
        per = LRU_HEADS // GATE_HALVES
        for head in range(LRU_HEADS):
            hf, hh = divmod(head, per)
            src = slice((l * LRU_HEADS + head) * LRU_HD, (l * LRU_HEADS + head + 1) * LRU_HD)
            blk = slice(hh * LRU_HD, (hh + 1) * LRU_HD)
            wgate_ref[l * GATE_HALVES + hf, blk, blk] = wa_ref[src, :].astype(BF16)
            wgate_ref[l * GATE_HALVES + hf, blk, MXU_DIM + hh * LRU_HD:MXU_DIM + (hh + 1) * LRU_HD] = (
                wx_ref[src, :].astype(BF16))


def _prep(lam_re, lam_im, log_dt, b_re, b_im, c_re, c_im, wa, wx):
    depth = lam_re.shape[0]
    rows = depth * SSM_G * SSM_H
    shape4 = (depth, SSM_G, SSM_H, SSM_P)

    def expand(a):
        return jnp.broadcast_to(a[:, :, None, :], shape4).reshape(rows, SSM_P)

    ldt = jnp.broadcast_to(log_dt[:, :, None, None], shape4).reshape(rows, SSM_P)
    br = jnp.swapaxes(b_re, 2, 3).reshape(rows, SSM_P)
    bi = jnp.swapaxes(b_im, 2, 3).reshape(rows, SSM_P)
    chunks = SSM_W // LANES
    sds = jax.ShapeDtypeStruct((rows, SSM_P), F32)
    out_shape = (sds, sds,
                 jax.ShapeDtypeStruct((depth * chunks, LANES, S5_ROWS_PER_CHUNK * 2 * LANES), BF16),
                 jax.ShapeDtypeStruct((depth * chunks, LANES, S5_ROWS_PER_CHUNK * 2 * LANES), BF16),
                 jax.ShapeDtypeStruct((depth * GATE_HALVES, MXU_DIM, 2 * MXU_DIM), BF16))
    lbr, lbi, bw, cwt, wgate = pl.pallas_call(
        functools.partial(_prep_kernel, depth), out_shape=out_shape, name="prep",
    )(expand(lam_re), expand(lam_im), ldt, br, bi, c_re.reshape(rows, SSM_P), c_im.reshape(rows, SSM_P),
      wa.reshape(depth * LRU_W, LRU_HD), wx.reshape(depth * LRU_W, LRU_HD))
    lbr = lbr.reshape(shape4)[:, :, 0, :].reshape(depth, 2, SUBLANES, LANES)
    lbi = lbi.reshape(shape4)[:, :, 0, :].reshape(depth, 2, SUBLANES, LANES)
    return (lbr, lbi, bw.reshape((depth, chunks) + bw.shape[1:]), cwt.reshape((depth, chunks) + cwt.shape[1:]),
            wgate.reshape((depth, GATE_HALVES) + wgate.shape[1:]))


def _s5_time_rows(srow, rows):
    return pl.ds(S5_ROW_PITCH * srow, rows, stride=S5_TIME_PITCH)


def _s5_step_rows(r):
    return pl.ds(S5_TIME_PITCH * r, SUBLANES, stride=S5_ROW_PITCH)


def _stage_a_tasks(nb, seq, x_ref, w, convc_ref, xp_ref, bufs, live):
    rows = nb * seq
    vregs = rows // SUBLANES
    st = {}
    tasks = []

    def norm():
        st["h"] = _rmsnorm(x_ref[...], w.gmix[...]).astype(BF16)
    tasks.append(Task(norm, 0, 14 * vregs))

    def in_proj(k):
        st["z", k] = _dot(st["h"], w.win[:, k * MXU_DIM:(k + 1) * MXU_DIM])
        if k in (2, 3):
            bufs.gl[:, (k - 2) * MXU_DIM:(k - 1) * MXU_DIM] = st["z", k]
        if k in (4, 5):
            bufs.us[:, (k - 4) * MXU_DIM:(k - 3) * MXU_DIM] = st["z", k]
    for k in range(IN_W // MXU_DIM):
        tasks.append(Task(functools.partial(in_proj, k), D_MODEL // MXU_DIM * vregs // 2 * 4, 0))

    def conv(c):
        zk, off = divmod(c * LANES, MXU_DIM)
        xl = st["z", zk][:, off:off + LANES]
        cw = w.convw[:, _lanes(c)]
        parts = []
        for j in range(nb):
            xl_j = xl[j * seq:(j + 1) * seq, :]
            xp_ref[j, c, pl.ds(0, SUBLANES, stride=CONV_PITCH), :] = convc_ref[j, :, _lanes(c)]
            xp_ref[j, c, pl.ds(CONV_BASE, seq, stride=CONV_PITCH), :] = xl_j
            acc = w.convb[:, _lanes(c)] + xl_j * cw[CONV_W - 1:CONV_W, :]
            for k in range(CONV_W - 1):
                start = CONV_BASE - CONV_PITCH * (CONV_W - 1 - k)
                acc = acc + xp_ref[j, c, pl.ds(start, seq, stride=CONV_PITCH), :] * cw[k:k + 1, :]
            parts.append(acc)
            convc_ref[j, :, _lanes(c)] = _keep_if(live, xl_j[seq - SUBLANES:, :], convc_ref[j, :, _lanes(c)])
        st["xc", c] = parts[0] if nb == 1 else jnp.concatenate(parts, axis=0)
    for c in range(LRU_CHUNKS):
        tasks.append(Task(functools.partial(conv, c), 0, 2 * vregs))

    def gates(hf):
        lo, hi = hf * MXU_DIM, (hf + 1) * MXU_DIM
        chunks = range(hf * MXU_DIM // LANES, (hf + 1) * MXU_DIM // LANES)
        xc = jnp.concatenate([st["xc", c] for c in chunks], axis=1)
        lam = w.lam[:, lo:hi]
        neg_c_softplus = -LRU_C * (jnp.maximum(-lam, 0.0) + jnp.log1p(jnp.exp(-jnp.abs(lam))))
        pre = _dot(xc.astype(BF16), w.wgate[hf])
        r = jax.nn.sigmoid(pre[:, :MXU_DIM] + w.bgate[0:1, lo:hi])
        ig = jax.nn.sigmoid(pre[:, MXU_DIM:] + w.bgate[1:2, lo:hi])
        log_a = neg_c_softplus * r
        a = jnp.exp(log_a)
        one_minus_a2 = -jnp.tanh(log_a) * (a * a + 1.0)
        b = jnp.sqrt(one_minus_a2) * (ig * xc)
        for i, c in enumerate(chunks):
            bufs.a[pl.ds(c, rows, stride=LRU_CHUNKS), :] = a[:, _lanes(i)]
            bufs.b[pl.ds(c, rows, stride=LRU_CHUNKS), :] = b[:, _lanes(i)]
    for hf in range(GATE_HALVES):
        tasks.append(Task(functools.partial(gates, hf), vregs // 2 * 2 * 4, 11 * vregs))

    def s5_in(c):
        zk, off = divmod(c * LANES, MXU_DIM)
        us_bf = st["z", 4 + zk][:, off:off + LANES].astype(BF16)
        bu = _dot(us_bf, w.bw[c])
        half = (c * S5_ROWS_PER_CHUNK) // SUBLANES
        for sl in range(S5_ROWS_PER_CHUNK):
            srow = (c * S5_ROWS_PER_CHUNK + sl) % SUBLANES
            for reim in range(2):
                col = (sl * 2 + reim) * LANES
                bufs.xs[reim * 2 + half, _s5_time_rows(srow, rows), :] = bu[:, col:col + LANES]
    for c in range(SSM_W // LANES):
        tasks.append(Task(functools.partial(s5_in, c), vregs // 2 * 4 * 4, vregs // 2))
    return tasks


def _stage_l_tasks(nb, seq, w, lruc_ref, ssmc_ref, bufs, live):
    st = {}
    tasks = []
    row = lambda: lax.broadcasted_iota(jnp.int32, (SUBLANES, LANES), 0)

    def begin(j):
        st["lb"] = ((w.lbr[0], w.lbr[1]), (w.lbi[0], w.lbi[1]))
        st["xr"] = [ssmc_ref[j, 0], ssmc_ref[j, 1]]
        st["xi"] = [ssmc_ref[j, 2], ssmc_ref[j, 3]]
        st["h"] = lruc_ref[j]
        st["old"] = (tuple(st["xr"]), tuple(st["xi"]), st["h"])

    def steps(j, tile):
        lbr, lbi = st["lb"]
        xr, xi = st["xr"], st["xi"]
        for t in range(tile * SUBLANES, (tile + 1) * SUBLANES):
            rows_t = _s5_step_rows(j * seq + t)
            for hv in range(2):
                nr = lbr[hv] * xr[hv] - lbi[hv] * xi[hv] + bufs.xs[hv, rows_t, :]
                ni = lbr[hv] * xi[hv] + lbi[hv] * xr[hv] + bufs.xs[2 + hv, rows_t, :]
                bufs.xs[hv, rows_t, :] = nr
                bufs.xs[2 + hv, rows_t, :] = ni
                xr[hv], xi[hv] = nr, ni
        first_half = row() < LRU_CHUNKS
        h = st["h"]
        for pair in range(tile * SUBLANES // 2, (tile + 1) * SUBLANES // 2):
            r0 = (j * seq + 2 * pair) * LRU_CHUNKS
            a2 = bufs.a[r0:r0 + SUBLANES, :]
            b2 = bufs.b[r0:r0 + SUBLANES, :]
            h_even = a2 * h + b2
            h_odd = a2 * pltpu.roll(h_even, LRU_CHUNKS, 0) + b2
            bufs.b[r0:r0 + SUBLANES, :] = jnp.where(first_half, h_even, h_odd)
            h = pltpu.roll(h_odd, LRU_CHUNKS, 0)
        st["h"] = h

    def end(j):
        old = st["old"]
        for hv in range(2):
            ssmc_ref[j, hv] = _keep_if(live, st["xr"][hv], old[0][hv])
            ssmc_ref[j, 2 + hv] = _keep_if(live, st["xi"][hv], old[1][hv])
        lruc_ref[j] = _keep_if(live, st["h"], old[2])

    for j in range(nb):
        tasks.append(Task(functools.partial(begin, j), 0, 1))
        for tile in range(seq // SUBLANES):
            tasks.append(Task(functools.partial(steps, j, tile), 0, 56))
        tasks.append(Task(functools.partial(end, j), 0, 1))
    return tasks


def _stage_c_tasks(rows, x_ref, y_ref, w, bufs):
    vregs = rows // SUBLANES
    st = {}
    tasks = []

    def lru_out():
        h = jnp.concatenate([bufs.b[pl.ds(c, rows, stride=LRU_CHUNKS), :] for c in range(LRU_CHUNKS)], axis=1)
        st["n_lru"] = _rmsnorm(h * _gelu(bufs.gl[...]), w.glru[...]).astype(BF16)

    def s5_out(c):
        pieces = []
        for sl in range(S5_ROWS_PER_CHUNK):
            s = c * S5_ROWS_PER_CHUNK + sl
            half, srow = s // SUBLANES, s % SUBLANES
            pieces.append(bufs.xs[half, _s5_time_rows(srow, rows), :])
            pieces.append(bufs.xs[2 + half, _s5_time_rows(srow, rows), :])
        xcat = jnp.concatenate(pieces, axis=1).astype(BF16)
        y = lax.dot_general(xcat, w.cwt[c], (((1,), (1,)), ((), ())), preferred_element_type=F32)
        st["g", c] = _gelu(y + w.dskip[:, _lanes(c)] * bufs.us[:, _lanes(c)])
    for c in range(SSM_W // LANES):
        tasks.append(Task(functools.partial(s5_out, c), vregs // 2 * 4 * 4, 4 * vregs))
    tasks.append(Task(lru_out, 0, 14 * vregs))

    def glu():
        g = jnp.concatenate([st["g", c] for c in range(SSM_W // LANES)], axis=1)
        gate = _dot(g.astype(BF16), w.wglu[...]) + w.bglu[...]
        st["n_ssm"] = _rmsnorm(g * jax.nn.sigmoid(gate), w.gssm[...]).astype(BF16)
    tasks.append(Task(glu, vregs // 2 * 4 * 4, 12 * vregs))

    def out_proj(k):
        cols = slice(k * MXU_DIM, (k + 1) * MXU_DIM)
        y_ref[:, cols] = (x_ref[:, cols] + _dot(st["n_lru"], w.wout[0:LRU_W, cols])
                          + _dot(st["n_ssm"], w.wout[LRU_W:, cols]))
    for k in range(D_MODEL // MXU_DIM):
        tasks.append(Task(functools.partial(out_proj, k), vregs // 2 * 4 * 4, vregs))
    return tasks


INTERLEAVE_BAND = 0.12


def _run_interleaved(*queues):
    queues = [list(q) for q in queues]
    cost = lambda t: max(t.mxu, t.vpu)
    totals = [float(sum(cost(t) for t in q)) for q in queues]
    total_mxu = float(sum(t.mxu for q in queues for t in q))
    total_vpu = float(sum(t.vpu for q in queues for t in q))
    done = [0.0] * len(queues)
    run_mxu = run_vpu = 0.0
    while any(queues):
        overall = sum(done) / sum(totals)
        live = [i for i, q in enumerate(queues) if q]
        pool = ([i for i in live if done[i] / totals[i] < overall - INTERLEAVE_BAND]
                or [i for i in live if done[i] / totals[i] <= overall + INTERLEAVE_BAND] or live)
        pick = min(pool, key=lambda i: abs((run_mxu + queues[i][0].mxu) / total_mxu
                                           - (run_vpu + queues[i][0].vpu) / total_vpu))
        task = queues[pick].pop(0)
        done[pick] += cost(task)
        run_mxu += task.mxu
        run_vpu += task.vpu
        task.fn()


N_WEIGHTS = len(MixerWeights._fields)
N_BUFS = len(ChunkBufs._fields)


def _mixer_plain_kernel(nb, seq, x_ref, conv0_ref, lru0_ref, ssm0_ref, *rest):
    w = MixerWeights(*rest[:N_WEIGHTS])
    y_ref, convc_ref, lruc_ref, ssmc_ref, xp_ref = rest[N_WEIGHTS:N_WEIGHTS + 5]
    bufs = ChunkBufs(*rest[N_WEIGHTS + 5:])
    convc_ref[...] = conv0_ref[...]
    lruc_ref[...] = lru0_ref[...]
    ssmc_ref[...] = ssm0_ref[...]
    for task in (_stage_a_tasks(nb, seq, x_ref, w, convc_ref, xp_ref, bufs, None)
                 + _stage_l_tasks(nb, seq, w, lruc_ref, ssmc_ref, bufs, None)
                 + _stage_c_tasks(nb * seq, x_ref, y_ref, w, bufs)):
        task.fn()


def _mixer_pipelined_kernel(n_chunks, seq, x_cur_ref, x_prev_ref, conv0_ref, lru0_ref, ssm0_ref, *rest):
    w = MixerWeights(*rest[:N_WEIGHTS])
    y_ref, convc_ref, lruc_ref, ssmc_ref, xp_ref = rest[N_WEIGHTS:N_WEIGHTS + 5]
    sets = (ChunkBufs(*rest[N_WEIGHTS + 5:N_WEIGHTS + 5 + N_BUFS]),
            ChunkBufs(*rest[N_WEIGHTS + 5 + N_BUFS:]))
    step = pl.program_id(0)

    @pl.when(step == 0)
    def _():
        convc_ref[...] = conv0_ref[...]
        lruc_ref[...] = lru0_ref[...]
        ssmc_ref[...] = ssm0_ref[...]
        for ref in sets[1]:
            ref[...] = jnp.zeros(ref.shape, ref.dtype)

    def body(write, read):
        _run_interleaved(
            _stage_l_tasks(1, seq, w, lruc_ref, ssmc_ref, read, step >= 1)
            + _stage_c_tasks(seq, x_prev_ref, y_ref, w, read),
            _stage_a_tasks(1, seq, x_cur_ref, w, convc_ref, xp_ref, write, step < n_chunks))

    @pl.when(step % 2 == 0)
    def _():
        body(sets[0], sets[1])

    @pl.when(step % 2 == 1)
    def _():
        body(sets[1], sets[0])


def _full_spec(shape):
    zeros = (0,) * len(shape)
    return pl.BlockSpec(shape, lambda i, _z=zeros: _z)


def _layer_spec(shape, layer):
    index = (layer,) + (0,) * (len(shape) - 1)
    return pl.BlockSpec((None,) + tuple(shape[1:]), lambda i, _x=index: _x, pipeline_mode=pl.Buffered(1))


def _chunk_scratch(rows):
    xs_rows = -(-(S5_TIME_PITCH * rows + S5_ROW_PITCH * SUBLANES) // SUBLANES) * SUBLANES
    return [pltpu.VMEM((SSM_SLABS, xs_rows, LANES), F32),
            pltpu.VMEM((rows * LRU_CHUNKS, LANES), F32), pltpu.VMEM((rows * LRU_CHUNKS, LANES), F32),
            pltpu.VMEM((rows, LRU_W), F32), pltpu.VMEM((rows, SSM_W), F32)]


def _mixer_call(x2d, states, state_layer, w, layer, nb, seq, pipelined):
    total = x2d.shape[0]
    rows = nb * seq
    conv0, lru0, ssm0 = states
    state_in_specs = [_layer_spec(s.shape, state_layer) for s in states]
    state_specs = [_full_spec(s.shape[1:]) for s in states]
    weight_specs = [_layer_spec(a.shape, layer) for a in w]
    out_shape = (jax.ShapeDtypeStruct((total, D_MODEL), F32),) + tuple(
        jax.ShapeDtypeStruct(s.shape[1:], F32) for s in states)
    conv_scratch = pltpu.VMEM((nb, LRU_CHUNKS, CONV_PITCH * seq + CONV_BASE, LANES), F32)
    scratch = [conv_scratch] + _chunk_scratch(rows)
    params = pltpu.CompilerParams(dimension_semantics=("arbitrary",), vmem_limit_bytes=VMEM_LIMIT_BYTES)
    if not pipelined:
        assert total == rows
        x_spec = pl.BlockSpec((rows, D_MODEL), lambda i: (0, 0))
        return pl.pallas_call(
            functools.partial(_mixer_plain_kernel, nb, seq),
            grid=(1,), in_specs=[x_spec] + state_in_specs + weight_specs,
            out_specs=(x_spec,) + tuple(state_specs), out_shape=out_shape,
            scratch_shapes=scratch, name="mixer_plain", compiler_params=params,
        )(x2d, conv0, lru0, ssm0, *w)
    assert nb == 1 and total % rows == 0
    n_chunks = total // rows
    cur_spec = pl.BlockSpec((rows, D_MODEL), lambda i: (jnp.minimum(i, n_chunks - 1), 0))
    prev_spec = pl.BlockSpec((rows, D_MODEL), lambda i: (jnp.maximum(i - 1, 0), 0))
    return pl.pallas_call(
        functools.partial(_mixer_pipelined_kernel, n_chunks, seq),
        grid=(n_chunks + 1,), in_specs=[cur_spec, prev_spec] + state_in_specs + weight_specs,
        out_specs=(prev_spec,) + tuple(state_specs), out_shape=out_shape,
        scratch_shapes=scratch + _chunk_scratch(rows), name="mixer", compiler_params=params,
    )(x2d, x2d, conv0, lru0, ssm0, *w)


def _ffn_kernel(final_norm, x_ref, gffn_ref, wg_ref, wu_ref, wd_ref, gfin_ref, o_ref):
    x = x_ref[...]
    h = _rmsnorm(x, gffn_ref[...]).astype(BF16)
    acc = x
    for c in range(D_FF // FF_CHUNK):
        lo, hi = c * FF_CHUNK, (c + 1) * FF_CHUNK
        gate = _dot(h, wg_ref[:, lo:hi])
        up = _dot(h, wu_ref[:, lo:hi])
        act = (gate * jax.nn.sigmoid(gate) * up).astype(BF16)
        acc = acc + _dot(act, wd_ref[lo:hi, :])
    if final_norm:
        acc = _rmsnorm(acc, gfin_ref[...])
    o_ref[...] = acc


def _ffn_call(x2d, w, layer, gfin, final_norm, tile):
    total = x2d.shape[0]
    weights = tuple(w) + (gfin,)
    in_specs = [pl.BlockSpec((tile, D_MODEL), lambda i: (i, 0))]
    in_specs += [_layer_spec(a.shape, layer) for a in w] + [_full_spec(gfin.shape)]
    return pl.pallas_call(
        functools.partial(_ffn_kernel, final_norm),
        grid=(total // tile,), in_specs=in_specs,
        out_specs=pl.BlockSpec((tile, D_MODEL), lambda i: (i, 0)),
        out_shape=jax.ShapeDtypeStruct((total, D_MODEL), F32), name="ffn",
        compiler_params=pltpu.CompilerParams(dimension_semantics=("arbitrary",),
                                             vmem_limit_bytes=VMEM_LIMIT_BYTES),
    )(x2d, *weights)


def _ssm_to_slabs(re, im):
    lead = re.shape[:-2]
    return jnp.concatenate([re.reshape(lead + (2, SUBLANES, LANES)), im.reshape(lead + (2, SUBLANES, LANES))],
                           axis=len(lead))


def _slabs_to_ssm(slabs):
    lead = slabs.shape[:-3]
    return (slabs[..., 0:2, :, :].reshape(lead + (SSM_G, SSM_P)), slabs[..., 2:4, :, :].reshape(lead + (SSM_G, SSM_P)))


def _conv_to_tile(conv):
    pad = [(0, 0)] * (conv.ndim - 2) + [(SUBLANES - (CONV_W - 1), 0), (0, 0)]
    return jnp.pad(conv, pad)


def _lru_to_tile(h):
    lead = h.shape[:-1]
    pad = [(0, 0)] * len(lead) + [(0, SUBLANES - LRU_CHUNKS), (0, 0)]
    return jnp.pad(h.reshape(lead + (LRU_CHUNKS, LANES)), pad)


def _tile_to_lru(tile):
    return tile[..., :LRU_CHUNKS, :].reshape(tile.shape[:-2] + (LRU_W,))


def kernel(x_prompt, x_sample, state_conv, state_lru, state_ssm_re, state_ssm_im, norm_mix, w_in, conv_w, conv_b, lru_wa, lru_ba, lru_wx, lru_bx, lru_lambda, ssm_lambda_re, ssm_lambda_im, ssm_b_re, ssm_b_im, ssm_c_re, ssm_c_im, ssm_d, ssm_log_dt, ssm_w_glu, ssm_b_glu, norm_lru_out, norm_ssm_out, w_out, norm_ffn, w_gate, w_up, w_down, norm_final):
    depth = w_in.shape[0]
    bp, tp, _ = x_prompt.shape
    bs, ts, _ = x_sample.shape
    assert bp == 1

    lbr, lbi, bw, cwt, wgate = _prep(ssm_lambda_re, ssm_lambda_im, ssm_log_dt, ssm_b_re, ssm_b_im,
                                     ssm_c_re, ssm_c_im, lru_wa, lru_wx)

    def rows(v):
        return v.reshape(depth, 1, -1)

    mixer_w = MixerWeights(
        gmix=rows(norm_mix), win=w_in.astype(BF16), convw=conv_w, convb=rows(conv_b),
        wgate=wgate, bgate=jnp.stack([lru_ba, lru_bx], axis=1), lam=rows(lru_lambda),
        lbr=lbr, lbi=lbi, bw=bw, cwt=cwt,
        dskip=rows(ssm_d), wglu=ssm_w_glu.astype(BF16), bglu=rows(ssm_b_glu),
        glru=rows(norm_lru_out), gssm=rows(norm_ssm_out), wout=w_out.astype(BF16))
    ffn_w = (rows(norm_ffn), w_gate.astype(BF16), w_up.astype(BF16), w_down.astype(BF16))
    gfin = norm_final.reshape(1, -1)

    yp = x_prompt.reshape(bp * tp, D_MODEL)
    ys = x_sample.reshape(bs * ts, D_MODEL)
    prompt_states = (jnp.zeros((1, bp, SUBLANES, LRU_W), F32), jnp.zeros((1, bp, SUBLANES, LANES), F32),
                     jnp.zeros((1, bp, SSM_SLABS, SUBLANES, LANES), F32))
    sample_states = (_conv_to_tile(state_conv), _lru_to_tile(state_lru), _ssm_to_slabs(state_ssm_re, state_ssm_im))

    prompt_out, sample_out = [], []
    for l in range(depth):
        last = l == depth - 1
        yp, *st_p = _mixer_call(yp, prompt_states, 0, mixer_w, l, 1, PROMPT_CHUNK, True)
        yp = _ffn_call(yp, ffn_w, l, gfin, last, FFN_TILE)
        ys, *st_s = _mixer_call(ys, sample_states, l, mixer_w, l, bs, ts, False)
        ys = _ffn_call(ys, ffn_w, l, gfin, last, bs * ts)
        prompt_out.append(st_p)
        sample_out.append(st_s)

    def unpack(per_layer):
        conv, lru, ssm = (jnp.stack(v) for v in zip(*per_layer))
        re, im = _slabs_to_ssm(ssm)
        return conv[:, :, SUBLANES - (CONV_W - 1):], _tile_to_lru(lru), re, im

    return (yp.reshape(bp, tp, D_MODEL), ys.reshape(bs, ts, D_MODEL)) + unpack(prompt_out) + unpack(sample_out)
```

```python
import collections
import functools
import math

import jax
import jax.numpy as jnp
from jax import lax
from jax.experimental import pallas as pl
from jax.experimental.pallas import tpu as pltpu

D_MODEL = 1024
LRU_W = 512
LRU_HEADS = 8
LRU_HD = LRU_W // LRU_HEADS
CONV_W = 4
LRU_C = 8.0
SSM_W = 512
SSM_H = 16
SSM_G = SSM_W // SSM_H
SSM_P = 64
IN_W = 2 * LRU_W + SSM_W
D_FF = 2816
EPS = 1e-6

SUBLANES = 8
LANES = 128
MXU_DIM = 256
VMEM_LIMIT_BYTES = 56 * 1024 * 1024

SSM_SLABS = 4
S5_ROWS_PER_CHUNK = LANES // (2 * SSM_H)
S5_TIME_PITCH = 9
S5_ROW_PITCH = 2
LRU_CHUNKS = LRU_W // LANES
CONV_PITCH = 2
CONV_BASE = CONV_PITCH * SUBLANES
GATE_HALVES = LRU_W // MXU_DIM
FF_CHUNK = MXU_DIM
PROMPT_CHUNK = 512
VPU_TASK_ROWS = 512
MXU_TASK_ROWS = 512
FFN_TILE = 512
BF16 = jnp.bfloat16
F32 = jnp.float32

MixerWeights = collections.namedtuple(
    "MixerWeights",
    "gmix win convw convb wgate bgate lam lbr lbi bw cwt dskip wglu bglu glru gssm wout")
ChunkBufs = collections.namedtuple("ChunkBufs", "xs a b gl us h")


def _rmsnorm(x, gain):
    var = jnp.mean(x * x, axis=-1, keepdims=True)
    return x * lax.rsqrt(var + EPS) * gain


def _dot(a, b):
    return jnp.dot(a, b, preferred_element_type=F32)


_GELU_K1 = -2.0 * math.sqrt(2.0 / math.pi) * math.log2(math.e)
_GELU_K2 = _GELU_K1 * 0.044715


def _gelu(x):
    return x / (1.0 + jnp.exp2(x * (_GELU_K1 + _GELU_K2 * (x * x))))


def _keep_if(pred, new, old):
    return new if pred is None else jnp.where(pred, new, old)


def _lanes(c):
    return slice(c * LANES, (c + 1) * LANES)


def _prep_kernel(depth, lr_ref, li_ref, ldt_ref, br_ref, bi_ref, cr_ref, ci_ref, wa_ref, wx_ref,
                 lbr_ref, lbi_ref, bw_ref, cwt_ref, wgate_ref):
    lr = lr_ref[...]
    li = li_ref[...]
    dt = jnp.exp(ldt_ref[...])
    mag = jnp.exp(lr * dt)
    lbr = mag * jnp.cos(li * dt)
    lbi = mag * jnp.sin(li * dt)
    nr, ni = lbr - 1.0, lbi
    den = lr * lr + li * li
    gr = (nr * lr + ni * li) / den
    gi = (ni * lr - nr * li) / den
    br = br_ref[...]
    bi = bi_ref[...]
    lbr_ref[...] = lbr
    lbi_ref[...] = lbi
    bb = ((gr * br - gi * bi).astype(BF16), (gr * bi + gi * br).astype(BF16))
    cc = (cr_ref[...].astype(BF16), (-ci_ref[...]).astype(BF16))
    bw_ref[...] = jnp.zeros(bw_ref.shape, BF16)
    cwt_ref[...] = jnp.zeros(cwt_ref.shape, BF16)
    wgate_ref[...] = jnp.zeros(wgate_ref.shape, BF16)
    chunks = SSM_W // LANES
    for l in range(depth):
        for g in range(SSM_G):
            s, gg = divmod(g, 2)
            c, sl = divmod(s, S5_ROWS_PER_CHUNK)
            src = slice((l * SSM_G + g) * SSM_H, (l * SSM_G + g + 1) * SSM_H)
            lanes_u = slice((sl * 2 + gg) * SSM_H, (sl * 2 + gg + 1) * SSM_H)
            for reim in range(2):
                col = (sl * 2 + reim) * LANES + gg * SSM_P
                bw_ref[l * chunks + c, lanes_u, col:col + SSM_P] = bb[reim][src, :]
                cwt_ref[l * chunks + c, lanes_u, col:col + SSM_P] = cc[reim][src, :]
        per = LRU_HEADS // GATE_HALVES
        for head in range(LRU_HEADS):
            hf, hh = divmod(head, per)
            src = slice((l * LRU_HEADS + head) * LRU_HD, (l * LRU_HEADS + head + 1) * LRU_HD)
            blk = slice(hh * LRU_HD, (hh + 1) * LRU_HD)
            wgate_ref[l * GATE_HALVES + hf, blk, blk] = wa_ref[src, :].astype(BF16)
            wgate_ref[l * GATE_HALVES + hf, blk, MXU_DIM + hh * LRU_HD:MXU_DIM + (hh + 1) * LRU_HD] = (
                wx_ref[src, :].astype(BF16))


def _prep(lam_re, lam_im, log_dt, b_re, b_im, c_re, c_im, wa, wx):
    depth = lam_re.shape[0]
    rows = depth * SSM_G * SSM_H
    shape4 = (depth, SSM_G, SSM_H, SSM_P)

    def expand(a):
        return jnp.broadcast_to(a[:, :, None, :], shape4).reshape(rows, SSM_P)

    ldt = jnp.broadcast_to(log_dt[:, :, None, None], shape4).reshape(rows, SSM_P)
    br = jnp.swapaxes(b_re, 2, 3).reshape(rows, SSM_P)
    bi = jnp.swapaxes(b_im, 2, 3).reshape(rows, SSM_P)
    chunks = SSM_W // LANES
    sds = jax.ShapeDtypeStruct((rows, SSM_P), F32)
    out_shape = (sds, sds,
                 jax.ShapeDtypeStruct((depth * chunks, LANES, S5_ROWS_PER_CHUNK * 2 * LANES), BF16),
                 jax.ShapeDtypeStruct((depth * chunks, LANES, S5_ROWS_PER_CHUNK * 2 * LANES), BF16),
                 jax.ShapeDtypeStruct((depth * GATE_HALVES, MXU_DIM, 2 * MXU_DIM), BF16))
    lbr, lbi, bw, cwt, wgate = pl.pallas_call(
        functools.partial(_prep_kernel, depth), out_shape=out_shape, name="prep",
    )(expand(lam_re), expand(lam_im), ldt, br, bi, c_re.reshape(rows, SSM_P), c_im.reshape(rows, SSM_P),
      wa.reshape(depth * LRU_W, LRU_HD), wx.reshape(depth * LRU_W, LRU_HD))
    lbr = lbr.reshape(shape4)[:, :, 0, :].reshape(depth, 2, SUBLANES, LANES)
    lbi = lbi.reshape(shape4)[:, :, 0, :].reshape(depth, 2, SUBLANES, LANES)
    return (lbr, lbi, bw.reshape((depth, chunks) + bw.shape[1:]), cwt.reshape((depth, chunks) + cwt.shape[1:]),
            wgate.reshape((depth, GATE_HALVES) + wgate.shape[1:]))


def _s5_time_rows(srow, row0, count):
    return pl.ds(S5_ROW_PITCH * srow + S5_TIME_PITCH * row0, count, stride=S5_TIME_PITCH)


def _s5_step_rows(r):
    return pl.ds(S5_TIME_PITCH * r, SUBLANES, stride=S5_ROW_PITCH)


def _lru_time_rows(c, row0, count):
    return pl.ds(c + LRU_CHUNKS * row0, count, stride=LRU_CHUNKS)


def _blocks(rows, size):
    size = min(size, rows)
    return [(start, size) for start in range(0, rows, size)]


def _zip_tasks(a, b):
    keyed = [((i + 0.5) / len(a), 0, i, t) for i, t in enumerate(a)]
    keyed += [((i + 0.5) / len(b), 1, i, t) for i, t in enumerate(b)]
    return [t for _, _, _, t in sorted(keyed, key=lambda e: e[:3])]


def _norm_tasks(rows, x_ref, w, h_ref):
    def norm(start, size):
        h_ref[start:start + size, :] = _rmsnorm(x_ref[start:start + size, :], w.gmix[...]).astype(BF16)
    return [functools.partial(norm, start, size) for start, size in _blocks(rows, VPU_TASK_ROWS)]


def _stage_a_tasks(nb, seq, w, convc_ref, xp_ref, bufs, live):
    rows = nb * seq
    vpu_blocks = _blocks(rows, VPU_TASK_ROWS)
    mxu_blocks = _blocks(rows, MXU_TASK_ROWS)
    st = {}
    groups = {"in_proj": [], "conv_gates": [], "s5_in": []}

    def in_proj(k, start, size):
        z = _dot(bufs.h[start:start + size, :], w.win[:, k * MXU_DIM:(k + 1) * MXU_DIM])
        st["z", k, start] = z
        if k in (2, 3):
            bufs.gl[start:start + size, (k - 2) * MXU_DIM:(k - 1) * MXU_DIM] = z
        if k in (4, 5):
            bufs.us[start:start + size, (k - 4) * MXU_DIM:(k - 3) * MXU_DIM] = z
    for k in range(IN_W // MXU_DIM):
        for start, size in mxu_blocks:
            groups["in_proj"].append(functools.partial(in_proj, k, start, size))

    def z_lanes(k0, c):
        zk, off = divmod(c * LANES, MXU_DIM)
        parts = [st["z", k0 + zk, s][:, off:off + LANES] for s, n in mxu_blocks]
        return parts[0] if len(parts) == 1 else jnp.concatenate(parts, axis=0)

    def conv(c):
        xl = z_lanes(0, c)
        cw = w.convw[:, _lanes(c)]
        parts = []
        for j in range(nb):
            xl_j = xl[j * seq:(j + 1) * seq, :]
            xp_ref[j, c, pl.ds(0, SUBLANES, stride=CONV_PITCH), :] = convc_ref[j, :, _lanes(c)]
            xp_ref[j, c, pl.ds(CONV_BASE, seq, stride=CONV_PITCH), :] = xl_j
            acc = w.convb[:, _lanes(c)] + xl_j * cw[CONV_W - 1:CONV_W, :]
            for k in range(CONV_W - 1):
                start = CONV_BASE - CONV_PITCH * (CONV_W - 1 - k)
                acc = acc + xp_ref[j, c, pl.ds(start, seq, stride=CONV_PITCH), :] * cw[k:k + 1, :]
            parts.append(acc)
            convc_ref[j, :, _lanes(c)] = _keep_if(live, xl_j[seq - SUBLANES:, :], convc_ref[j, :, _lanes(c)])
        st["xc", c] = parts[0] if nb == 1 else jnp.concatenate(parts, axis=0)
    for c in range(LRU_CHUNKS):
        groups["conv_gates"].append(functools.partial(conv, c))

    def gates_dot(hf):
        chunks = range(hf * MXU_DIM // LANES, (hf + 1) * MXU_DIM // LANES)
        xc = jnp.concatenate([st["xc", c] for c in chunks], axis=1)
        st["xc2", hf] = xc
        st["pre", hf] = _dot(xc.astype(BF16), w.wgate[hf])

    def gates_ew(hf, start, size):
        lo, hi = hf * MXU_DIM, (hf + 1) * MXU_DIM
        lam = w.lam[:, lo:hi]
        neg_c_softplus = -LRU_C * (jnp.maximum(-lam, 0.0) + jnp.log1p(jnp.exp(-jnp.abs(lam))))
        pre = st["pre", hf][start:start + size, :]
        xc = st["xc2", hf][start:start + size, :]
        r = jax.nn.sigmoid(pre[:, :MXU_DIM] + w.bgate[0:1, lo:hi])
        ig = jax.nn.sigmoid(pre[:, MXU_DIM:] + w.bgate[1:2, lo:hi])
        log_a = neg_c_softplus * r
        a = jnp.exp(log_a)
        one_minus_a2 = -jnp.tanh(log_a) * (a * a + 1.0)
        b = jnp.sqrt(one_minus_a2) * (ig * xc)
        for i in range(MXU_DIM // LANES):
            c = hf * MXU_DIM // LANES + i
            bufs.a[_lru_time_rows(c, start, size), :] = a[:, _lanes(i)]
            bufs.b[_lru_time_rows(c, start, size), :] = b[:, _lanes(i)]
    for hf in range(GATE_HALVES):
        groups["conv_gates"].append(functools.partial(gates_dot, hf))
        for start, size in vpu_blocks:
            groups["conv_gates"].append(functools.partial(gates_ew, hf, start, size))

    def s5_in(c, start, size):
        zk, off = divmod(c * LANES, MXU_DIM)
        us_bf = st["z", 4 + zk, start][:, off:off + LANES].astype(BF16)
        bu = _dot(us_bf, w.bw[c])
        half = (c * S5_ROWS_PER_CHUNK) // SUBLANES
        for sl in range(S5_ROWS_PER_CHUNK):
            srow = (c * S5_ROWS_PER_CHUNK + sl) % SUBLANES
            for reim in range(2):
                col = (sl * 2 + reim) * LANES
                bufs.xs[reim * 2 + half, _s5_time_rows(srow, start, size), :] = bu[:, col:col + LANES]
    for c in range(SSM_W // LANES):
        for start, size in mxu_blocks:
            groups["s5_in"].append(functools.partial(s5_in, c, start, size))
    return groups


def _stage_l_tasks(nb, seq, w, lruc_ref, ssmc_ref, bufs, live):
    st = {}
    tasks = []
    row = lambda: lax.broadcasted_iota(jnp.int32, (SUBLANES, LANES), 0)

    def begin(j):
        st["lb"] = ((w.lbr[0], w.lbr[1]), (w.lbi[0], w.lbi[1]))
        st["xr"] = [ssmc_ref[j, 0], ssmc_ref[j, 1]]
        st["xi"] = [ssmc_ref[j, 2], ssmc_ref[j, 3]]
        st["h"] = lruc_ref[j]
        st["old"] = (tuple(st["xr"]), tuple(st["xi"]), st["h"])

    def steps(j, tile):
        lbr, lbi = st["lb"]
        xr, xi = st["xr"], st["xi"]
        for t in range(tile * SUBLANES, (tile + 1) * SUBLANES):
            rows_t = _s5_step_rows(j * seq + t)
            for hv in range(2):
                nr = lbr[hv] * xr[hv] - lbi[hv] * xi[hv] + bufs.xs[hv, rows_t, :]
                ni = lbr[hv] * xi[hv] + lbi[hv] * xr[hv] + bufs.xs[2 + hv, rows_t, :]
                bufs.xs[hv, rows_t, :] = nr
                bufs.xs[2 + hv, rows_t, :] = ni
                xr[hv], xi[hv] = nr, ni
        first_half = row() < LRU_CHUNKS
        h = st["h"]
        for pair in range(tile * SUBLANES // 2, (tile + 1) * SUBLANES // 2):
            r0 = (j * seq + 2 * pair) * LRU_CHUNKS
            a2 = bufs.a[r0:r0 + SUBLANES, :]
            b2 = bufs.b[r0:r0 + SUBLANES, :]
            h_even = a2 * h + b2
            h_odd = a2 * pltpu.roll(h_even, LRU_CHUNKS, 0) + b2
            bufs.b[r0:r0 + SUBLANES, :] = jnp.where(first_half, h_even, h_odd)
            h = pltpu.roll(h_odd, LRU_CHUNKS, 0)
        st["h"] = h

    def end(j):
        old = st["old"]
        for hv in range(2):
            ssmc_ref[j, hv] = _keep_if(live, st["xr"][hv], old[0][hv])
            ssmc_ref[j, 2 + hv] = _keep_if(live, st["xi"][hv], old[1][hv])
        lruc_ref[j] = _keep_if(live, st["h"], old[2])

    for j in range(nb):
        tasks.append(functools.partial(begin, j))
        for tile in range(seq // SUBLANES):
            tasks.append(functools.partial(steps, j, tile))
        tasks.append(functools.partial(end, j))
    return tasks


def _stage_c_tasks(rows, x_ref, y_ref, w, bufs):
    vpu_blocks = _blocks(rows, VPU_TASK_ROWS)
    mxu_blocks = _blocks(rows, MXU_TASK_ROWS)
    chunks = SSM_W // LANES
    st = {}
    groups = {"s5_out": [], "glu_dot": {}, "lru_out": {}, "glu_ew": {}, "out_proj": {}}

    def rows_of(key, start, size):
        parts = [st[key, s] for s, n in vpu_blocks if start <= s < start + size]
        return parts[0] if len(parts) == 1 else jnp.concatenate(parts, axis=0)

    def s5_out(c, start, size):
        pieces = []
        for sl in range(S5_ROWS_PER_CHUNK):
            s = c * S5_ROWS_PER_CHUNK + sl
            half, srow = s // SUBLANES, s % SUBLANES
            pieces.append(bufs.xs[half, _s5_time_rows(srow, start, size), :])
            pieces.append(bufs.xs[2 + half, _s5_time_rows(srow, start, size), :])
        xcat = jnp.concatenate(pieces, axis=1).astype(BF16)
        y = lax.dot_general(xcat, w.cwt[c], (((1,), (1,)), ((), ())), preferred_element_type=F32)
        st["g", c, start] = _gelu(y + w.dskip[:, _lanes(c)] * bufs.us[start:start + size, _lanes(c)])
    for start, size in mxu_blocks:
        for c in range(chunks):
            groups["s5_out"].append(functools.partial(s5_out, c, start, size))

    def lru_out(start, size):
        h = jnp.concatenate([bufs.b[_lru_time_rows(c, start, size), :] for c in range(LRU_CHUNKS)], axis=1)
        st["n_lru", start] = _rmsnorm(h * _gelu(bufs.gl[start:start + size, :]), w.glru[...]).astype(BF16)

    def glu_dot(start, size):
        g = jnp.concatenate([st["g", c, start] for c in range(chunks)], axis=1)
        st["g4", start] = g
        st["gate", start] = _dot(g.astype(BF16), w.wglu[...]) + w.bglu[...]

    def glu_ew(start, size):
        m0 = max(s for s, n in mxu_blocks if s <= start)
        g = st["g4", m0][start - m0:start - m0 + size, :]
        gate = st["gate", m0][start - m0:start - m0 + size, :]
        st["n_ssm", start] = _rmsnorm(g * jax.nn.sigmoid(gate), w.gssm[...]).astype(BF16)

    def out_proj(k, start, size):
        cols = slice(k * MXU_DIM, (k + 1) * MXU_DIM)
        y_ref[start:start + size, cols] = (
            x_ref[start:start + size, cols] + _dot(rows_of("n_lru", start, size), w.wout[0:LRU_W, cols])
            + _dot(rows_of("n_ssm", start, size), w.wout[LRU_W:, cols]))
    for start, size in mxu_blocks:
        inside = [(s, n) for s, n in vpu_blocks if start <= s < start + size]
        groups["glu_dot"][start] = functools.partial(glu_dot, start, size)
        groups["lru_out"][start] = [functools.partial(lru_out, s, n) for s, n in inside]
        groups["glu_ew"][start] = [functools.partial(glu_ew, s, n) for s, n in inside]
        groups["out_proj"][start] = [functools.partial(out_proj, k, start, size) for k in range(D_MODEL // MXU_DIM)]
    return groups


def _stage_c_tail(c_groups, fillers):
    starts = sorted(c_groups["glu_dot"])
    order = _zip_tasks([c_groups["glu_dot"][m] for m in starts], c_groups["lru_out"][starts[0]])
    order += c_groups["glu_ew"][starts[0]]
    for m, nxt in zip(starts, starts[1:] + [None]):
        companions = fillers if nxt is None else c_groups["lru_out"][nxt] + c_groups["glu_ew"][nxt]
        order += _zip_tasks(c_groups["out_proj"][m], companions) if companions else c_groups["out_proj"][m]
    return order


N_WEIGHTS = len(MixerWeights._fields)
N_BUFS = len(ChunkBufs._fields)


def _mixer_plain_kernel(nb, seq, x_ref, conv0_ref, lru0_ref, ssm0_ref, *rest):
    w = MixerWeights(*rest[:N_WEIGHTS])
    y_ref, convc_ref, lruc_ref, ssmc_ref, xp_ref = rest[N_WEIGHTS:N_WEIGHTS + 5]
    bufs = ChunkBufs(*rest[N_WEIGHTS + 5:])
    rows = nb * seq
    convc_ref[...] = conv0_ref[...]
    lruc_ref[...] = lru0_ref[...]
    ssmc_ref[...] = ssm0_ref[...]
    a = _stage_a_tasks(nb, seq, w, convc_ref, xp_ref, bufs, None)
    c = _stage_c_tasks(rows, x_ref, y_ref, w, bufs)
    for task in (_norm_tasks(rows, x_ref, w, bufs.h) + a["in_proj"] + a["conv_gates"] + a["s5_in"]
                 + _stage_l_tasks(nb, seq, w, lruc_ref, ssmc_ref, bufs, None)
                 + c["s5_out"] + _stage_c_tail(c, [])):
        task()


def _mixer_pipelined_kernel(n_chunks, seq, x_next_ref, x_prev_ref, conv0_ref, lru0_ref, ssm0_ref, *rest):
    w = MixerWeights(*rest[:N_WEIGHTS])
    y_ref, convc_ref, lruc_ref, ssmc_ref, xp_ref = rest[N_WEIGHTS:N_WEIGHTS + 5]
    sets = (ChunkBufs(*rest[N_WEIGHTS + 5:N_WEIGHTS + 5 + N_BUFS]),
            ChunkBufs(*rest[N_WEIGHTS + 5 + N_BUFS:]))
    step = pl.program_id(0)

    @pl.when(step == 0)
    def _():
        convc_ref[...] = conv0_ref[...]
        lruc_ref[...] = lru0_ref[...]
        ssmc_ref[...] = ssm0_ref[...]
        for ref in sets[1]:
            ref[...] = jnp.zeros(ref.shape, ref.dtype)
        for task in _norm_tasks(seq, x_prev_ref, w, sets[0].h):
            task()

    def body(write, read):
        a = _stage_a_tasks(1, seq, w, convc_ref, xp_ref, write, step < n_chunks)
        l = _stage_l_tasks(1, seq, w, lruc_ref, ssmc_ref, read, step >= 1)
        c = _stage_c_tasks(seq, x_prev_ref, y_ref, w, read)
        norm_next = _norm_tasks(seq, x_next_ref, w, read.h)
        for task in (_zip_tasks(l, a["in_proj"])
                     + _zip_tasks(c["s5_out"], _zip_tasks(a["conv_gates"], a["s5_in"]))
                     + _stage_c_tail(c, norm_next)):
            task()

    @pl.when(step % 2 == 0)
    def _():
        body(sets[0], sets[1])

    @pl.when(step % 2 == 1)
    def _():
        body(sets[1], sets[0])


def _full_spec(shape):
    zeros = (0,) * len(shape)
    return pl.BlockSpec(shape, lambda i, _z=zeros: _z)


def _layer_spec(shape, layer):
    index = (layer,) + (0,) * (len(shape) - 1)
    return pl.BlockSpec((None,) + tuple(shape[1:]), lambda i, _x=index: _x, pipeline_mode=pl.Buffered(1))


def _chunk_scratch(rows):
    xs_rows = -(-(S5_TIME_PITCH * rows + S5_ROW_PITCH * SUBLANES) // SUBLANES) * SUBLANES
    return [pltpu.VMEM((SSM_SLABS, xs_rows, LANES), F32),
            pltpu.VMEM((rows * LRU_CHUNKS, LANES), F32), pltpu.VMEM((rows * LRU_CHUNKS, LANES), F32),
            pltpu.VMEM((rows, LRU_W), F32), pltpu.VMEM((rows, SSM_W), F32),
            pltpu.VMEM((rows, D_MODEL), BF16)]


def _mixer_call(x2d, states, state_layer, w, layer, nb, seq, pipelined):
    total = x2d.shape[0]
    rows = nb * seq
    conv0, lru0, ssm0 = states
    state_in_specs = [_layer_spec(s.shape, state_layer) for s in states]
    state_specs = [_full_spec(s.shape[1:]) for s in states]
    weight_specs = [_layer_spec(a.shape, layer) for a in w]
    out_shape = (jax.ShapeDtypeStruct((total, D_MODEL), F32),) + tuple(
        jax.ShapeDtypeStruct(s.shape[1:], F32) for s in states)
    conv_scratch = pltpu.VMEM((nb, LRU_CHUNKS, CONV_PITCH * seq + CONV_BASE, LANES), F32)
    scratch = [conv_scratch] + _chunk_scratch(rows)
    params = pltpu.CompilerParams(dimension_semantics=("arbitrary",), vmem_limit_bytes=VMEM_LIMIT_BYTES)
    if not pipelined:
        assert total == rows
        x_spec = pl.BlockSpec((rows, D_MODEL), lambda i: (0, 0))
        return pl.pallas_call(
            functools.partial(_mixer_plain_kernel, nb, seq),
            grid=(1,), in_specs=[x_spec] + state_in_specs + weight_specs,
            out_specs=(x_spec,) + tuple(state_specs), out_shape=out_shape,
            scratch_shapes=scratch, name="mixer_plain", compiler_params=params,
        )(x2d, conv0, lru0, ssm0, *w)
    assert nb == 1 and total % rows == 0
    n_chunks = total // rows
    next_spec = pl.BlockSpec((rows, D_MODEL), lambda i: (jnp.minimum(i + 1, n_chunks - 1), 0))
    prev_spec = pl.BlockSpec((rows, D_MODEL), lambda i: (jnp.maximum(i - 1, 0), 0))
    return pl.pallas_call(
        functools.partial(_mixer_pipelined_kernel, n_chunks, seq),
        grid=(n_chunks + 1,), in_specs=[next_spec, prev_spec] + state_in_specs + weight_specs,
        out_specs=(prev_spec,) + tuple(state_specs), out_shape=out_shape,
        scratch_shapes=scratch + _chunk_scratch(rows), name="mixer", compiler_params=params,
    )(x2d, x2d, conv0, lru0, ssm0, *w)


def _ffn_rows(final_norm, x, gffn_ref, wg_ref, wu_ref, wd_ref, gfin_ref):
    h = _rmsnorm(x, gffn_ref[...]).astype(BF16)
    acc = x
    for c in range(D_FF // FF_CHUNK):
        lo, hi = c * FF_CHUNK, (c + 1) * FF_CHUNK
        gate = _dot(h, wg_ref[:, lo:hi])
        up = _dot(h, wu_ref[:, lo:hi])
        act = (gate * jax.nn.sigmoid(gate) * up).astype(BF16)
        acc = acc + _dot(act, wd_ref[lo:hi, :])
    return _rmsnorm(acc, gfin_ref[...]) if final_norm else acc


def _ffn_kernel(final_norm, main_steps, xa_ref, xb_ref, gffn_ref, wg_ref, wu_ref, wd_ref, gfin_ref,
                oa_ref, ob_ref):
    weights = (gffn_ref, wg_ref, wu_ref, wd_ref, gfin_ref)
    step = pl.program_id(0)

    @pl.when(step < main_steps)
    def _():
        oa_ref[...] = _ffn_rows(final_norm, xa_ref[...], *weights)

    @pl.when(step == main_steps)
    def _():
        ob_ref[...] = _ffn_rows(final_norm, xb_ref[...], *weights)


def _ffn_call(xa, xb, w, layer, gfin, final_norm, tile):
    main_steps = xa.shape[0] // tile
    tile_spec = pl.BlockSpec((tile, D_MODEL), lambda i: (jnp.minimum(i, main_steps - 1), 0))
    in_specs = [tile_spec, _full_spec(xb.shape)]
    in_specs += [_layer_spec(a.shape, layer) for a in w] + [_full_spec(gfin.shape)]
    return pl.pallas_call(
        functools.partial(_ffn_kernel, final_norm, main_steps),
        grid=(main_steps + 1,), in_specs=in_specs,
        out_specs=(tile_spec, _full_spec(xb.shape)),
        out_shape=(jax.ShapeDtypeStruct(xa.shape, F32), jax.ShapeDtypeStruct(xb.shape, F32)), name="ffn",
        compiler_params=pltpu.CompilerParams(dimension_semantics=("arbitrary",),
                                             vmem_limit_bytes=VMEM_LIMIT_BYTES),
    )(xa, xb, *w, gfin)


def _ssm_to_slabs(re, im):
    lead = re.shape[:-2]
    return jnp.concatenate([re.reshape(lead + (2, SUBLANES, LANES)), im.reshape(lead + (2, SUBLANES, LANES))],
                           axis=len(lead))


def _slabs_to_ssm(slabs):
    lead = slabs.shape[:-3]
    return (slabs[..., 0:2, :, :].reshape(lead + (SSM_G, SSM_P)), slabs[..., 2:4, :, :].reshape(lead + (SSM_G, SSM_P)))


def _conv_to_tile(conv):
    pad = [(0, 0)] * (conv.ndim - 2) + [(SUBLANES - (CONV_W - 1), 0), (0, 0)]
    return jnp.pad(conv, pad)


def _lru_to_tile(h):
    lead = h.shape[:-1]
    pad = [(0, 0)] * len(lead) + [(0, SUBLANES - LRU_CHUNKS), (0, 0)]
    return jnp.pad(h.reshape(lead + (LRU_CHUNKS, LANES)), pad)


def _tile_to_lru(tile):
    return tile[..., :LRU_CHUNKS, :].reshape(tile.shape[:-2] + (LRU_W,))


def kernel(x_prompt, x_sample, state_conv, state_lru, state_ssm_re, state_ssm_im, norm_mix, w_in, conv_w, conv_b, lru_wa, lru_ba, lru_wx, lru_bx, lru_lambda, ssm_lambda_re, ssm_lambda_im, ssm_b_re, ssm_b_im, ssm_c_re, ssm_c_im, ssm_d, ssm_log_dt, ssm_w_glu, ssm_b_glu, norm_lru_out, norm_ssm_out, w_out, norm_ffn, w_gate, w_up, w_down, norm_final):
    depth = w_in.shape[0]
    bp, tp, _ = x_prompt.shape
    bs, ts, _ = x_sample.shape
    assert bp == 1

    lbr, lbi, bw, cwt, wgate = _prep(ssm_lambda_re, ssm_lambda_im, ssm_log_dt, ssm_b_re, ssm_b_im,
                                     ssm_c_re, ssm_c_im, lru_wa, lru_wx)

    def rows(v):
        return v.reshape(depth, 1, -1)

    mixer_w = MixerWeights(
        gmix=rows(norm_mix), win=w_in.astype(BF16), convw=conv_w, convb=rows(conv_b),
        wgate=wgate, bgate=jnp.stack([lru_ba, lru_bx], axis=1), lam=rows(lru_lambda),
        lbr=lbr, lbi=lbi, bw=bw, cwt=cwt,
        dskip=rows(ssm_d), wglu=ssm_w_glu.astype(BF16), bglu=rows(ssm_b_glu),
        glru=rows(norm_lru_out), gssm=rows(norm_ssm_out), wout=w_out.astype(BF16))
    ffn_w = (rows(norm_ffn), w_gate.astype(BF16), w_up.astype(BF16), w_down.astype(BF16))
    gfin = norm_final.reshape(1, -1)

    yp = x_prompt.reshape(bp * tp, D_MODEL)
    ys = x_sample.reshape(bs * ts, D_MODEL)
    prompt_states = (jnp.zeros((1, bp, SUBLANES, LRU_W), F32), jnp.zeros((1, bp, SUBLANES, LANES), F32),
                     jnp.zeros((1, bp, SSM_SLABS, SUBLANES, LANES), F32))
    sample_states = (_conv_to_tile(state_conv), _lru_to_tile(state_lru), _ssm_to_slabs(state_ssm_re, state_ssm_im))

    prompt_out, sample_out = [], []
    for l in range(depth):
        last = l == depth - 1
        yp, *st_p = _mixer_call(yp, prompt_states, 0, mixer_w, l, 1, PROMPT_CHUNK, True)
        ys, *st_s = _mixer_call(ys, sample_states, l, mixer_w, l, bs, ts, False)
        yp, ys = _ffn_call(yp, ys, ffn_w, l, gfin, last, FFN_TILE)
        prompt_out.append(st_p)
        sample_out.append(st_s)

    def unpack(per_layer):
        conv, lru, ssm = (jnp.stack(v) for v in zip(*per_layer))
        re, im = _slabs_to_ssm(ssm)
        return conv[:, :, SUBLANES - (CONV_W - 1):], _tile_to_lru(lru), re, im

    return (yp.reshape(bp, tp, D_MODEL), ys.reshape(bs, ts, D_MODEL)) + unpack(prompt_out) + unpack(sample_out)
```

```python
import collections
import functools
import math

import jax
import jax.numpy as jnp
from jax import lax
from jax.experimental import pallas as pl
from jax.experimental.pallas import tpu as pltpu

D_MODEL = 1024
LRU_W = 512
LRU_HEADS = 8
LRU_HD = LRU_W // LRU_HEADS
CONV_W = 4
LRU_C = 8.0
SSM_W = 512
SSM_H = 16
SSM_G = SSM_W // SSM_H
SSM_P = 64
IN_W = 2 * LRU_W + SSM_W
D_FF = 2816
EPS = 1e-6

SUBLANES = 8
LANES = 128
MXU_DIM = 256
VMEM_LIMIT_BYTES = 56 * 1024 * 1024

SSM_SLABS = 4
S5_ROWS_PER_CHUNK = LANES // (2 * SSM_H)
S5_TIME_PITCH = 9
S5_ROW_PITCH = 2
LRU_CHUNKS = LRU_W // LANES
CONV_PITCH = 2
CONV_BASE = CONV_PITCH * SUBLANES
GATE_HALVES = LRU_W // MXU_DIM
FF_CHUNK = MXU_DIM
PROMPT_CHUNK = 512
VPU_TASK_ROWS = 512
MXU_TASK_ROWS = 512
FFN_TILE = 1024
BF16 = jnp.bfloat16
F32 = jnp.float32

MixerWeights = collections.namedtuple(
    "MixerWeights",
    "vec win convw wgate lbr lbi bw cwt wglu wout")
VEC_FIELDS = (("gmix", D_MODEL), ("convb", LRU_W), ("lam", LRU_W), ("ba", LRU_W), ("bx", LRU_W),
              ("dskip", SSM_W), ("bglu", SSM_W), ("glru", LRU_W), ("gssm", SSM_W))
VEC_OFFSET = {name: sum(width for _, width in VEC_FIELDS[:i]) for i, (name, _) in enumerate(VEC_FIELDS)}
VEC_WIDTH = dict(VEC_FIELDS)
ChunkBufs = collections.namedtuple("ChunkBufs", "xs a b gl us h")


def _rmsnorm(x, gain):
    var = jnp.mean(x * x, axis=-1, keepdims=True)
    return x * lax.rsqrt(var + EPS) * gain


def _dot(a, b):
    return jnp.dot(a, b, preferred_element_type=F32)


_GELU_K1 = -2.0 * math.sqrt(2.0 / math.pi) * math.log2(math.e)
_GELU_K2 = _GELU_K1 * 0.044715


def _gelu(x):
    return x / (1.0 + jnp.exp2(x * (_GELU_K1 + _GELU_K2 * (x * x))))


def _keep_if(pred, new, old):
    return new if pred is None else jnp.where(pred, new, old)


def _lanes(c):
    return slice(c * LANES, (c + 1) * LANES)


def _vec(w, name, lo=0, hi=None):
    hi = VEC_WIDTH[name] if hi is None else hi
    return w.vec[:, VEC_OFFSET[name] + lo:VEC_OFFSET[name] + hi]


def _prep_kernel(depth, lr_ref, li_ref, ldt_ref, br_ref, bi_ref, cr_ref, ci_ref, wa_ref, wx_ref,
                 lbr_ref, lbi_ref, bw_ref, cwt_ref, wgate_ref):
    lr = lr_ref[...]
    li = li_ref[...]
    dt = jnp.exp(ldt_ref[...])
    mag = jnp.exp(lr * dt)
    lbr = mag * jnp.cos(li * dt)
    lbi = mag * jnp.sin(li * dt)
    nr, ni = lbr - 1.0, lbi
    den = lr * lr + li * li
    gr = (nr * lr + ni * li) / den
    gi = (ni * lr - nr * li) / den
    br = br_ref[...]
    bi = bi_ref[...]
    lbr_ref[...] = lbr
    lbi_ref[...] = lbi
    bb = ((gr * br - gi * bi).astype(BF16), (gr * bi + gi * br).astype(BF16))
    cc = (cr_ref[...].astype(BF16), (-ci_ref[...]).astype(BF16))
    bw_ref[...] = jnp.zeros(bw_ref.shape, BF16)
    cwt_ref[...] = jnp.zeros(cwt_ref.shape, BF16)
    wgate_ref[...] = jnp.zeros(wgate_ref.shape, BF16)
    chunks = SSM_W // LANES
    for l in range(depth):
        for g in range(SSM_G):
            s, gg = divmod(g, 2)
            c, sl = divmod(s, S5_ROWS_PER_CHUNK)
            src = slice((l * SSM_G + g) * SSM_H, (l * SSM_G + g + 1) * SSM_H)
            lanes_u = slice((sl * 2 + gg) * SSM_H, (sl * 2 + gg + 1) * SSM_H)
            for reim in range(2):
                col = (sl * 2 + reim) * LANES + gg * SSM_P
                bw_ref[l * chunks + c, lanes_u, col:col + SSM_P] = bb[reim][src, :]
                cwt_ref[l * chunks + c, lanes_u, col:col + SSM_P] = cc[reim][src, :]
        per = LRU_HEADS // GATE_HALVES
        for head in range(LRU_HEADS):
            hf, hh = divmod(head, per)
            src = slice((l * LRU_HEADS + head) * LRU_HD, (l * LRU_HEADS + head + 1) * LRU_HD)
            blk = slice(hh * LRU_HD, (hh + 1) * LRU_HD)
            wgate_ref[l * GATE_HALVES + hf, blk, blk] = wa_ref[src, :].astype(BF16)
            wgate_ref[l * GATE_HALVES + hf, blk, MXU_DIM + hh * LRU_HD:MXU_DIM + (hh + 1) * LRU_HD] = (
                wx_ref[src, :].astype(BF16))


def _prep(lam_re, lam_im, log_dt, b_re, b_im, c_re, c_im, wa, wx):
    depth = lam_re.shape[0]
    rows = depth * SSM_G * SSM_H
    shape4 = (depth, SSM_G, SSM_H, SSM_P)

    def expand(a):
        return jnp.broadcast_to(a[:, :, None, :], shape4).reshape(rows, SSM_P)

    ldt = jnp.broadcast_to(log_dt[:, :, None, None], shape4).reshape(rows, SSM_P)
    br = jnp.swapaxes(b_re, 2, 3).reshape(rows, SSM_P)
    bi = jnp.swapaxes(b_im, 2, 3).reshape(rows, SSM_P)
    chunks = SSM_W // LANES
    sds = jax.ShapeDtypeStruct((rows, SSM_P), F32)
    out_shape = (sds, sds,
                 jax.ShapeDtypeStruct((depth * chunks, LANES, S5_ROWS_PER_CHUNK * 2 * LANES), BF16),
                 jax.ShapeDtypeStruct((depth * chunks, LANES, S5_ROWS_PER_CHUNK * 2 * LANES), BF16),
                 jax.ShapeDtypeStruct((depth * GATE_HALVES, MXU_DIM, 2 * MXU_DIM), BF16))
    lbr, lbi, bw, cwt, wgate = pl.pallas_call(
        functools.partial(_prep_kernel, depth), out_shape=out_shape, name="prep",
    )(expand(lam_re), expand(lam_im), ldt, br, bi, c_re.reshape(rows, SSM_P), c_im.reshape(rows, SSM_P),
      wa.reshape(depth * LRU_W, LRU_HD), wx.reshape(depth * LRU_W, LRU_HD))
    lbr = lbr.reshape(shape4)[:, :, 0, :].reshape(depth, 2, SUBLANES, LANES)
    lbi = lbi.reshape(shape4)[:, :, 0, :].reshape(depth, 2, SUBLANES, LANES)
    return (lbr, lbi, bw.reshape((depth, chunks) + bw.shape[1:]), cwt.reshape((depth, chunks) + cwt.shape[1:]),
            wgate.reshape((depth, GATE_HALVES) + wgate.shape[1:]))


def _s5_time_rows(srow, row0, count):
    return pl.ds(S5_ROW_PITCH * srow + S5_TIME_PITCH * row0, count, stride=S5_TIME_PITCH)


def _s5_step_rows(r):
    return pl.ds(S5_TIME_PITCH * r, SUBLANES, stride=S5_ROW_PITCH)


def _lru_time_rows(c, row0, count):
    return pl.ds(c + LRU_CHUNKS * row0, count, stride=LRU_CHUNKS)


def _blocks(rows, size):
    size = min(size, rows)
    return [(start, size) for start in range(0, rows, size)]


def _zip_tasks(a, b):
    keyed = [((i + 0.5) / len(a), 0, i, t) for i, t in enumerate(a)]
    keyed += [((i + 0.5) / len(b), 1, i, t) for i, t in enumerate(b)]
    return [t for _, _, _, t in sorted(keyed, key=lambda e: e[:3])]


def _norm_tasks(rows, x_ref, w, h_ref):
    def norm(start, size):
        h_ref[start:start + size, :] = _rmsnorm(x_ref[start:start + size, :], _vec(w, "gmix")).astype(BF16)
    return [functools.partial(norm, start, size) for start, size in _blocks(rows, VPU_TASK_ROWS)]


def _stage_a_tasks(nb, seq, w, convc_ref, xp_ref, bufs, live):
    rows = nb * seq
    vpu_blocks = _blocks(rows, VPU_TASK_ROWS)
    mxu_blocks = _blocks(rows, MXU_TASK_ROWS)
    st = {}
    groups = {"in_proj": [], "conv_gates": [], "s5_in": []}

    def in_proj(k, start, size):
        z = _dot(bufs.h[start:start + size, :], w.win[:, k * MXU_DIM:(k + 1) * MXU_DIM])
        st["z", k, start] = z
        if k in (2, 3):
            bufs.gl[start:start + size, (k - 2) * MXU_DIM:(k - 1) * MXU_DIM] = z
        if k in (4, 5):
            bufs.us[start:start + size, (k - 4) * MXU_DIM:(k - 3) * MXU_DIM] = z
    for k in range(IN_W // MXU_DIM):
        for start, size in mxu_blocks:
            groups["in_proj"].append(functools.partial(in_proj, k, start, size))

    def z_lanes(k0, c):
        zk, off = divmod(c * LANES, MXU_DIM)
        parts = [st["z", k0 + zk, s][:, off:off + LANES] for s, n in mxu_blocks]
        return parts[0] if len(parts) == 1 else jnp.concatenate(parts, axis=0)

    def conv(c):
        xl = z_lanes(0, c)
        cw = w.convw[:, _lanes(c)]
        parts = []
        for j in range(nb):
            xl_j = xl[j * seq:(j + 1) * seq, :]
            xp_ref[j, c, pl.ds(0, SUBLANES, stride=CONV_PITCH), :] = convc_ref[j, :, _lanes(c)]
            xp_ref[j, c, pl.ds(CONV_BASE, seq, stride=CONV_PITCH), :] = xl_j
            acc = _vec(w, "convb", c * LANES, (c + 1) * LANES) + xl_j * cw[CONV_W - 1:CONV_W, :]
            for k in range(CONV_W - 1):
                start = CONV_BASE - CONV_PITCH * (CONV_W - 1 - k)
                acc = acc + xp_ref[j, c, pl.ds(start, seq, stride=CONV_PITCH), :] * cw[k:k + 1, :]
            parts.append(acc)
            convc_ref[j, :, _lanes(c)] = _keep_if(live, xl_j[seq - SUBLANES:, :], convc_ref[j, :, _lanes(c)])
        st["xc", c] = parts[0] if nb == 1 else jnp.concatenate(parts, axis=0)
    for c in range(LRU_CHUNKS):
        groups["conv_gates"].append(functools.partial(conv, c))

    def gates_dot(hf):
        chunks = range(hf * MXU_DIM // LANES, (hf + 1) * MXU_DIM // LANES)
        xc = jnp.concatenate([st["xc", c] for c in chunks], axis=1)
        st["xc2", hf] = xc
        st["pre", hf] = _dot(xc.astype(BF16), w.wgate[hf])

    def gates_ew(hf, start, size):
        lo, hi = hf * MXU_DIM, (hf + 1) * MXU_DIM
        lam = _vec(w, "lam", lo, hi)
        neg_c_softplus = -LRU_C * (jnp.maximum(-lam, 0.0) + jnp.log1p(jnp.exp(-jnp.abs(lam))))
        pre = st["pre", hf][start:start + size, :]
        xc = st["xc2", hf][start:start + size, :]
        r = jax.nn.sigmoid(pre[:, :MXU_DIM] + _vec(w, "ba", lo, hi))
        ig = jax.nn.sigmoid(pre[:, MXU_DIM:] + _vec(w, "bx", lo, hi))
        log_a = neg_c_softplus * r
        a = jnp.exp(log_a)
        one_minus_a2 = -jnp.tanh(log_a) * (a * a + 1.0)
        b = jnp.sqrt(one_minus_a2) * (ig * xc)
        for i in range(MXU_DIM // LANES):
            c = hf * MXU_DIM // LANES + i
            bufs.a[_lru_time_rows(c, start, size), :] = a[:, _lanes(i)]
            bufs.b[_lru_time_rows(c, start, size), :] = b[:, _lanes(i)]
    for hf in range(GATE_HALVES):
        groups["conv_gates"].append(functools.partial(gates_dot, hf))
        for start, size in vpu_blocks:
            groups["conv_gates"].append(functools.partial(gates_ew, hf, start, size))

    def s5_in(c, start, size):
        zk, off = divmod(c * LANES, MXU_DIM)
        us_bf = st["z", 4 + zk, start][:, off:off + LANES].astype(BF16)
        bu = _dot(us_bf, w.bw[c])
        half = (c * S5_ROWS_PER_CHUNK) // SUBLANES
        for sl in range(S5_ROWS_PER_CHUNK):
            srow = (c * S5_ROWS_PER_CHUNK + sl) % SUBLANES
            for reim in range(2):
                col = (sl * 2 + reim) * LANES
                bufs.xs[reim * 2 + half, _s5_time_rows(srow, start, size), :] = bu[:, col:col + LANES]
    for c in range(SSM_W // LANES):
        for start, size in mxu_blocks:
            groups["s5_in"].append(functools.partial(s5_in, c, start, size))
    return groups


def _stage_l_tasks(nb, seq, w, lruc_ref, ssmc_ref, bufs, live):
    st = {}
    tasks = []
    row = lambda: lax.broadcasted_iota(jnp.int32, (SUBLANES, LANES), 0)

    def begin(j):
        st["lb"] = ((w.lbr[0], w.lbr[1]), (w.lbi[0], w.lbi[1]))
        st["xr"] = [ssmc_ref[j, 0], ssmc_ref[j, 1]]
        st["xi"] = [ssmc_ref[j, 2], ssmc_ref[j, 3]]
        st["h"] = lruc_ref[j]
        st["old"] = (tuple(st["xr"]), tuple(st["xi"]), st["h"])

    def steps(j, tile):
        lbr, lbi = st["lb"]
        xr, xi = st["xr"], st["xi"]
        for t in range(tile * SUBLANES, (tile + 1) * SUBLANES):
            rows_t = _s5_step_rows(j * seq + t)
            for hv in range(2):
                nr = lbr[hv] * xr[hv] - lbi[hv] * xi[hv] + bufs.xs[hv, rows_t, :]
                ni = lbr[hv] * xi[hv] + lbi[hv] * xr[hv] + bufs.xs[2 + hv, rows_t, :]
                bufs.xs[hv, rows_t, :] = nr
                bufs.xs[2 + hv, rows_t, :] = ni
                xr[hv], xi[hv] = nr, ni
        first_half = row() < LRU_CHUNKS
        h = st["h"]
        for pair in range(tile * SUBLANES // 2, (tile + 1) * SUBLANES // 2):
            r0 = (j * seq + 2 * pair) * LRU_CHUNKS
            a2 = bufs.a[r0:r0 + SUBLANES, :]
            b2 = bufs.b[r0:r0 + SUBLANES, :]
            h_even = a2 * h + b2
            h_odd = a2 * pltpu.roll(h_even, LRU_CHUNKS, 0) + b2
            bufs.b[r0:r0 + SUBLANES, :] = jnp.where(first_half, h_even, h_odd)
            h = pltpu.roll(h_odd, LRU_CHUNKS, 0)
        st["h"] = h

    def end(j):
        old = st["old"]
        for hv in range(2):
            ssmc_ref[j, hv] = _keep_if(live, st["xr"][hv], old[0][hv])
            ssmc_ref[j, 2 + hv] = _keep_if(live, st["xi"][hv], old[1][hv])
        lruc_ref[j] = _keep_if(live, st["h"], old[2])

    for j in range(nb):
        tasks.append(functools.partial(begin, j))
        for tile in range(seq // SUBLANES):
            tasks.append(functools.partial(steps, j, tile))
        tasks.append(functools.partial(end, j))
    return tasks


def _stage_c_tasks(rows, x_ref, y_ref, w, bufs):
    vpu_blocks = _blocks(rows, VPU_TASK_ROWS)
    mxu_blocks = _blocks(rows, MXU_TASK_ROWS)
    chunks = SSM_W // LANES
    st = {}
    groups = {"s5_out": [], "glu_dot": {}, "lru_out": {}, "glu_ew": {}, "out_proj": {}}

    def rows_of(key, start, size):
        parts = [st[key, s] for s, n in vpu_blocks if start <= s < start + size]
        return parts[0] if len(parts) == 1 else jnp.concatenate(parts, axis=0)

    def s5_out(c, start, size):
        pieces = []
        for sl in range(S5_ROWS_PER_CHUNK):
            s = c * S5_ROWS_PER_CHUNK + sl
            half, srow = s // SUBLANES, s % SUBLANES
            pieces.append(bufs.xs[half, _s5_time_rows(srow, start, size), :])
            pieces.append(bufs.xs[2 + half, _s5_time_rows(srow, start, size), :])
        xcat = jnp.concatenate(pieces, axis=1).astype(BF16)
        y = lax.dot_general(xcat, w.cwt[c], (((1,), (1,)), ((), ())), preferred_element_type=F32)
        st["g", c, start] = _gelu(y + _vec(w, "dskip", c * LANES, (c + 1) * LANES) * bufs.us[start:start + size, _lanes(c)])
    for start, size in mxu_blocks:
        for c in range(chunks):
            groups["s5_out"].append(functools.partial(s5_out, c, start, size))

    def lru_out(start, size):
        h = jnp.concatenate([bufs.b[_lru_time_rows(c, start, size), :] for c in range(LRU_CHUNKS)], axis=1)
        st["n_lru", start] = _rmsnorm(h * _gelu(bufs.gl[start:start + size, :]), _vec(w, "glru")).astype(BF16)

    def glu_dot(start, size):
        g = jnp.concatenate([st["g", c, start] for c in range(chunks)], axis=1)
        st["g4", start] = g
        st["gate", start] = _dot(g.astype(BF16), w.wglu[...]) + _vec(w, "bglu")

    def glu_ew(start, size):
        m0 = max(s for s, n in mxu_blocks if s <= start)
        g = st["g4", m0][start - m0:start - m0 + size, :]
        gate = st["gate", m0][start - m0:start - m0 + size, :]
        st["n_ssm", start] = _rmsnorm(g * jax.nn.sigmoid(gate), _vec(w, "gssm")).astype(BF16)

    def out_proj(k, start, size):
        cols = slice(k * MXU_DIM, (k + 1) * MXU_DIM)
        y_ref[start:start + size, cols] = (
            x_ref[start:start + size, cols] + _dot(rows_of("n_lru", start, size), w.wout[0:LRU_W, cols])
            + _dot(rows_of("n_ssm", start, size), w.wout[LRU_W:, cols]))
    for start, size in mxu_blocks:
        inside = [(s, n) for s, n in vpu_blocks if start <= s < start + size]
        groups["glu_dot"][start] = functools.partial(glu_dot, start, size)
        groups["lru_out"][start] = [functools.partial(lru_out, s, n) for s, n in inside]
        groups["glu_ew"][start] = [functools.partial(glu_ew, s, n) for s, n in inside]
        groups["out_proj"][start] = [functools.partial(out_proj, k, start, size) for k in range(D_MODEL // MXU_DIM)]
    return groups


def _stage_c_tail(c_groups, fillers):
    starts = sorted(c_groups["glu_dot"])
    order = _zip_tasks([c_groups["glu_dot"][m] for m in starts], c_groups["lru_out"][starts[0]])
    order += c_groups["glu_ew"][starts[0]]
    for m, nxt in zip(starts, starts[1:] + [None]):
        companions = fillers if nxt is None else c_groups["lru_out"][nxt] + c_groups["glu_ew"][nxt]
        order += _zip_tasks(c_groups["out_proj"][m], companions) if companions else c_groups["out_proj"][m]
    return order


N_WEIGHTS = len(MixerWeights._fields)
N_BUFS = len(ChunkBufs._fields)


def _mixer_plain_kernel(nb, seq, x_ref, conv0_ref, lru0_ref, ssm0_ref, *rest):
    w = MixerWeights(*rest[:N_WEIGHTS])
    y_ref, convc_ref, lruc_ref, ssmc_ref, xp_ref = rest[N_WEIGHTS:N_WEIGHTS + 5]
    bufs = ChunkBufs(*rest[N_WEIGHTS + 5:])
    rows = nb * seq
    convc_ref[...] = conv0_ref[...]
    lruc_ref[...] = lru0_ref[...]
    ssmc_ref[...] = ssm0_ref[...]
    a = _stage_a_tasks(nb, seq, w, convc_ref, xp_ref, bufs, None)
    c = _stage_c_tasks(rows, x_ref, y_ref, w, bufs)
    for task in (_norm_tasks(rows, x_ref, w, bufs.h) + a["in_proj"] + a["conv_gates"] + a["s5_in"]
                 + _stage_l_tasks(nb, seq, w, lruc_ref, ssmc_ref, bufs, None)
                 + c["s5_out"] + _stage_c_tail(c, [])):
        task()


def _mixer_pipelined_kernel(n_chunks, seq, x_next_ref, x_prev_ref, conv0_ref, lru0_ref, ssm0_ref, *rest):
    w = MixerWeights(*rest[:N_WEIGHTS])
    y_ref, convc_ref, lruc_ref, ssmc_ref, xp_ref = rest[N_WEIGHTS:N_WEIGHTS + 5]
    sets = (ChunkBufs(*rest[N_WEIGHTS + 5:N_WEIGHTS + 5 + N_BUFS]),
            ChunkBufs(*rest[N_WEIGHTS + 5 + N_BUFS:]))
    step = pl.program_id(0)

    @pl.when(step == 0)
    def _():
        convc_ref[...] = conv0_ref[...]
        lruc_ref[...] = lru0_ref[...]
        ssmc_ref[...] = ssm0_ref[...]
        for ref in sets[1]:
            ref[...] = jnp.zeros(ref.shape, ref.dtype)
        for task in _norm_tasks(seq, x_prev_ref, w, sets[0].h):
            task()

    def body(write, read):
        a = _stage_a_tasks(1, seq, w, convc_ref, xp_ref, write, step < n_chunks)
        l = _stage_l_tasks(1, seq, w, lruc_ref, ssmc_ref, read, step >= 1)
        c = _stage_c_tasks(seq, x_prev_ref, y_ref, w, read)
        norm_next = _norm_tasks(seq, x_next_ref, w, read.h)
        for task in (_zip_tasks(l, a["in_proj"])
                     + _zip_tasks(c["s5_out"], _zip_tasks(a["conv_gates"], a["s5_in"]))
                     + _stage_c_tail(c, norm_next)):
            task()

    @pl.when(step % 2 == 0)
    def _():
        body(sets[0], sets[1])

    @pl.when(step % 2 == 1)
    def _():
        body(sets[1], sets[0])


def _full_spec(shape):
    zeros = (0,) * len(shape)
    return pl.BlockSpec(shape, lambda i, _z=zeros: _z)


def _layer_spec(shape, layer):
    index = (layer,) + (0,) * (len(shape) - 1)
    return pl.BlockSpec((None,) + tuple(shape[1:]), lambda i, _x=index: _x, pipeline_mode=pl.Buffered(1))


def _chunk_scratch(rows):
    xs_rows = -(-(S5_TIME_PITCH * rows + S5_ROW_PITCH * SUBLANES) // SUBLANES) * SUBLANES
    return [pltpu.VMEM((SSM_SLABS, xs_rows, LANES), F32),
            pltpu.VMEM((rows * LRU_CHUNKS, LANES), F32), pltpu.VMEM((rows * LRU_CHUNKS, LANES), F32),
            pltpu.VMEM((rows, LRU_W), F32), pltpu.VMEM((rows, SSM_W), F32),
            pltpu.VMEM((rows, D_MODEL), BF16)]


def _mixer_call(x2d, states, state_layer, w, layer, nb, seq, pipelined):
    total = x2d.shape[0]
    rows = nb * seq
    conv0, lru0, ssm0 = states
    state_in_specs = [_layer_spec(s.shape, state_layer) for s in states]
    state_specs = [_full_spec(s.shape[1:]) for s in states]
    weight_specs = [_layer_spec(a.shape, layer) for a in w]
    out_shape = (jax.ShapeDtypeStruct((total, D_MODEL), F32),) + tuple(
        jax.ShapeDtypeStruct(s.shape[1:], F32) for s in states)
    conv_scratch = pltpu.VMEM((nb, LRU_CHUNKS, CONV_PITCH * seq + CONV_BASE, LANES), F32)
    scratch = [conv_scratch] + _chunk_scratch(rows)
    params = pltpu.CompilerParams(dimension_semantics=("arbitrary",), vmem_limit_bytes=VMEM_LIMIT_BYTES)
    if not pipelined:
        assert total == rows
        x_spec = pl.BlockSpec((rows, D_MODEL), lambda i: (0, 0))
        return pl.pallas_call(
            functools.partial(_mixer_plain_kernel, nb, seq),
            grid=(1,), in_specs=[x_spec] + state_in_specs + weight_specs,
            out_specs=(x_spec,) + tuple(state_specs), out_shape=out_shape,
            scratch_shapes=scratch, name="mixer_plain", compiler_params=params,
        )(x2d, conv0, lru0, ssm0, *w)
    assert nb == 1 and total % rows == 0
    n_chunks = total // rows
    next_spec = pl.BlockSpec((rows, D_MODEL), lambda i: (jnp.minimum(i + 1, n_chunks - 1), 0))
    prev_spec = pl.BlockSpec((rows, D_MODEL), lambda i: (jnp.maximum(i - 1, 0), 0))
    return pl.pallas_call(
        functools.partial(_mixer_pipelined_kernel, n_chunks, seq),
        grid=(n_chunks + 1,), in_specs=[next_spec, prev_spec] + state_in_specs + weight_specs,
        out_specs=(prev_spec,) + tuple(state_specs), out_shape=out_shape,
        scratch_shapes=scratch + _chunk_scratch(rows), name="mixer", compiler_params=params,
    )(x2d, x2d, conv0, lru0, ssm0, *w)


def _ffn_rows(final_norm, x, gffn_ref, wg_ref, wu_ref, wd_ref, gfin_ref):
    h = _rmsnorm(x, gffn_ref[...]).astype(BF16)
    acc = x
    for c in range(D_FF // FF_CHUNK):
        lo, hi = c * FF_CHUNK, (c + 1) * FF_CHUNK
        gate = _dot(h, wg_ref[:, lo:hi])
        up = _dot(h, wu_ref[:, lo:hi])
        act = (gate * jax.nn.sigmoid(gate) * up).astype(BF16)
        acc = acc + _dot(act, wd_ref[lo:hi, :])
    return _rmsnorm(acc, gfin_ref[...]) if final_norm else acc


def _ffn_kernel(final_norm, main_steps, xa_ref, xb_ref, gffn_ref, wg_ref, wu_ref, wd_ref, gfin_ref,
                oa_ref, ob_ref):
    weights = (gffn_ref, wg_ref, wu_ref, wd_ref, gfin_ref)
    step = pl.program_id(0)

    @pl.when(step < main_steps)
    def _():
        oa_ref[...] = _ffn_rows(final_norm, xa_ref[...], *weights)

    @pl.when(step == main_steps)
    def _():
        ob_ref[...] = _ffn_rows(final_norm, xb_ref[...], *weights)


def _ffn_call(xa, xb, w, layer, gfin, final_norm, tile):
    main_steps = xa.shape[0] // tile
    tile_spec = pl.BlockSpec((tile, D_MODEL), lambda i: (jnp.minimum(i, main_steps - 1), 0))
    in_specs = [tile_spec, _full_spec(xb.shape)]
    in_specs += [_layer_spec(a.shape, layer) for a in w] + [_full_spec(gfin.shape)]
    return pl.pallas_call(
        functools.partial(_ffn_kernel, final_norm, main_steps),
        grid=(main_steps + 1,), in_specs=in_specs,
        out_specs=(tile_spec, _full_spec(xb.shape)),
        out_shape=(jax.ShapeDtypeStruct(xa.shape, F32), jax.ShapeDtypeStruct(xb.shape, F32)), name="ffn",
        compiler_params=pltpu.CompilerParams(dimension_semantics=("arbitrary",),
                                             vmem_limit_bytes=VMEM_LIMIT_BYTES),
    )(xa, xb, *w, gfin)


def _ssm_to_slabs(re, im):
    lead = re.shape[:-2]
    return jnp.concatenate([re.reshape(lead + (2, SUBLANES, LANES)), im.reshape(lead + (2, SUBLANES, LANES))],
                           axis=len(lead))


def _slabs_to_ssm(slabs):
    lead = slabs.shape[:-3]
    return (slabs[..., 0:2, :, :].reshape(lead + (SSM_G, SSM_P)), slabs[..., 2:4, :, :].reshape(lead + (SSM_G, SSM_P)))


def _conv_to_tile(conv):
    pad = [(0, 0)] * (conv.ndim - 2) + [(SUBLANES - (CONV_W - 1), 0), (0, 0)]
    return jnp.pad(conv, pad)


def _lru_to_tile(h):
    lead = h.shape[:-1]
    pad = [(0, 0)] * len(lead) + [(0, SUBLANES - LRU_CHUNKS), (0, 0)]
    return jnp.pad(h.reshape(lead + (LRU_CHUNKS, LANES)), pad)


def _tile_to_lru(tile):
    return tile[..., :LRU_CHUNKS, :].reshape(tile.shape[:-2] + (LRU_W,))


def kernel(x_prompt, x_sample, state_conv, state_lru, state_ssm_re, state_ssm_im, norm_mix, w_in, conv_w, conv_b, lru_wa, lru_ba, lru_wx, lru_bx, lru_lambda, ssm_lambda_re, ssm_lambda_im, ssm_b_re, ssm_b_im, ssm_c_re, ssm_c_im, ssm_d, ssm_log_dt, ssm_w_glu, ssm_b_glu, norm_lru_out, norm_ssm_out, w_out, norm_ffn, w_gate, w_up, w_down, norm_final):
    depth = w_in.shape[0]
    bp, tp, _ = x_prompt.shape
    bs, ts, _ = x_sample.shape
    assert bp == 1

    lbr, lbi, bw, cwt, wgate = _prep(ssm_lambda_re, ssm_lambda_im, ssm_log_dt, ssm_b_re, ssm_b_im,
                                     ssm_c_re, ssm_c_im, lru_wa, lru_wx)

    def rows(v):
        return v.reshape(depth, 1, -1)

    by_name = dict(gmix=norm_mix, convb=conv_b, lam=lru_lambda, ba=lru_ba, bx=lru_bx, dskip=ssm_d,
                   bglu=ssm_b_glu, glru=norm_lru_out, gssm=norm_ssm_out)
    mixer_w = MixerWeights(
        vec=rows(jnp.concatenate([by_name[name] for name, _ in VEC_FIELDS], axis=1)),
        win=w_in.astype(BF16), convw=conv_w, wgate=wgate, lbr=lbr, lbi=lbi, bw=bw, cwt=cwt,
        wglu=ssm_w_glu.astype(BF16), wout=w_out.astype(BF16))
    ffn_w = (rows(norm_ffn), w_gate.astype(BF16), w_up.astype(BF16), w_down.astype(BF16))
    gfin = norm_final.reshape(1, -1)

    yp = x_prompt.reshape(bp * tp, D_MODEL)
    ys = x_sample.reshape(bs * ts, D_MODEL)
    prompt_states = (jnp.zeros((1, bp, SUBLANES, LRU_W), F32), jnp.zeros((1, bp, SUBLANES, LANES), F32),
                     jnp.zeros((1, bp, SSM_SLABS, SUBLANES, LANES), F32))
    sample_states = (_conv_to_tile(state_conv), _lru_to_tile(state_lru), _ssm_to_slabs(state_ssm_re, state_ssm_im))

    prompt_out, sample_out = [], []
    for l in range(depth):
        last = l == depth - 1
        yp, *st_p = _mixer_call(yp, prompt_states, 0, mixer_w, l, 1, PROMPT_CHUNK, True)
        ys, *st_s = _mixer_call(ys, sample_states, l, mixer_w, l, bs, ts, False)
        yp, ys = _ffn_call(yp, ys, ffn_w, l, gfin, last, FFN_TILE)
        prompt_out.append(st_p)
        sample_out.append(st_s)

    def unpack(per_layer):
        conv, lru, ssm = (jnp.stack(v) for v in zip(*per_layer))
        re, im = _slabs_to_ssm(ssm)
        return conv[:, :, SUBLANES - (CONV_W - 1):], _tile_to_lru(lru), re, im

    return (yp.reshape(bp, tp, D_MODEL), ys.reshape(bs, ts, D_MODEL)) + unpack(prompt_out) + unpack(sample_out)
```

```python
import collections
import functools
import math

import jax
import jax.numpy as jnp
from jax import lax
from jax.experimental import pallas as pl
from jax.experimental.pallas import tpu as pltpu

D_MODEL = 1024
LRU_W = 512
LRU_HEADS = 8
LRU_HD = LRU_W // LRU_HEADS
CONV_W = 4
LRU_C = 8.0
SSM_W = 512
SSM_H = 16
SSM_G = SSM_W // SSM_H
SSM_P = 64
IN_W = 2 * LRU_W + SSM_W
D_FF = 2816
EPS = 1e-6

SUBLANES = 8
LANES = 128
MXU_DIM = 256
VMEM_LIMIT_BYTES = 56 * 1024 * 1024

SSM_SLABS = 4
S5_ROWS_PER_CHUNK = LANES // (2 * SSM_H)
S5_TIME_PITCH = 9
S5_ROW_PITCH = 2
LRU_CHUNKS = LRU_W // LANES
CONV_PITCH = 2
CONV_BASE = CONV_PITCH * SUBLANES
GATE_HALVES = LRU_W // MXU_DIM
FF_CHUNK = MXU_DIM
PROMPT_CHUNK = 512
VPU_TASK_ROWS = 512
MXU_TASK_ROWS = 512
FFN_TILE = 1024
WEIGHT_STAGE_ROWS = 128
BF16 = jnp.bfloat16
F32 = jnp.float32

MixerWeights = collections.namedtuple(
    "MixerWeights",
    "vec win convw wgate lbr lbi bw cwt wglu wout")
VEC_FIELDS = (("gmix", D_MODEL), ("convb", LRU_W), ("lam", LRU_W), ("ba", LRU_W), ("bx", LRU_W),
              ("dskip", SSM_W), ("bglu", SSM_W), ("glru", LRU_W), ("gssm", SSM_W))
VEC_OFFSET = {name: sum(width for _, width in VEC_FIELDS[:i]) for i, (name, _) in enumerate(VEC_FIELDS)}
VEC_WIDTH = dict(VEC_FIELDS)
ChunkBufs = collections.namedtuple("ChunkBufs", "xs a b gl us h")


def _rmsnorm(x, gain):
    var = jnp.mean(x * x, axis=-1, keepdims=True)
    return x * lax.rsqrt(var + EPS) * gain


def _dot(a, b):
    return jnp.dot(a, b, preferred_element_type=F32)


_GELU_K1 = -2.0 * math.sqrt(2.0 / math.pi) * math.log2(math.e)
_GELU_K2 = _GELU_K1 * 0.044715


def _gelu(x):
    return x / (1.0 + jnp.exp2(x * (_GELU_K1 + _GELU_K2 * (x * x))))


def _keep_if(pred, new, old):
    return new if pred is None else jnp.where(pred, new, old)


def _lanes(c):
    return slice(c * LANES, (c + 1) * LANES)


def _vec(w, name, lo=0, hi=None):
    hi = VEC_WIDTH[name] if hi is None else hi
    return w.vec[:, VEC_OFFSET[name] + lo:VEC_OFFSET[name] + hi]


def _prep_kernel(depth, lr_ref, li_ref, ldt_ref, br_ref, bi_ref, cr_ref, ci_ref, wa_ref, wx_ref,
                 lbr_ref, lbi_ref, bw_ref, cwt_ref, wgate_ref):
    lr = lr_ref[...]
    li = li_ref[...]
    dt = jnp.exp(ldt_ref[...])
    mag = jnp.exp(lr * dt)
    lbr = mag * jnp.cos(li * dt)
    lbi = mag * jnp.sin(li * dt)
    nr, ni = lbr - 1.0, lbi
    den = lr * lr + li * li
    gr = (nr * lr + ni * li) / den
    gi = (ni * lr - nr * li) / den
    br = br_ref[...]
    bi = bi_ref[...]
    lbr_ref[...] = lbr
    lbi_ref[...] = lbi
    bb = ((gr * br - gi * bi).astype(BF16), (gr * bi + gi * br).astype(BF16))
    cc = (cr_ref[...].astype(BF16), (-ci_ref[...]).astype(BF16))
    bw_ref[...] = jnp.zeros(bw_ref.shape, BF16)
    cwt_ref[...] = jnp.zeros(cwt_ref.shape, BF16)
    wgate_ref[...] = jnp.zeros(wgate_ref.shape, BF16)
    chunks = SSM_W // LANES
    for l in range(depth):
        for g in range(SSM_G):
            s, gg = divmod(g, 2)
            c, sl = divmod(s, S5_ROWS_PER_CHUNK)
            src = slice((l * SSM_G + g) * SSM_H, (l * SSM_G + g + 1) * SSM_H)
            lanes_u = slice((sl * 2 + gg) * SSM_H, (sl * 2 + gg + 1) * SSM_H)
            for reim in range(2):
                col = (sl * 2 + reim) * LANES + gg * SSM_P
                bw_ref[l * chunks + c, lanes_u, col:col + SSM_P] = bb[reim][src, :]
                cwt_ref[l * chunks + c, lanes_u, col:col + SSM_P] = cc[reim][src, :]
        per = LRU_HEADS // GATE_HALVES
        for head in range(LRU_HEADS):
            hf, hh = divmod(head, per)
            src = slice((l * LRU_HEADS + head) * LRU_HD, (l * LRU_HEADS + head + 1) * LRU_HD)
            blk = slice(hh * LRU_HD, (hh + 1) * LRU_HD)
            wgate_ref[l * GATE_HALVES + hf, blk, blk] = wa_ref[src, :].astype(BF16)
            wgate_ref[l * GATE_HALVES + hf, blk, MXU_DIM + hh * LRU_HD:MXU_DIM + (hh + 1) * LRU_HD] = (
                wx_ref[src, :].astype(BF16))


def _prep(lam_re, lam_im, log_dt, b_re, b_im, c_re, c_im, wa, wx):
    depth = lam_re.shape[0]
    rows = depth * SSM_G * SSM_H
    shape4 = (depth, SSM_G, SSM_H, SSM_P)

    def expand(a):
        return jnp.broadcast_to(a[:, :, None, :], shape4).reshape(rows, SSM_P)

    ldt = jnp.broadcast_to(log_dt[:, :, None, None], shape4).reshape(rows, SSM_P)
    br = jnp.swapaxes(b_re, 2, 3).reshape(rows, SSM_P)
    bi = jnp.swapaxes(b_im, 2, 3).reshape(rows, SSM_P)
    chunks = SSM_W // LANES
    sds = jax.ShapeDtypeStruct((rows, SSM_P), F32)
    out_shape = (sds, sds,
                 jax.ShapeDtypeStruct((depth * chunks, LANES, S5_ROWS_PER_CHUNK * 2 * LANES), BF16),
                 jax.ShapeDtypeStruct((depth * chunks, LANES, S5_ROWS_PER_CHUNK * 2 * LANES), BF16),
                 jax.ShapeDtypeStruct((depth * GATE_HALVES, MXU_DIM, 2 * MXU_DIM), BF16))
    lbr, lbi, bw, cwt, wgate = pl.pallas_call(
        functools.partial(_prep_kernel, depth), out_shape=out_shape, name="prep",
    )(expand(lam_re), expand(lam_im), ldt, br, bi, c_re.reshape(rows, SSM_P), c_im.reshape(rows, SSM_P),
      wa.reshape(depth * LRU_W, LRU_HD), wx.reshape(depth * LRU_W, LRU_HD))
    lbr = lbr.reshape(shape4)[:, :, 0, :].reshape(depth, 2, SUBLANES, LANES)
    lbi = lbi.reshape(shape4)[:, :, 0, :].reshape(depth, 2, SUBLANES, LANES)
    return (lbr, lbi, bw.reshape((depth, chunks) + bw.shape[1:]), cwt.reshape((depth, chunks) + cwt.shape[1:]),
            wgate.reshape((depth, GATE_HALVES) + wgate.shape[1:]))


def _s5_time_rows(srow, row0, count):
    return pl.ds(S5_ROW_PITCH * srow + S5_TIME_PITCH * row0, count, stride=S5_TIME_PITCH)


def _s5_step_rows(r):
    return pl.ds(S5_TIME_PITCH * r, SUBLANES, stride=S5_ROW_PITCH)


def _lru_time_rows(c, row0, count):
    return pl.ds(c + LRU_CHUNKS * row0, count, stride=LRU_CHUNKS)


def _blocks(rows, size):
    size = min(size, rows)
    return [(start, size) for start in range(0, rows, size)]


def _zip_tasks(a, b):
    keyed = [((i + 0.5) / len(a), 0, i, t) for i, t in enumerate(a)]
    keyed += [((i + 0.5) / len(b), 1, i, t) for i, t in enumerate(b)]
    return [t for _, _, _, t in sorted(keyed, key=lambda e: e[:3])]


def _norm_tasks(rows, x_ref, w, h_ref):
    def norm(start, size):
        h_ref[start:start + size, :] = _rmsnorm(x_ref[start:start + size, :], _vec(w, "gmix")).astype(BF16)
    return [functools.partial(norm, start, size) for start, size in _blocks(rows, VPU_TASK_ROWS)]


def _stage_a_tasks(nb, seq, w, convc_ref, xp_ref, bufs, live):
    rows = nb * seq
    vpu_blocks = _blocks(rows, VPU_TASK_ROWS)
    mxu_blocks = _blocks(rows, MXU_TASK_ROWS)
    st = {}
    groups = {"in_proj": [], "conv_gates": [], "s5_in": []}

    def in_proj(k, start, size):
        z = _dot(bufs.h[start:start + size, :], w.win[:, k * MXU_DIM:(k + 1) * MXU_DIM])
        st["z", k, start] = z
        if k in (2, 3):
            bufs.gl[start:start + size, (k - 2) * MXU_DIM:(k - 1) * MXU_DIM] = z
        if k in (4, 5):
            bufs.us[start:start + size, (k - 4) * MXU_DIM:(k - 3) * MXU_DIM] = z
    for k in range(IN_W // MXU_DIM):
        for start, size in mxu_blocks:
            groups["in_proj"].append(functools.partial(in_proj, k, start, size))

    def z_lanes(k0, c):
        zk, off = divmod(c * LANES, MXU_DIM)
        parts = [st["z", k0 + zk, s][:, off:off + LANES] for s, n in mxu_blocks]
        return parts[0] if len(parts) == 1 else jnp.concatenate(parts, axis=0)

    def conv(c):
        xl = z_lanes(0, c)
        cw = w.convw[:, _lanes(c)]
        parts = []
        for j in range(nb):
            xl_j = xl[j * seq:(j + 1) * seq, :]
            xp_ref[j, c, pl.ds(0, SUBLANES, stride=CONV_PITCH), :] = convc_ref[j, :, _lanes(c)]
            xp_ref[j, c, pl.ds(CONV_BASE, seq, stride=CONV_PITCH), :] = xl_j
            acc = _vec(w, "convb", c * LANES, (c + 1) * LANES) + xl_j * cw[CONV_W - 1:CONV_W, :]
            for k in range(CONV_W - 1):
                start = CONV_BASE - CONV_PITCH * (CONV_W - 1 - k)
                acc = acc + xp_ref[j, c, pl.ds(start, seq, stride=CONV_PITCH), :] * cw[k:k + 1, :]
            parts.append(acc)
            convc_ref[j, :, _lanes(c)] = _keep_if(live, xl_j[seq - SUBLANES:, :], convc_ref[j, :, _lanes(c)])
        st["xc", c] = parts[0] if nb == 1 else jnp.concatenate(parts, axis=0)
    for c in range(LRU_CHUNKS):
        groups["conv_gates"].append(functools.partial(conv, c))

    def gates_dot(hf):
        chunks = range(hf * MXU_DIM // LANES, (hf + 1) * MXU_DIM // LANES)
        xc = jnp.concatenate([st["xc", c] for c in chunks], axis=1)
        st["xc2", hf] = xc
        st["pre", hf] = _dot(xc.astype(BF16), w.wgate[hf])

    def gates_ew(hf, start, size):
        lo, hi = hf * MXU_DIM, (hf + 1) * MXU_DIM
        lam = _vec(w, "lam", lo, hi)
        neg_c_softplus = -LRU_C * (jnp.maximum(-lam, 0.0) + jnp.log1p(jnp.exp(-jnp.abs(lam))))
        pre = st["pre", hf][start:start + size, :]
        xc = st["xc2", hf][start:start + size, :]
        r = jax.nn.sigmoid(pre[:, :MXU_DIM] + _vec(w, "ba", lo, hi))
        ig = jax.nn.sigmoid(pre[:, MXU_DIM:] + _vec(w, "bx", lo, hi))
        log_a = neg_c_softplus * r
        a = jnp.exp(log_a)
        one_minus_a2 = -jnp.tanh(log_a) * (a * a + 1.0)
        b = jnp.sqrt(one_minus_a2) * (ig * xc)
        for i in range(MXU_DIM // LANES):
            c = hf * MXU_DIM // LANES + i
            bufs.a[_lru_time_rows(c, start, size), :] = a[:, _lanes(i)]
            bufs.b[_lru_time_rows(c, start, size), :] = b[:, _lanes(i)]
    for hf in range(GATE_HALVES):
        groups["conv_gates"].append(functools.partial(gates_dot, hf))
        for start, size in vpu_blocks:
            groups["conv_gates"].append(functools.partial(gates_ew, hf, start, size))

    def s5_in(c, start, size):
        zk, off = divmod(c * LANES, MXU_DIM)
        us_bf = st["z", 4 + zk, start][:, off:off + LANES].astype(BF16)
        bu = _dot(us_bf, w.bw[c])
        half = (c * S5_ROWS_PER_CHUNK) // SUBLANES
        for sl in range(S5_ROWS_PER_CHUNK):
            srow = (c * S5_ROWS_PER_CHUNK + sl) % SUBLANES
            for reim in range(2):
                col = (sl * 2 + reim) * LANES
                bufs.xs[reim * 2 + half, _s5_time_rows(srow, start, size), :] = bu[:, col:col + LANES]
    for c in range(SSM_W // LANES):
        for start, size in mxu_blocks:
            groups["s5_in"].append(functools.partial(s5_in, c, start, size))
    return groups


def _stage_l_tasks(nb, seq, w, lruc_ref, ssmc_ref, bufs, live):
    st = {}
    tasks = []
    row = lambda: lax.broadcasted_iota(jnp.int32, (SUBLANES, LANES), 0)

    def begin(j):
        st["lb"] = ((w.lbr[0], w.lbr[1]), (w.lbi[0], w.lbi[1]))
        st["xr"] = [ssmc_ref[j, 0], ssmc_ref[j, 1]]
        st["xi"] = [ssmc_ref[j, 2], ssmc_ref[j, 3]]
        st["h"] = lruc_ref[j]
        st["old"] = (tuple(st["xr"]), tuple(st["xi"]), st["h"])

    def steps(j, tile):
        lbr, lbi = st["lb"]
        xr, xi = st["xr"], st["xi"]
        for t in range(tile * SUBLANES, (tile + 1) * SUBLANES):
            rows_t = _s5_step_rows(j * seq + t)
            for hv in range(2):
                nr = lbr[hv] * xr[hv] - lbi[hv] * xi[hv] + bufs.xs[hv, rows_t, :]
                ni = lbr[hv] * xi[hv] + lbi[hv] * xr[hv] + bufs.xs[2 + hv, rows_t, :]
                bufs.xs[hv, rows_t, :] = nr
                bufs.xs[2 + hv, rows_t, :] = ni
                xr[hv], xi[hv] = nr, ni
        first_half = row() < LRU_CHUNKS
        h = st["h"]
        for pair in range(tile * SUBLANES // 2, (tile + 1) * SUBLANES // 2):
            r0 = (j * seq + 2 * pair) * LRU_CHUNKS
            a2 = bufs.a[r0:r0 + SUBLANES, :]
            b2 = bufs.b[r0:r0 + SUBLANES, :]
            h_even = a2 * h + b2
            h_odd = a2 * pltpu.roll(h_even, LRU_CHUNKS, 0) + b2
            bufs.b[r0:r0 + SUBLANES, :] = jnp.where(first_half, h_even, h_odd)
            h = pltpu.roll(h_odd, LRU_CHUNKS, 0)
        st["h"] = h

    def end(j):
        old = st["old"]
        for hv in range(2):
            ssmc_ref[j, hv] = _keep_if(live, st["xr"][hv], old[0][hv])
            ssmc_ref[j, 2 + hv] = _keep_if(live, st["xi"][hv], old[1][hv])
        lruc_ref[j] = _keep_if(live, st["h"], old[2])

    for j in range(nb):
        tasks.append(functools.partial(begin, j))
        for tile in range(seq // SUBLANES):
            tasks.append(functools.partial(steps, j, tile))
        tasks.append(functools.partial(end, j))
    return tasks


def _stage_c_tasks(rows, x_ref, y_ref, w, bufs):
    vpu_blocks = _blocks(rows, VPU_TASK_ROWS)
    mxu_blocks = _blocks(rows, MXU_TASK_ROWS)
    chunks = SSM_W // LANES
    st = {}
    groups = {"s5_out": [], "glu_dot": {}, "lru_out": {}, "glu_ew": {}, "out_proj": {}}

    def rows_of(key, start, size):
        parts = [st[key, s] for s, n in vpu_blocks if start <= s < start + size]
        return parts[0] if len(parts) == 1 else jnp.concatenate(parts, axis=0)

    def s5_out(c, start, size):
        pieces = []
        for sl in range(S5_ROWS_PER_CHUNK):
            s = c * S5_ROWS_PER_CHUNK + sl
            half, srow = s // SUBLANES, s % SUBLANES
            pieces.append(bufs.xs[half, _s5_time_rows(srow, start, size), :])
            pieces.append(bufs.xs[2 + half, _s5_time_rows(srow, start, size), :])
        xcat = jnp.concatenate(pieces, axis=1).astype(BF16)
        y = lax.dot_general(xcat, w.cwt[c], (((1,), (1,)), ((), ())), preferred_element_type=F32)
        st["g", c, start] = _gelu(y + _vec(w, "dskip", c * LANES, (c + 1) * LANES) * bufs.us[start:start + size, _lanes(c)])
    for start, size in mxu_blocks:
        for c in range(chunks):
            groups["s5_out"].append(functools.partial(s5_out, c, start, size))

    def lru_out(start, size):
        h = jnp.concatenate([bufs.b[_lru_time_rows(c, start, size), :] for c in range(LRU_CHUNKS)], axis=1)
        st["n_lru", start] = _rmsnorm(h * _gelu(bufs.gl[start:start + size, :]), _vec(w, "glru")).astype(BF16)

    def glu_dot(start, size):
        g = jnp.concatenate([st["g", c, start] for c in range(chunks)], axis=1)
        st["g4", start] = g
        st["gate", start] = _dot(g.astype(BF16), w.wglu[...]) + _vec(w, "bglu")

    def glu_ew(start, size):
        m0 = max(s for s, n in mxu_blocks if s <= start)
        g = st["g4", m0][start - m0:start - m0 + size, :]
        gate = st["gate", m0][start - m0:start - m0 + size, :]
        st["n_ssm", start] = _rmsnorm(g * jax.nn.sigmoid(gate), _vec(w, "gssm")).astype(BF16)

    def out_proj(k, start, size):
        cols = slice(k * MXU_DIM, (k + 1) * MXU_DIM)
        y_ref[start:start + size, cols] = (
            x_ref[start:start + size, cols] + _dot(rows_of("n_lru", start, size), w.wout[0:LRU_W, cols])
            + _dot(rows_of("n_ssm", start, size), w.wout[LRU_W:, cols]))
    for start, size in mxu_blocks:
        inside = [(s, n) for s, n in vpu_blocks if start <= s < start + size]
        groups["glu_dot"][start] = functools.partial(glu_dot, start, size)
        groups["lru_out"][start] = [functools.partial(lru_out, s, n) for s, n in inside]
        groups["glu_ew"][start] = [functools.partial(glu_ew, s, n) for s, n in inside]
        groups["out_proj"][start] = [functools.partial(out_proj, k, start, size) for k in range(D_MODEL // MXU_DIM)]
    return groups


def _stage_c_tail(c_groups, fillers):
    starts = sorted(c_groups["glu_dot"])
    order = _zip_tasks([c_groups["glu_dot"][m] for m in starts], c_groups["lru_out"][starts[0]])
    order += c_groups["glu_ew"][starts[0]]
    for m, nxt in zip(starts, starts[1:] + [None]):
        companions = fillers if nxt is None else c_groups["lru_out"][nxt] + c_groups["glu_ew"][nxt]
        order += _zip_tasks(c_groups["out_proj"][m], companions) if companions else c_groups["out_proj"][m]
    return order


N_WEIGHTS = len(MixerWeights._fields)
N_BUFS = len(ChunkBufs._fields)


def _mixer_plain_kernel(nb, seq, x_ref, conv0_ref, lru0_ref, ssm0_ref, *rest):
    w = MixerWeights(*rest[:N_WEIGHTS])
    y_ref, convc_ref, lruc_ref, ssmc_ref, xp_ref = rest[N_WEIGHTS:N_WEIGHTS + 5]
    bufs = ChunkBufs(*rest[N_WEIGHTS + 5:])
    rows = nb * seq
    convc_ref[...] = conv0_ref[...]
    lruc_ref[...] = lru0_ref[...]
    ssmc_ref[...] = ssm0_ref[...]
    a = _stage_a_tasks(nb, seq, w, convc_ref, xp_ref, bufs, None)
    c = _stage_c_tasks(rows, x_ref, y_ref, w, bufs)
    for task in (_norm_tasks(rows, x_ref, w, bufs.h) + a["in_proj"] + a["conv_gates"] + a["s5_in"]
                 + _stage_l_tasks(nb, seq, w, lruc_ref, ssmc_ref, bufs, None)
                 + c["s5_out"] + _stage_c_tail(c, [])):
        task()


def _mixer_pipelined_kernel(n_chunks, seq, x_next_ref, x_prev_ref, conv0_ref, lru0_ref, ssm0_ref, *rest):
    w = MixerWeights(*rest[:N_WEIGHTS])
    y_ref, convc_ref, lruc_ref, ssmc_ref, xp_ref = rest[N_WEIGHTS:N_WEIGHTS + 5]
    sets = (ChunkBufs(*rest[N_WEIGHTS + 5:N_WEIGHTS + 5 + N_BUFS]),
            ChunkBufs(*rest[N_WEIGHTS + 5 + N_BUFS:]))
    step = pl.program_id(0)

    @pl.when(step == 0)
    def _():
        convc_ref[...] = conv0_ref[...]
        lruc_ref[...] = lru0_ref[...]
        ssmc_ref[...] = ssm0_ref[...]
        for ref in sets[1]:
            ref[...] = jnp.zeros(ref.shape, ref.dtype)
        for task in _norm_tasks(seq, x_prev_ref, w, sets[0].h):
            task()

    def body(write, read):
        a = _stage_a_tasks(1, seq, w, convc_ref, xp_ref, write, step < n_chunks)
        l = _stage_l_tasks(1, seq, w, lruc_ref, ssmc_ref, read, step >= 1)
        c = _stage_c_tasks(seq, x_prev_ref, y_ref, w, read)
        norm_next = _norm_tasks(seq, x_next_ref, w, read.h)
        for task in (_zip_tasks(l, a["in_proj"])
                     + _zip_tasks(c["s5_out"], _zip_tasks(a["conv_gates"], a["s5_in"]))
                     + _stage_c_tail(c, norm_next)):
            task()

    @pl.when(step % 2 == 0)
    def _():
        body(sets[0], sets[1])

    @pl.when(step % 2 == 1)
    def _():
        body(sets[1], sets[0])


def _full_spec(shape):
    zeros = (0,) * len(shape)
    return pl.BlockSpec(shape, lambda i, _z=zeros: _z)


def _layer_spec(shape, layer):
    index = (layer,) + (0,) * (len(shape) - 1)
    return pl.BlockSpec((None,) + tuple(shape[1:]), lambda i, _x=index: _x, pipeline_mode=pl.Buffered(1))


def _chunk_scratch(rows):
    xs_rows = -(-(S5_TIME_PITCH * rows + S5_ROW_PITCH * SUBLANES) // SUBLANES) * SUBLANES
    return [pltpu.VMEM((SSM_SLABS, xs_rows, LANES), F32),
            pltpu.VMEM((rows * LRU_CHUNKS, LANES), F32), pltpu.VMEM((rows * LRU_CHUNKS, LANES), F32),
            pltpu.VMEM((rows, LRU_W), F32), pltpu.VMEM((rows, SSM_W), F32),
            pltpu.VMEM((rows, D_MODEL), BF16)]


def _mixer_call(x2d, states, state_layer, w, layer, nb, seq, pipelined):
    total = x2d.shape[0]
    rows = nb * seq
    conv0, lru0, ssm0 = states
    state_in_specs = [_layer_spec(s.shape, state_layer) for s in states]
    state_specs = [_full_spec(s.shape[1:]) for s in states]
    weight_specs = [_layer_spec(a.shape, layer) for a in w]
    out_shape = (jax.ShapeDtypeStruct((total, D_MODEL), F32),) + tuple(
        jax.ShapeDtypeStruct(s.shape[1:], F32) for s in states)
    conv_scratch = pltpu.VMEM((nb, LRU_CHUNKS, CONV_PITCH * seq + CONV_BASE, LANES), F32)
    scratch = [conv_scratch] + _chunk_scratch(rows)
    params = pltpu.CompilerParams(dimension_semantics=("arbitrary",), vmem_limit_bytes=VMEM_LIMIT_BYTES)
    if not pipelined:
        assert total == rows
        x_spec = pl.BlockSpec((rows, D_MODEL), lambda i: (0, 0))
        return pl.pallas_call(
            functools.partial(_mixer_plain_kernel, nb, seq),
            grid=(1,), in_specs=[x_spec] + state_in_specs + weight_specs,
            out_specs=(x_spec,) + tuple(state_specs), out_shape=out_shape,
            scratch_shapes=scratch, name="mixer_plain", compiler_params=params,
        )(x2d, conv0, lru0, ssm0, *w)
    assert nb == 1 and total % rows == 0
    n_chunks = total // rows
    next_spec = pl.BlockSpec((rows, D_MODEL), lambda i: (jnp.minimum(i + 1, n_chunks - 1), 0))
    prev_spec = pl.BlockSpec((rows, D_MODEL), lambda i: (jnp.maximum(i - 1, 0), 0))
    return pl.pallas_call(
        functools.partial(_mixer_pipelined_kernel, n_chunks, seq),
        grid=(n_chunks + 1,), in_specs=[next_spec, prev_spec] + state_in_specs + weight_specs,
        out_specs=(prev_spec,) + tuple(state_specs), out_shape=out_shape,
        scratch_shapes=scratch + _chunk_scratch(rows), name="mixer", compiler_params=params,
    )(x2d, x2d, conv0, lru0, ssm0, *w)


def _ffn_rows(final_norm, x, gffn_ref, wg_ref, wu_ref, wd_ref, gfin_ref):
    h = _rmsnorm(x, gffn_ref[...]).astype(BF16)
    acc = x
    for c in range(D_FF // FF_CHUNK):
        lo, hi = c * FF_CHUNK, (c + 1) * FF_CHUNK
        gate = _dot(h, wg_ref[:, lo:hi])
        up = _dot(h, wu_ref[:, lo:hi])
        act = (gate * jax.nn.sigmoid(gate) * up).astype(BF16)
        acc = acc + _dot(act, wd_ref[lo:hi, :])
    return _rmsnorm(acc, gfin_ref[...]) if final_norm else acc


def _stage_cast(src_hbm, dst_ref, stage_ref, sem_ref):
    rows, cols = dst_ref.shape
    chunks = _blocks(rows, WEIGHT_STAGE_ROWS)

    def copy(i):
        start, size = chunks[i]
        return pltpu.make_async_copy(src_hbm.at[pl.ds(start, size), :],
                                     stage_ref.at[i % 2, pl.ds(0, size), pl.ds(0, cols)], sem_ref.at[i % 2])

    copy(0).start()
    for i, (start, size) in enumerate(chunks):
        if i + 1 < len(chunks):
            copy(i + 1).start()
        copy(i).wait()
        dst_ref[start:start + size, :] = stage_ref[i % 2, 0:size, 0:cols].astype(BF16)


def _ffn_kernel(final_norm, main_steps, layer, xa_ref, xb_ref, gffn_ref, wg_hbm, wu_hbm, wd_hbm, gfin_ref,
                oa_ref, ob_ref, wg_ref, wu_ref, wd_ref, stage_ref, sem_ref):
    weights = (gffn_ref, wg_ref, wu_ref, wd_ref, gfin_ref)
    step = pl.program_id(0)

    @pl.when(step == 0)
    def _():
        for src, dst in ((wg_hbm, wg_ref), (wu_hbm, wu_ref), (wd_hbm, wd_ref)):
            _stage_cast(src.at[layer], dst, stage_ref, sem_ref)

    @pl.when(step < main_steps)
    def _():
        oa_ref[...] = _ffn_rows(final_norm, xa_ref[...], *weights)

    @pl.when(step == main_steps)
    def _():
        ob_ref[...] = _ffn_rows(final_norm, xb_ref[...], *weights)


def _ffn_call(xa, xb, w, layer, gfin, final_norm, tile):
    main_steps = xa.shape[0] // tile
    gffn, wg, wu, wd = w
    tile_spec = pl.BlockSpec((tile, D_MODEL), lambda i: (jnp.minimum(i, main_steps - 1), 0))
    hbm_spec = pl.BlockSpec(memory_space=pl.ANY)
    in_specs = [tile_spec, _full_spec(xb.shape), _layer_spec(gffn.shape, layer), hbm_spec, hbm_spec, hbm_spec,
                _full_spec(gfin.shape)]
    scratch = [pltpu.VMEM(wg.shape[1:], BF16), pltpu.VMEM(wu.shape[1:], BF16), pltpu.VMEM(wd.shape[1:], BF16),
               pltpu.VMEM((2, WEIGHT_STAGE_ROWS, D_FF), F32), pltpu.SemaphoreType.DMA((2,))]
    return pl.pallas_call(
        functools.partial(_ffn_kernel, final_norm, main_steps, layer),
        grid=(main_steps + 1,), in_specs=in_specs,
        out_specs=(tile_spec, _full_spec(xb.shape)),
        out_shape=(jax.ShapeDtypeStruct(xa.shape, F32), jax.ShapeDtypeStruct(xb.shape, F32)),
        scratch_shapes=scratch, name="ffn",
        compiler_params=pltpu.CompilerParams(dimension_semantics=("arbitrary",),
                                             vmem_limit_bytes=VMEM_LIMIT_BYTES),
    )(xa, xb, gffn, wg, wu, wd, gfin)


def _ssm_to_slabs(re, im):
    lead = re.shape[:-2]
    return jnp.concatenate([re.reshape(lead + (2, SUBLANES, LANES)), im.reshape(lead + (2, SUBLANES, LANES))],
                           axis=len(lead))


def _slabs_to_ssm(slabs):
    lead = slabs.shape[:-3]
    return (slabs[..., 0:2, :, :].reshape(lead + (SSM_G, SSM_P)), slabs[..., 2:4, :, :].reshape(lead + (SSM_G, SSM_P)))


def _conv_to_tile(conv):
    pad = [(0, 0)] * (conv.ndim - 2) + [(SUBLANES - (CONV_W - 1), 0), (0, 0)]
    return jnp.pad(conv, pad)


def _lru_to_tile(h):
    lead = h.shape[:-1]
    pad = [(0, 0)] * len(lead) + [(0, SUBLANES - LRU_CHUNKS), (0, 0)]
    return jnp.pad(h.reshape(lead + (LRU_CHUNKS, LANES)), pad)


def _tile_to_lru(tile):
    return tile[..., :LRU_CHUNKS, :].reshape(tile.shape[:-2] + (LRU_W,))


def kernel(x_prompt, x_sample, state_conv, state_lru, state_ssm_re, state_ssm_im, norm_mix, w_in, conv_w, conv_b, lru_wa, lru_ba, lru_wx, lru_bx, lru_lambda, ssm_lambda_re, ssm_lambda_im, ssm_b_re, ssm_b_im, ssm_c_re, ssm_c_im, ssm_d, ssm_log_dt, ssm_w_glu, ssm_b_glu, norm_lru_out, norm_ssm_out, w_out, norm_ffn, w_gate, w_up, w_down, norm_final):
    depth = w_in.shape[0]
    bp, tp, _ = x_prompt.shape
    bs, ts, _ = x_sample.shape
    assert bp == 1

    lbr, lbi, bw, cwt, wgate = _prep(ssm_lambda_re, ssm_lambda_im, ssm_log_dt, ssm_b_re, ssm_b_im,
                                     ssm_c_re, ssm_c_im, lru_wa, lru_wx)

    def rows(v):
        return v.reshape(depth, 1, -1)

    by_name = dict(gmix=norm_mix, convb=conv_b, lam=lru_lambda, ba=lru_ba, bx=lru_bx, dskip=ssm_d,
                   bglu=ssm_b_glu, glru=norm_lru_out, gssm=norm_ssm_out)
    mixer_w = MixerWeights(
        vec=rows(jnp.concatenate([by_name[name] for name, _ in VEC_FIELDS], axis=1)),
        win=w_in.astype(BF16), convw=conv_w, wgate=wgate, lbr=lbr, lbi=lbi, bw=bw, cwt=cwt,
        wglu=ssm_w_glu.astype(BF16), wout=w_out.astype(BF16))
    ffn_w = (rows(norm_ffn), w_gate, w_up, w_down)
    gfin = norm_final.reshape(1, -1)

    yp = x_prompt.reshape(bp * tp, D_MODEL)
    ys = x_sample.reshape(bs * ts, D_MODEL)
    prompt_states = (jnp.zeros((1, bp, SUBLANES, LRU_W), F32), jnp.zeros((1, bp, SUBLANES, LANES), F32),
                     jnp.zeros((1, bp, SSM_SLABS, SUBLANES, LANES), F32))
    sample_states = (_conv_to_tile(state_conv), _lru_to_tile(state_lru), _ssm_to_slabs(state_ssm_re, state_ssm_im))

    prompt_out, sample_out = [], []
    for l in range(depth):
        last = l == depth - 1
        yp, *st_p = _mixer_call(yp, prompt_states, 0, mixer_w, l, 1, PROMPT_CHUNK, True)
        ys, *st_s = _mixer_call(ys, sample_states, l, mixer_w, l, bs, ts, False)
        yp, ys = _ffn_call(yp, ys, ffn_w, l, gfin, last, FFN_TILE)
        prompt_out.append(st_p)
        sample_out.append(st_s)

    def unpack(per_layer):
        conv, lru, ssm = (jnp.stack(v) for v in zip(*per_layer))
        re, im = _slabs_to_ssm(ssm)
        return conv[:, :, SUBLANES - (CONV_W - 1):], _tile_to_lru(lru), re, im

    return (yp.reshape(bp, tp, D_MODEL), ys.reshape(bs, ts, D_MODEL)) + unpack(prompt_out) + unpack(sample_out)
```

```python
import collections
import functools
import math

import jax
import jax.numpy as jnp
from jax import lax
from jax.experimental import pallas as pl
from jax.experimental.pallas import tpu as pltpu

D_MODEL = 1024
LRU_W = 512
LRU_HEADS = 8
LRU_HD = LRU_W // LRU_HEADS
CONV_W = 4
LRU_C = 8.0
SSM_W = 512
SSM_H = 16
SSM_G = SSM_W // SSM_H
SSM_P = 64
IN_W = 2 * LRU_W + SSM_W
D_FF = 2816
EPS = 1e-6

SUBLANES = 8
LANES = 128
MXU_DIM = 256
VMEM_LIMIT_BYTES = 56 * 1024 * 1024

SSM_SLABS = 4
S5_ROWS_PER_CHUNK = LANES // (2 * SSM_H)
S5_TIME_PITCH = 9
S5_ROW_PITCH = 2
LRU_CHUNKS = LRU_W // LANES
CONV_PITCH = 2
CONV_BASE = CONV_PITCH * SUBLANES
GATE_HALVES = LRU_W // MXU_DIM
FF_CHUNK = MXU_DIM
PROMPT_CHUNK = 512
VPU_TASK_ROWS = 512
MXU_TASK_ROWS = 512
FFN_TILE = 1024
BF16 = jnp.bfloat16
F32 = jnp.float32

MixerWeights = collections.namedtuple(
    "MixerWeights",
    "vec win convw wgate lbr lbi bw cwt wglu wout")
VEC_FIELDS = (("gmix", D_MODEL), ("convb", LRU_W), ("lam", LRU_W), ("ba", LRU_W), ("bx", LRU_W),
              ("dskip", SSM_W), ("bglu", SSM_W), ("glru", LRU_W), ("gssm", SSM_W))
VEC_OFFSET = {name: sum(width for _, width in VEC_FIELDS[:i]) for i, (name, _) in enumerate(VEC_FIELDS)}
VEC_WIDTH = dict(VEC_FIELDS)
ChunkBufs = collections.namedtuple("ChunkBufs", "xs a b gl us h")


def _rmsnorm(x, gain):
    var = jnp.mean(x * x, axis=-1, keepdims=True)
    return x * lax.rsqrt(var + EPS) * gain


def _dot(a, b):
    return jnp.dot(a, b, preferred_element_type=F32)


_GELU_K1 = -2.0 * math.sqrt(2.0 / math.pi) * math.log2(math.e)
_GELU_K2 = _GELU_K1 * 0.044715


def _gelu(x):
    return x / (1.0 + jnp.exp2(x * (_GELU_K1 + _GELU_K2 * (x * x))))


def _keep_if(pred, new, old):
    return new if pred is None else jnp.where(pred, new, old)


def _lanes(c):
    return slice(c * LANES, (c + 1) * LANES)


def _vec(w, name, lo=0, hi=None):
    hi = VEC_WIDTH[name] if hi is None else hi
    return w.vec[:, VEC_OFFSET[name] + lo:VEC_OFFSET[name] + hi]


def _prep_kernel(depth, lr_ref, li_ref, ldt_ref, br_ref, bi_ref, cr_ref, ci_ref, wa_ref, wx_ref,
                 lbr_ref, lbi_ref, bw_ref, cwt_ref, wgate_ref):
    lr = lr_ref[...]
    li = li_ref[...]
    dt = jnp.exp(ldt_ref[...])
    mag = jnp.exp(lr * dt)
    lbr = mag * jnp.cos(li * dt)
    lbi = mag * jnp.sin(li * dt)
    nr, ni = lbr - 1.0, lbi
    den = lr * lr + li * li
    gr = (nr * lr + ni * li) / den
    gi = (ni * lr - nr * li) / den
    br = br_ref[...]
    bi = bi_ref[...]
    lbr_ref[...] = lbr
    lbi_ref[...] = lbi
    bb = ((gr * br - gi * bi).astype(BF16), (gr * bi + gi * br).astype(BF16))
    cc = (cr_ref[...].astype(BF16), (-ci_ref[...]).astype(BF16))
    bw_ref[...] = jnp.zeros(bw_ref.shape, BF16)
    cwt_ref[...] = jnp.zeros(cwt_ref.shape, BF16)
    wgate_ref[...] = jnp.zeros(wgate_ref.shape, BF16)
    chunks = SSM_W // LANES
    for l in range(depth):
        for g in range(SSM_G):
            s, gg = divmod(g, 2)
            c, sl = divmod(s, S5_ROWS_PER_CHUNK)
            src = slice((l * SSM_G + g) * SSM_H, (l * SSM_G + g + 1) * SSM_H)
            lanes_u = slice((sl * 2 + gg) * SSM_H, (sl * 2 + gg + 1) * SSM_H)
            for reim in range(2):
                col = (sl * 2 + reim) * LANES + gg * SSM_P
                bw_ref[l * chunks + c, lanes_u, col:col + SSM_P] = bb[reim][src, :]
                cwt_ref[l * chunks + c, lanes_u, col:col + SSM_P] = cc[reim][src, :]
        per = LRU_HEADS // GATE_HALVES
        for head in range(LRU_HEADS):
            hf, hh = divmod(head, per)
            src = slice((l * LRU_HEADS + head) * LRU_HD, (l * LRU_HEADS + head + 1) * LRU_HD)
            blk = slice(hh * LRU_HD, (hh + 1) * LRU_HD)
            wgate_ref[l * GATE_HALVES + hf, blk, blk] = wa_ref[src, :].astype(BF16)
            wgate_ref[l * GATE_HALVES + hf, blk, MXU_DIM + hh * LRU_HD:MXU_DIM + (hh + 1) * LRU_HD] = (
                wx_ref[src, :].astype(BF16))


def _prep(lam_re, lam_im, log_dt, b_re, b_im, c_re, c_im, wa, wx):
    depth = lam_re.shape[0]
    rows = depth * SSM_G * SSM_H
    shape4 = (depth, SSM_G, SSM_H, SSM_P)

    def expand(a):
        return jnp.broadcast_to(a[:, :, None, :], shape4).reshape(rows, SSM_P)

    ldt = jnp.broadcast_to(log_dt[:, :, None, None], shape4).reshape(rows, SSM_P)
    br = jnp.swapaxes(b_re, 2, 3).reshape(rows, SSM_P)
    bi = jnp.swapaxes(b_im, 2, 3).reshape(rows, SSM_P)
    chunks = SSM_W // LANES
    sds = jax.ShapeDtypeStruct((rows, SSM_P), F32)
    out_shape = (sds, sds,
                 jax.ShapeDtypeStruct((depth * chunks, LANES, S5_ROWS_PER_CHUNK * 2 * LANES), BF16),
                 jax.ShapeDtypeStruct((depth * chunks, LANES, S5_ROWS_PER_CHUNK * 2 * LANES), BF16),
                 jax.ShapeDtypeStruct((depth * GATE_HALVES, MXU_DIM, 2 * MXU_DIM), BF16))
    lbr, lbi, bw, cwt, wgate = pl.pallas_call(
        functools.partial(_prep_kernel, depth), out_shape=out_shape, name="prep",
    )(expand(lam_re), expand(lam_im), ldt, br, bi, c_re.reshape(rows, SSM_P), c_im.reshape(rows, SSM_P),
      wa.reshape(depth * LRU_W, LRU_HD), wx.reshape(depth * LRU_W, LRU_HD))
    lbr = lbr.reshape(shape4)[:, :, 0, :].reshape(depth, 2, SUBLANES, LANES)
    lbi = lbi.reshape(shape4)[:, :, 0, :].reshape(depth, 2, SUBLANES, LANES)
    return (lbr, lbi, bw.reshape((depth, chunks) + bw.shape[1:]), cwt.reshape((depth, chunks) + cwt.shape[1:]),
            wgate.reshape((depth, GATE_HALVES) + wgate.shape[1:]))


def _s5_time_rows(srow, row0, count):
    return pl.ds(S5_ROW_PITCH * srow + S5_TIME_PITCH * row0, count, stride=S5_TIME_PITCH)


def _s5_step_rows(r):
    return pl.ds(S5_TIME_PITCH * r, SUBLANES, stride=S5_ROW_PITCH)


def _lru_time_rows(c, row0, count):
    return pl.ds(c + LRU_CHUNKS * row0, count, stride=LRU_CHUNKS)


def _blocks(rows, size):
    size = min(size, rows)
    return [(start, size) for start in range(0, rows, size)]


def _zip_tasks(a, b):
    keyed = [((i + 0.5) / len(a), 0, i, t) for i, t in enumerate(a)]
    keyed += [((i + 0.5) / len(b), 1, i, t) for i, t in enumerate(b)]
    return [t for _, _, _, t in sorted(keyed, key=lambda e: e[:3])]


def _norm_tasks(rows, x_ref, w, h_ref):
    def norm(start, size):
        h_ref[start:start + size, :] = _rmsnorm(x_ref[start:start + size, :], _vec(w, "gmix")).astype(BF16)
    return [functools.partial(norm, start, size) for start, size in _blocks(rows, VPU_TASK_ROWS)]


def _stage_a_tasks(nb, seq, w, convc_ref, xp_ref, bufs, live):
    rows = nb * seq
    vpu_blocks = _blocks(rows, VPU_TASK_ROWS)
    mxu_blocks = _blocks(rows, MXU_TASK_ROWS)
    st = {}
    groups = {"in_proj": [], "conv_gates": [], "s5_in": []}

    def in_proj(k, start, size):
        z = _dot(bufs.h[start:start + size, :], w.win[:, k * MXU_DIM:(k + 1) * MXU_DIM])
        st["z", k, start] = z
        if k in (2, 3):
            bufs.gl[start:start + size, (k - 2) * MXU_DIM:(k - 1) * MXU_DIM] = z
        if k in (4, 5):
            bufs.us[start:start + size, (k - 4) * MXU_DIM:(k - 3) * MXU_DIM] = z
    for k in range(IN_W // MXU_DIM):
        for start, size in mxu_blocks:
            groups["in_proj"].append(functools.partial(in_proj, k, start, size))

    def z_lanes(k0, c):
        zk, off = divmod(c * LANES, MXU_DIM)
        parts = [st["z", k0 + zk, s][:, off:off + LANES] for s, n in mxu_blocks]
        return parts[0] if len(parts) == 1 else jnp.concatenate(parts, axis=0)

    def conv(c):
        xl = z_lanes(0, c)
        cw = w.convw[:, _lanes(c)]
        parts = []
        for j in range(nb):
            xl_j = xl[j * seq:(j + 1) * seq, :]
            xp_ref[j, c, pl.ds(0, SUBLANES, stride=CONV_PITCH), :] = convc_ref[j, :, _lanes(c)]
            xp_ref[j, c, pl.ds(CONV_BASE, seq, stride=CONV_PITCH), :] = xl_j
            acc = _vec(w, "convb", c * LANES, (c + 1) * LANES) + xl_j * cw[CONV_W - 1:CONV_W, :]
            for k in range(CONV_W - 1):
                start = CONV_BASE - CONV_PITCH * (CONV_W - 1 - k)
                acc = acc + xp_ref[j, c, pl.ds(start, seq, stride=CONV_PITCH), :] * cw[k:k + 1, :]
            parts.append(acc)
            convc_ref[j, :, _lanes(c)] = _keep_if(live, xl_j[seq - SUBLANES:, :], convc_ref[j, :, _lanes(c)])
        st["xc", c] = parts[0] if nb == 1 else jnp.concatenate(parts, axis=0)
    for c in range(LRU_CHUNKS):
        groups["conv_gates"].append(functools.partial(conv, c))

    def gates_dot(hf):
        chunks = range(hf * MXU_DIM // LANES, (hf + 1) * MXU_DIM // LANES)
        xc = jnp.concatenate([st["xc", c] for c in chunks], axis=1)
        st["xc2", hf] = xc
        st["pre", hf] = _dot(xc.astype(BF16), w.wgate[hf])

    def gates_ew(hf, start, size):
        lo, hi = hf * MXU_DIM, (hf + 1) * MXU_DIM
        lam = _vec(w, "lam", lo, hi)
        neg_c_softplus = -LRU_C * (jnp.maximum(-lam, 0.0) + jnp.log1p(jnp.exp(-jnp.abs(lam))))
        pre = st["pre", hf][start:start + size, :]
        xc = st["xc2", hf][start:start + size, :]
        r = jax.nn.sigmoid(pre[:, :MXU_DIM] + _vec(w, "ba", lo, hi))
        ig = jax.nn.sigmoid(pre[:, MXU_DIM:] + _vec(w, "bx", lo, hi))
        log_a = neg_c_softplus * r
        a = jnp.exp(log_a)
        one_minus_a2 = -jnp.tanh(log_a) * (a * a + 1.0)
        b = jnp.sqrt(one_minus_a2) * (ig * xc)
        for i in range(MXU_DIM // LANES):
            c = hf * MXU_DIM // LANES + i
            bufs.a[_lru_time_rows(c, start, size), :] = a[:, _lanes(i)]
            bufs.b[_lru_time_rows(c, start, size), :] = b[:, _lanes(i)]
    for hf in range(GATE_HALVES):
        groups["conv_gates"].append(functools.partial(gates_dot, hf))
        for start, size in vpu_blocks:
            groups["conv_gates"].append(functools.partial(gates_ew, hf, start, size))

    def s5_in(c, start, size):
        zk, off = divmod(c * LANES, MXU_DIM)
        us_bf = st["z", 4 + zk, start][:, off:off + LANES].astype(BF16)
        bu = _dot(us_bf, w.bw[c])
        half = (c * S5_ROWS_PER_CHUNK) // SUBLANES
        for sl in range(S5_ROWS_PER_CHUNK):
            srow = (c * S5_ROWS_PER_CHUNK + sl) % SUBLANES
            for reim in range(2):
                col = (sl * 2 + reim) * LANES
                bufs.xs[reim * 2 + half, _s5_time_rows(srow, start, size), :] = bu[:, col:col + LANES]
    for c in range(SSM_W // LANES):
        for start, size in mxu_blocks:
            groups["s5_in"].append(functools.partial(s5_in, c, start, size))
    return groups


def _stage_l_tasks(nb, seq, w, lruc_ref, ssmc_ref, bufs, live):
    st = {}
    tasks = []
    row = lambda: lax.broadcasted_iota(jnp.int32, (SUBLANES, LANES), 0)

    def begin(j):
        st["lb"] = ((w.lbr[0], w.lbr[1]), (w.lbi[0], w.lbi[1]))
        st["xr"] = [ssmc_ref[j, 0], ssmc_ref[j, 1]]
        st["xi"] = [ssmc_ref[j, 2], ssmc_ref[j, 3]]
        st["h"] = lruc_ref[j]
        st["old"] = (tuple(st["xr"]), tuple(st["xi"]), st["h"])

    def steps(j, tile):
        lbr, lbi = st["lb"]
        xr, xi = st["xr"], st["xi"]
        for t in range(tile * SUBLANES, (tile + 1) * SUBLANES):
            rows_t = _s5_step_rows(j * seq + t)
            for hv in range(2):
                nr = lbr[hv] * xr[hv] - lbi[hv] * xi[hv] + bufs.xs[hv, rows_t, :]
                ni = lbr[hv] * xi[hv] + lbi[hv] * xr[hv] + bufs.xs[2 + hv, rows_t, :]
                bufs.xs[hv, rows_t, :] = nr
                bufs.xs[2 + hv, rows_t, :] = ni
                xr[hv], xi[hv] = nr, ni
        first_half = row() < LRU_CHUNKS
        h = st["h"]
        for pair in range(tile * SUBLANES // 2, (tile + 1) * SUBLANES // 2):
            r0 = (j * seq + 2 * pair) * LRU_CHUNKS
            a2 = bufs.a[r0:r0 + SUBLANES, :]
            b2 = bufs.b[r0:r0 + SUBLANES, :]
            h_even = a2 * h + b2
            h_odd = a2 * pltpu.roll(h_even, LRU_CHUNKS, 0) + b2
            bufs.b[r0:r0 + SUBLANES, :] = jnp.where(first_half, h_even, h_odd)
            h = pltpu.roll(h_odd, LRU_CHUNKS, 0)
        st["h"] = h

    def end(j):
        old = st["old"]
        for hv in range(2):
            ssmc_ref[j, hv] = _keep_if(live, st["xr"][hv], old[0][hv])
            ssmc_ref[j, 2 + hv] = _keep_if(live, st["xi"][hv], old[1][hv])
        lruc_ref[j] = _keep_if(live, st["h"], old[2])

    for j in range(nb):
        tasks.append(functools.partial(begin, j))
        for tile in range(seq // SUBLANES):
            tasks.append(functools.partial(steps, j, tile))
        tasks.append(functools.partial(end, j))
    return tasks


def _stage_c_tasks(rows, x_ref, y_ref, w, bufs):
    vpu_blocks = _blocks(rows, VPU_TASK_ROWS)
    mxu_blocks = _blocks(rows, MXU_TASK_ROWS)
    chunks = SSM_W // LANES
    st = {}
    groups = {"s5_out": [], "glu_dot": {}, "lru_out": {}, "glu_ew": {}, "out_proj": {}}

    def rows_of(key, start, size):
        parts = [st[key, s] for s, n in vpu_blocks if start <= s < start + size]
        return parts[0] if len(parts) == 1 else jnp.concatenate(parts, axis=0)

    def s5_out(c, start, size):
        pieces = []
        for sl in range(S5_ROWS_PER_CHUNK):
            s = c * S5_ROWS_PER_CHUNK + sl
            half, srow = s // SUBLANES, s % SUBLANES
            pieces.append(bufs.xs[half, _s5_time_rows(srow, start, size), :])
            pieces.append(bufs.xs[2 + half, _s5_time_rows(srow, start, size), :])
        xcat = jnp.concatenate(pieces, axis=1).astype(BF16)
        y = lax.dot_general(xcat, w.cwt[c], (((1,), (1,)), ((), ())), preferred_element_type=F32)
        st["g", c, start] = _gelu(y + _vec(w, "dskip", c * LANES, (c + 1) * LANES) * bufs.us[start:start + size, _lanes(c)])
    for start, size in mxu_blocks:
        for c in range(chunks):
            groups["s5_out"].append(functools.partial(s5_out, c, start, size))

    def lru_out(start, size):
        h = jnp.concatenate([bufs.b[_lru_time_rows(c, start, size), :] for c in range(LRU_CHUNKS)], axis=1)
        st["n_lru", start] = _rmsnorm(h * _gelu(bufs.gl[start:start + size, :]), _vec(w, "glru")).astype(BF16)

    def glu_dot(start, size):
        g = jnp.concatenate([st["g", c, start] for c in range(chunks)], axis=1)
        st["g4", start] = g
        st["gate", start] = _dot(g.astype(BF16), w.wglu[...]) + _vec(w, "bglu")

    def glu_ew(start, size):
        m0 = max(s for s, n in mxu_blocks if s <= start)
        g = st["g4", m0][start - m0:start - m0 + size, :]
        gate = st["gate", m0][start - m0:start - m0 + size, :]
        st["n_ssm", start] = _rmsnorm(g * jax.nn.sigmoid(gate), _vec(w, "gssm")).astype(BF16)

    def out_proj(k, start, size):
        cols = slice(k * MXU_DIM, (k + 1) * MXU_DIM)
        y_ref[start:start + size, cols] = (
            x_ref[start:start + size, cols] + _dot(rows_of("n_lru", start, size), w.wout[0:LRU_W, cols])
            + _dot(rows_of("n_ssm", start, size), w.wout[LRU_W:, cols]))
    for start, size in mxu_blocks:
        inside = [(s, n) for s, n in vpu_blocks if start <= s < start + size]
        groups["glu_dot"][start] = functools.partial(glu_dot, start, size)
        groups["lru_out"][start] = [functools.partial(lru_out, s, n) for s, n in inside]
        groups["glu_ew"][start] = [functools.partial(glu_ew, s, n) for s, n in inside]
        groups["out_proj"][start] = [functools.partial(out_proj, k, start, size) for k in range(D_MODEL // MXU_DIM)]
    return groups


def _stage_c_tail(c_groups, fillers):
    starts = sorted(c_groups["glu_dot"])
    order = _zip_tasks([c_groups["glu_dot"][m] for m in starts], c_groups["lru_out"][starts[0]])
    order += c_groups["glu_ew"][starts[0]]
    for m, nxt in zip(starts, starts[1:] + [None]):
        companions = fillers if nxt is None else c_groups["lru_out"][nxt] + c_groups["glu_ew"][nxt]
        order += _zip_tasks(c_groups["out_proj"][m], companions) if companions else c_groups["out_proj"][m]
    return order


N_WEIGHTS = len(MixerWeights._fields)
N_BUFS = len(ChunkBufs._fields)


def _mixer_plain_kernel(nb, seq, x_ref, conv0_ref, lru0_ref, ssm0_ref, *rest):
    w = MixerWeights(*rest[:N_WEIGHTS])
    y_ref, convc_ref, lruc_ref, ssmc_ref, xp_ref = rest[N_WEIGHTS:N_WEIGHTS + 5]
    bufs = ChunkBufs(*rest[N_WEIGHTS + 5:])
    rows = nb * seq
    convc_ref[...] = conv0_ref[...]
    lruc_ref[...] = lru0_ref[...]
    ssmc_ref[...] = ssm0_ref[...]
    a = _stage_a_tasks(nb, seq, w, convc_ref, xp_ref, bufs, None)
    c = _stage_c_tasks(rows, x_ref, y_ref, w, bufs)
    for task in (_norm_tasks(rows, x_ref, w, bufs.h) + a["in_proj"] + a["conv_gates"] + a["s5_in"]
                 + _stage_l_tasks(nb, seq, w, lruc_ref, ssmc_ref, bufs, None)
                 + c["s5_out"] + _stage_c_tail(c, [])):
        task()


def _mixer_pipelined_kernel(n_chunks, seq, x_next_ref, x_prev_ref, conv0_ref, lru0_ref, ssm0_ref, *rest):
    w = MixerWeights(*rest[:N_WEIGHTS])
    y_ref, convc_ref, lruc_ref, ssmc_ref, xp_ref = rest[N_WEIGHTS:N_WEIGHTS + 5]
    sets = (ChunkBufs(*rest[N_WEIGHTS + 5:N_WEIGHTS + 5 + N_BUFS]),
            ChunkBufs(*rest[N_WEIGHTS + 5 + N_BUFS:]))
    step = pl.program_id(0)

    @pl.when(step == 0)
    def _():
        convc_ref[...] = conv0_ref[...]
        lruc_ref[...] = lru0_ref[...]
        ssmc_ref[...] = ssm0_ref[...]
        a = _stage_a_tasks(1, seq, w, convc_ref, xp_ref, sets[0], None)
        for task in (_norm_tasks(seq, x_prev_ref, w, sets[0].h) + a["in_proj"] + a["conv_gates"] + a["s5_in"]
                     + _norm_tasks(seq, x_next_ref, w, sets[1].h)):
            task()

    def body(write, read):
        a = _stage_a_tasks(1, seq, w, convc_ref, xp_ref, write, None)
        l = _stage_l_tasks(1, seq, w, lruc_ref, ssmc_ref, read, None)
        c = _stage_c_tasks(seq, x_prev_ref, y_ref, w, read)
        norm_next = _norm_tasks(seq, x_next_ref, w, read.h)
        for task in (_zip_tasks(l, a["in_proj"])
                     + _zip_tasks(c["s5_out"], _zip_tasks(a["conv_gates"], a["s5_in"]))
                     + _stage_c_tail(c, norm_next)):
            task()

    @pl.when((step % 2 == 0) & (step > 0) & (step < n_chunks))
    def _():
        body(sets[0], sets[1])

    @pl.when((step % 2 == 1) & (step < n_chunks))
    def _():
        body(sets[1], sets[0])

    @pl.when(step == n_chunks)
    def _():
        last = sets[(n_chunks - 1) % 2]
        c = _stage_c_tasks(seq, x_prev_ref, y_ref, w, last)
        for task in (_stage_l_tasks(1, seq, w, lruc_ref, ssmc_ref, last, None)
                     + c["s5_out"] + _stage_c_tail(c, [])):
            task()


def _full_spec(shape):
    zeros = (0,) * len(shape)
    return pl.BlockSpec(shape, lambda i, _z=zeros: _z)


def _layer_spec(shape, layer):
    index = (layer,) + (0,) * (len(shape) - 1)
    return pl.BlockSpec((None,) + tuple(shape[1:]), lambda i, _x=index: _x, pipeline_mode=pl.Buffered(1))


def _chunk_scratch(rows):
    xs_rows = -(-(S5_TIME_PITCH * rows + S5_ROW_PITCH * SUBLANES) // SUBLANES) * SUBLANES
    return [pltpu.VMEM((SSM_SLABS, xs_rows, LANES), F32),
            pltpu.VMEM((rows * LRU_CHUNKS, LANES), F32), pltpu.VMEM((rows * LRU_CHUNKS, LANES), F32),
            pltpu.VMEM((rows, LRU_W), F32), pltpu.VMEM((rows, SSM_W), F32),
            pltpu.VMEM((rows, D_MODEL), BF16)]


def _mixer_call(x2d, states, state_layer, w, layer, nb, seq, pipelined):
    total = x2d.shape[0]
    rows = nb * seq
    conv0, lru0, ssm0 = states
    state_in_specs = [_layer_spec(s.shape, state_layer) for s in states]
    state_specs = [_full_spec(s.shape[1:]) for s in states]
    weight_specs = [_layer_spec(a.shape, layer) for a in w]
    out_shape = (jax.ShapeDtypeStruct((total, D_MODEL), F32),) + tuple(
        jax.ShapeDtypeStruct(s.shape[1:], F32) for s in states)
    conv_scratch = pltpu.VMEM((nb, LRU_CHUNKS, CONV_PITCH * seq + CONV_BASE, LANES), F32)
    scratch = [conv_scratch] + _chunk_scratch(rows)
    params = pltpu.CompilerParams(dimension_semantics=("arbitrary",), vmem_limit_bytes=VMEM_LIMIT_BYTES)
    if not pipelined:
        assert total == rows
        x_spec = pl.BlockSpec((rows, D_MODEL), lambda i: (0, 0))
        return pl.pallas_call(
            functools.partial(_mixer_plain_kernel, nb, seq),
            grid=(1,), in_specs=[x_spec] + state_in_specs + weight_specs,
            out_specs=(x_spec,) + tuple(state_specs), out_shape=out_shape,
            scratch_shapes=scratch, name="mixer_plain", compiler_params=params,
        )(x2d, conv0, lru0, ssm0, *w)
    assert nb == 1 and total % rows == 0
    n_chunks = total // rows
    next_spec = pl.BlockSpec((rows, D_MODEL), lambda i: (jnp.minimum(i + 1, n_chunks - 1), 0))
    prev_spec = pl.BlockSpec((rows, D_MODEL), lambda i: (jnp.maximum(i - 1, 0), 0))
    return pl.pallas_call(
        functools.partial(_mixer_pipelined_kernel, n_chunks, seq),
        grid=(n_chunks + 1,), in_specs=[next_spec, prev_spec] + state_in_specs + weight_specs,
        out_specs=(prev_spec,) + tuple(state_specs), out_shape=out_shape,
        scratch_shapes=scratch + _chunk_scratch(rows), name="mixer", compiler_params=params,
    )(x2d, x2d, conv0, lru0, ssm0, *w)


def _ffn_rows(final_norm, x, gffn_ref, wg_ref, wu_ref, wd_ref, gfin_ref):
    h = _rmsnorm(x, gffn_ref[...]).astype(BF16)
    acc = x
    for c in range(D_FF // FF_CHUNK):
        lo, hi = c * FF_CHUNK, (c + 1) * FF_CHUNK
        gate = _dot(h, wg_ref[:, lo:hi])
        up = _dot(h, wu_ref[:, lo:hi])
        act = (gate * jax.nn.sigmoid(gate) * up).astype(BF16)
        acc = acc + _dot(act, wd_ref[lo:hi, :])
    return _rmsnorm(acc, gfin_ref[...]) if final_norm else acc


def _ffn_kernel(final_norm, main_steps, xa_ref, xb_ref, gffn_ref, wg_ref, wu_ref, wd_ref, gfin_ref,
                oa_ref, ob_ref):
    weights = (gffn_ref, wg_ref, wu_ref, wd_ref, gfin_ref)
    step = pl.program_id(0)

    @pl.when(step < main_steps)
    def _():
        oa_ref[...] = _ffn_rows(final_norm, xa_ref[...], *weights)

    @pl.when(step == main_steps)
    def _():
        ob_ref[...] = _ffn_rows(final_norm, xb_ref[...], *weights)


def _ffn_call(xa, xb, w, layer, gfin, final_norm, tile):
    main_steps = xa.shape[0] // tile
    tile_spec = pl.BlockSpec((tile, D_MODEL), lambda i: (jnp.minimum(i, main_steps - 1), 0))
    in_specs = [tile_spec, _full_spec(xb.shape)]
    in_specs += [_layer_spec(a.shape, layer) for a in w] + [_full_spec(gfin.shape)]
    return pl.pallas_call(
        functools.partial(_ffn_kernel, final_norm, main_steps),
        grid=(main_steps + 1,), in_specs=in_specs,
        out_specs=(tile_spec, _full_spec(xb.shape)),
        out_shape=(jax.ShapeDtypeStruct(xa.shape, F32), jax.ShapeDtypeStruct(xb.shape, F32)), name="ffn",
        compiler_params=pltpu.CompilerParams(dimension_semantics=("arbitrary",),
                                             vmem_limit_bytes=VMEM_LIMIT_BYTES),
    )(xa, xb, *w, gfin)


def _ssm_to_slabs(re, im):
    lead = re.shape[:-2]
    return jnp.concatenate([re.reshape(lead + (2, SUBLANES, LANES)), im.reshape(lead + (2, SUBLANES, LANES))],
                           axis=len(lead))


def _slabs_to_ssm(slabs):
    lead = slabs.shape[:-3]
    return (slabs[..., 0:2, :, :].reshape(lead + (SSM_G, SSM_P)), slabs[..., 2:4, :, :].reshape(lead + (SSM_G, SSM_P)))


def _conv_to_tile(conv):
    pad = [(0, 0)] * (conv.ndim - 2) + [(SUBLANES - (CONV_W - 1), 0), (0, 0)]
    return jnp.pad(conv, pad)


def _lru_to_tile(h):
    lead = h.shape[:-1]
    pad = [(0, 0)] * len(lead) + [(0, SUBLANES - LRU_CHUNKS), (0, 0)]
    return jnp.pad(h.reshape(lead + (LRU_CHUNKS, LANES)), pad)


def _tile_to_lru(tile):
    return tile[..., :LRU_CHUNKS, :].reshape(tile.shape[:-2] + (LRU_W,))


def kernel(x_prompt, x_sample, state_conv, state_lru, state_ssm_re, state_ssm_im, norm_mix, w_in, conv_w, conv_b, lru_wa, lru_ba, lru_wx, lru_bx, lru_lambda, ssm_lambda_re, ssm_lambda_im, ssm_b_re, ssm_b_im, ssm_c_re, ssm_c_im, ssm_d, ssm_log_dt, ssm_w_glu, ssm_b_glu, norm_lru_out, norm_ssm_out, w_out, norm_ffn, w_gate, w_up, w_down, norm_final):
    depth = w_in.shape[0]
    bp, tp, _ = x_prompt.shape
    bs, ts, _ = x_sample.shape
    assert bp == 1

    lbr, lbi, bw, cwt, wgate = _prep(ssm_lambda_re, ssm_lambda_im, ssm_log_dt, ssm_b_re, ssm_b_im,
                                     ssm_c_re, ssm_c_im, lru_wa, lru_wx)

    def rows(v):
        return v.reshape(depth, 1, -1)

    by_name = dict(gmix=norm_mix, convb=conv_b, lam=lru_lambda, ba=lru_ba, bx=lru_bx, dskip=ssm_d,
                   bglu=ssm_b_glu, glru=norm_lru_out, gssm=norm_ssm_out)
    mixer_w = MixerWeights(
        vec=rows(jnp.concatenate([by_name[name] for name, _ in VEC_FIELDS], axis=1)),
        win=w_in.astype(BF16), convw=conv_w, wgate=wgate, lbr=lbr, lbi=lbi, bw=bw, cwt=cwt,
        wglu=ssm_w_glu.astype(BF16), wout=w_out.astype(BF16))
    ffn_w = (rows(norm_ffn), w_gate.astype(BF16), w_up.astype(BF16), w_down.astype(BF16))
    gfin = norm_final.reshape(1, -1)

    yp = x_prompt.reshape(bp * tp, D_MODEL)
    ys = x_sample.reshape(bs * ts, D_MODEL)
    prompt_states = (jnp.zeros((1, bp, SUBLANES, LRU_W), F32), jnp.zeros((1, bp, SUBLANES, LANES), F32),
                     jnp.zeros((1, bp, SSM_SLABS, SUBLANES, LANES), F32))
    sample_states = (_conv_to_tile(state_conv), _lru_to_tile(state_lru), _ssm_to_slabs(state_ssm_re, state_ssm_im))

    prompt_out, sample_out = [], []
    for l in range(depth):
        last = l == depth - 1
        yp, *st_p = _mixer_call(yp, prompt_states, 0, mixer_w, l, 1, PROMPT_CHUNK, True)
        ys, *st_s = _mixer_call(ys, sample_states, l, mixer_w, l, bs, ts, False)
        yp, ys = _ffn_call(yp, ys, ffn_w, l, gfin, last, FFN_TILE)
        prompt_out.append(st_p)
        sample_out.append(st_s)

    def unpack(per_layer):
        conv, lru, ssm = (jnp.stack(v) for v in zip(*per_layer))
        re, im = _slabs_to_ssm(ssm)
        return conv[:, :, SUBLANES - (CONV_W - 1):], _tile_to_lru(lru), re, im

    return (yp.reshape(bp, tp, D_MODEL), ys.reshape(bs, ts, D_MODEL)) + unpack(prompt_out) + unpack(sample_out)
```

```python
import collections
import functools
import math

import jax
import jax.numpy as jnp
from jax import lax
from jax.experimental import pallas as pl
from jax.experimental.pallas import tpu as pltpu

D_MODEL = 1024
LRU_W = 512
LRU_HEADS = 8
LRU_HD = LRU_W // LRU_HEADS
CONV_W = 4
LRU_C = 8.0
SSM_W = 512
SSM_H = 16
SSM_G = SSM_W // SSM_H
SSM_P = 64
IN_W = 2 * LRU_W + SSM_W
D_FF = 2816
EPS = 1e-6

SUBLANES = 8
LANES = 128
MXU_DIM = 256
VMEM_LIMIT_BYTES = 56 * 1024 * 1024

SSM_SLABS = 4
S5_ROWS_PER_CHUNK = LANES // (2 * SSM_H)
S5_TIME_PITCH = 9
S5_ROW_PITCH = 2
LRU_CHUNKS = LRU_W // LANES
CONV_PITCH = 2
CONV_BASE = CONV_PITCH * SUBLANES
GATE_HALVES = LRU_W // MXU_DIM
FF_CHUNK = MXU_DIM
PROMPT_CHUNK = 512
VPU_TASK_ROWS = 512
MXU_TASK_ROWS = 512
FFN_TILE = 1024
FFN_LOOP_ROWS = 512
BF16 = jnp.bfloat16
F32 = jnp.float32

MixerWeights = collections.namedtuple(
    "MixerWeights",
    "vec win convw wgate lbr lbi bw cwt wglu wout")
VEC_FIELDS = (("gmix", D_MODEL), ("convb", LRU_W), ("lam", LRU_W), ("ba", LRU_W), ("bx", LRU_W),
              ("dskip", SSM_W), ("bglu", SSM_W), ("glru", LRU_W), ("gssm", SSM_W))
VEC_OFFSET = {name: sum(width for _, width in VEC_FIELDS[:i]) for i, (name, _) in enumerate(VEC_FIELDS)}
VEC_WIDTH = dict(VEC_FIELDS)
ChunkBufs = collections.namedtuple("ChunkBufs", "xs a b gl us h")


def _rmsnorm(x, gain):
    var = jnp.mean(x * x, axis=-1, keepdims=True)
    return x * lax.rsqrt(var + EPS) * gain


def _dot(a, b):
    return jnp.dot(a, b, preferred_element_type=F32)


_GELU_K1 = -2.0 * math.sqrt(2.0 / math.pi) * math.log2(math.e)
_GELU_K2 = _GELU_K1 * 0.044715


def _gelu(x):
    return x / (1.0 + jnp.exp2(x * (_GELU_K1 + _GELU_K2 * (x * x))))


def _keep_if(pred, new, old):
    return new if pred is None else jnp.where(pred, new, old)


def _lanes(c):
    return slice(c * LANES, (c + 1) * LANES)


def _vec(w, name, lo=0, hi=None):
    hi = VEC_WIDTH[name] if hi is None else hi
    return w.vec[:, VEC_OFFSET[name] + lo:VEC_OFFSET[name] + hi]


def _prep_kernel(depth, lr_ref, li_ref, ldt_ref, br_ref, bi_ref, cr_ref, ci_ref, wa_ref, wx_ref,
                 lbr_ref, lbi_ref, bw_ref, cwt_ref, wgate_ref):
    lr = lr_ref[...]
    li = li_ref[...]
    dt = jnp.exp(ldt_ref[...])
    mag = jnp.exp(lr * dt)
    lbr = mag * jnp.cos(li * dt)
    lbi = mag * jnp.sin(li * dt)
    nr, ni = lbr - 1.0, lbi
    den = lr * lr + li * li
    gr = (nr * lr + ni * li) / den
    gi = (ni * lr - nr * li) / den
    br = br_ref[...]
    bi = bi_ref[...]
    lbr_ref[...] = lbr
    lbi_ref[...] = lbi
    bb = ((gr * br - gi * bi).astype(BF16), (gr * bi + gi * br).astype(BF16))
    cc = (cr_ref[...].astype(BF16), (-ci_ref[...]).astype(BF16))
    bw_ref[...] = jnp.zeros(bw_ref.shape, BF16)
    cwt_ref[...] = jnp.zeros(cwt_ref.shape, BF16)
    wgate_ref[...] = jnp.zeros(wgate_ref.shape, BF16)
    chunks = SSM_W // LANES
    for l in range(depth):
        for g in range(SSM_G):
            s, gg = divmod(g, 2)
            c, sl = divmod(s, S5_ROWS_PER_CHUNK)
            src = slice((l * SSM_G + g) * SSM_H, (l * SSM_G + g + 1) * SSM_H)
            lanes_u = slice((sl * 2 + gg) * SSM_H, (sl * 2 + gg + 1) * SSM_H)
            for reim in range(2):
                col = (sl * 2 + reim) * LANES + gg * SSM_P
                bw_ref[l * chunks + c, lanes_u, col:col + SSM_P] = bb[reim][src, :]
                cwt_ref[l * chunks + c, lanes_u, col:col + SSM_P] = cc[reim][src, :]
        per = LRU_HEADS // GATE_HALVES
        for head in range(LRU_HEADS):
            hf, hh = divmod(head, per)
            src = slice((l * LRU_HEADS + head) * LRU_HD, (l * LRU_HEADS + head + 1) * LRU_HD)
            blk = slice(hh * LRU_HD, (hh + 1) * LRU_HD)
            wgate_ref[l * GATE_HALVES + hf, blk, blk] = wa_ref[src, :].astype(BF16)
            wgate_ref[l * GATE_HALVES + hf, blk, MXU_DIM + hh * LRU_HD:MXU_DIM + (hh + 1) * LRU_HD] = (
                wx_ref[src, :].astype(BF16))


def _prep(lam_re, lam_im, log_dt, b_re, b_im, c_re, c_im, wa, wx):
    depth = lam_re.shape[0]
    rows = depth * SSM_G * SSM_H
    shape4 = (depth, SSM_G, SSM_H, SSM_P)

    def expand(a):
        return jnp.broadcast_to(a[:, :, None, :], shape4).reshape(rows, SSM_P)

    ldt = jnp.broadcast_to(log_dt[:, :, None, None], shape4).reshape(rows, SSM_P)
    br = jnp.swapaxes(b_re, 2, 3).reshape(rows, SSM_P)
    bi = jnp.swapaxes(b_im, 2, 3).reshape(rows, SSM_P)
    chunks = SSM_W // LANES
    sds = jax.ShapeDtypeStruct((rows, SSM_P), F32)
    out_shape = (sds, sds,
                 jax.ShapeDtypeStruct((depth * chunks, LANES, S5_ROWS_PER_CHUNK * 2 * LANES), BF16),
                 jax.ShapeDtypeStruct((depth * chunks, LANES, S5_ROWS_PER_CHUNK * 2 * LANES), BF16),
                 jax.ShapeDtypeStruct((depth * GATE_HALVES, MXU_DIM, 2 * MXU_DIM), BF16))
    lbr, lbi, bw, cwt, wgate = pl.pallas_call(
        functools.partial(_prep_kernel, depth), out_shape=out_shape, name="prep",
    )(expand(lam_re), expand(lam_im), ldt, br, bi, c_re.reshape(rows, SSM_P), c_im.reshape(rows, SSM_P),
      wa.reshape(depth * LRU_W, LRU_HD), wx.reshape(depth * LRU_W, LRU_HD))
    lbr = lbr.reshape(shape4)[:, :, 0, :].reshape(depth, 2, SUBLANES, LANES)
    lbi = lbi.reshape(shape4)[:, :, 0, :].reshape(depth, 2, SUBLANES, LANES)
    return (lbr, lbi, bw.reshape((depth, chunks) + bw.shape[1:]), cwt.reshape((depth, chunks) + cwt.shape[1:]),
            wgate.reshape((depth, GATE_HALVES) + wgate.shape[1:]))


def _s5_time_rows(srow, row0, count):
    return pl.ds(S5_ROW_PITCH * srow + S5_TIME_PITCH * row0, count, stride=S5_TIME_PITCH)


def _s5_step_rows(r):
    return pl.ds(S5_TIME_PITCH * r, SUBLANES, stride=S5_ROW_PITCH)


def _lru_time_rows(c, row0, count):
    return pl.ds(c + LRU_CHUNKS * row0, count, stride=LRU_CHUNKS)


def _blocks(rows, size):
    size = min(size, rows)
    return [(start, size) for start in range(0, rows, size)]


def _zip_tasks(a, b):
    keyed = [((i + 0.5) / len(a), 0, i, t) for i, t in enumerate(a)]
    keyed += [((i + 0.5) / len(b), 1, i, t) for i, t in enumerate(b)]
    return [t for _, _, _, t in sorted(keyed, key=lambda e: e[:3])]


def _norm_tasks(rows, x_ref, w, h_ref):
    def norm(start, size):
        h_ref[start:start + size, :] = _rmsnorm(x_ref[start:start + size, :], _vec(w, "gmix")).astype(BF16)
    return [functools.partial(norm, start, size) for start, size in _blocks(rows, VPU_TASK_ROWS)]


def _stage_a_tasks(nb, seq, w, convc_ref, xp_ref, bufs, live):
    rows = nb * seq
    vpu_blocks = _blocks(rows, VPU_TASK_ROWS)
    mxu_blocks = _blocks(rows, MXU_TASK_ROWS)
    st = {}
    groups = {"in_proj": [], "conv_gates": [], "s5_in": []}

    def in_proj(k, start, size):
        z = _dot(bufs.h[start:start + size, :], w.win[:, k * MXU_DIM:(k + 1) * MXU_DIM])
        st["z", k, start] = z
        if k in (2, 3):
            bufs.gl[start:start + size, (k - 2) * MXU_DIM:(k - 1) * MXU_DIM] = z
        if k in (4, 5):
            bufs.us[start:start + size, (k - 4) * MXU_DIM:(k - 3) * MXU_DIM] = z
    for k in range(IN_W // MXU_DIM):
        for start, size in mxu_blocks:
            groups["in_proj"].append(functools.partial(in_proj, k, start, size))

    def z_lanes(k0, c):
        zk, off = divmod(c * LANES, MXU_DIM)
        parts = [st["z", k0 + zk, s][:, off:off + LANES] for s, n in mxu_blocks]
        return parts[0] if len(parts) == 1 else jnp.concatenate(parts, axis=0)

    def conv(c):
        xl = z_lanes(0, c)
        cw = w.convw[:, _lanes(c)]
        parts = []
        for j in range(nb):
            xl_j = xl[j * seq:(j + 1) * seq, :]
            xp_ref[j, c, pl.ds(0, SUBLANES, stride=CONV_PITCH), :] = convc_ref[j, :, _lanes(c)]
            xp_ref[j, c, pl.ds(CONV_BASE, seq, stride=CONV_PITCH), :] = xl_j
            acc = _vec(w, "convb", c * LANES, (c + 1) * LANES) + xl_j * cw[CONV_W - 1:CONV_W, :]
            for k in range(CONV_W - 1):
                start = CONV_BASE - CONV_PITCH * (CONV_W - 1 - k)
                acc = acc + xp_ref[j, c, pl.ds(start, seq, stride=CONV_PITCH), :] * cw[k:k + 1, :]
            parts.append(acc)
            convc_ref[j, :, _lanes(c)] = _keep_if(live, xl_j[seq - SUBLANES:, :], convc_ref[j, :, _lanes(c)])
        st["xc", c] = parts[0] if nb == 1 else jnp.concatenate(parts, axis=0)
    for c in range(LRU_CHUNKS):
        groups["conv_gates"].append(functools.partial(conv, c))

    def gates_dot(hf):
        chunks = range(hf * MXU_DIM // LANES, (hf + 1) * MXU_DIM // LANES)
        xc = jnp.concatenate([st["xc", c] for c in chunks], axis=1)
        st["xc2", hf] = xc
        st["pre", hf] = _dot(xc.astype(BF16), w.wgate[hf])

    def gates_ew(hf, start, size):
        lo, hi = hf * MXU_DIM, (hf + 1) * MXU_DIM
        lam = _vec(w, "lam", lo, hi)
        neg_c_softplus = -LRU_C * (jnp.maximum(-lam, 0.0) + jnp.log1p(jnp.exp(-jnp.abs(lam))))
        pre = st["pre", hf][start:start + size, :]
        xc = st["xc2", hf][start:start + size, :]
        r = jax.nn.sigmoid(pre[:, :MXU_DIM] + _vec(w, "ba", lo, hi))
        ig = jax.nn.sigmoid(pre[:, MXU_DIM:] + _vec(w, "bx", lo, hi))
        log_a = neg_c_softplus * r
        a = jnp.exp(log_a)
        one_minus_a2 = -jnp.tanh(log_a) * (a * a + 1.0)
        b = jnp.sqrt(one_minus_a2) * (ig * xc)
        for i in range(MXU_DIM // LANES):
            c = hf * MXU_DIM // LANES + i
            bufs.a[_lru_time_rows(c, start, size), :] = a[:, _lanes(i)]
            bufs.b[_lru_time_rows(c, start, size), :] = b[:, _lanes(i)]
    for hf in range(GATE_HALVES):
        groups["conv_gates"].append(functools.partial(gates_dot, hf))
        for start, size in vpu_blocks:
            groups["conv_gates"].append(functools.partial(gates_ew, hf, start, size))

    def s5_in(c, start, size):
        zk, off = divmod(c * LANES, MXU_DIM)
        us_bf = st["z", 4 + zk, start][:, off:off + LANES].astype(BF16)
        bu = _dot(us_bf, w.bw[c])
        half = (c * S5_ROWS_PER_CHUNK) // SUBLANES
        for sl in range(S5_ROWS_PER_CHUNK):
            srow = (c * S5_ROWS_PER_CHUNK + sl) % SUBLANES
            for reim in range(2):
                col = (sl * 2 + reim) * LANES
                bufs.xs[reim * 2 + half, _s5_time_rows(srow, start, size), :] = bu[:, col:col + LANES]
    for c in range(SSM_W // LANES):
        for start, size in mxu_blocks:
            groups["s5_in"].append(functools.partial(s5_in, c, start, size))
    return groups


def _stage_l_tasks(nb, seq, w, lruc_ref, ssmc_ref, bufs, live):
    st = {}
    tasks = []
    row = lambda: lax.broadcasted_iota(jnp.int32, (SUBLANES, LANES), 0)

    def begin(j):
        st["lb"] = ((w.lbr[0], w.lbr[1]), (w.lbi[0], w.lbi[1]))
        st["xr"] = [ssmc_ref[j, 0], ssmc_ref[j, 1]]
        st["xi"] = [ssmc_ref[j, 2], ssmc_ref[j, 3]]
        st["h"] = lruc_ref[j]
        st["old"] = (tuple(st["xr"]), tuple(st["xi"]), st["h"])

    def steps(j, tile):
        lbr, lbi = st["lb"]
        xr, xi = st["xr"], st["xi"]
        for t in range(tile * SUBLANES, (tile + 1) * SUBLANES):
            rows_t = _s5_step_rows(j * seq + t)
            for hv in range(2):
                nr = lbr[hv] * xr[hv] - lbi[hv] * xi[hv] + bufs.xs[hv, rows_t, :]
                ni = lbr[hv] * xi[hv] + lbi[hv] * xr[hv] + bufs.xs[2 + hv, rows_t, :]
                bufs.xs[hv, rows_t, :] = nr
                bufs.xs[2 + hv, rows_t, :] = ni
                xr[hv], xi[hv] = nr, ni
        first_half = row() < LRU_CHUNKS
        h = st["h"]
        for pair in range(tile * SUBLANES // 2, (tile + 1) * SUBLANES // 2):
            r0 = (j * seq + 2 * pair) * LRU_CHUNKS
            a2 = bufs.a[r0:r0 + SUBLANES, :]
            b2 = bufs.b[r0:r0 + SUBLANES, :]
            h_even = a2 * h + b2
            h_odd = a2 * pltpu.roll(h_even, LRU_CHUNKS, 0) + b2
            bufs.b[r0:r0 + SUBLANES, :] = jnp.where(first_half, h_even, h_odd)
            h = pltpu.roll(h_odd, LRU_CHUNKS, 0)
        st["h"] = h

    def end(j):
        old = st["old"]
        for hv in range(2):
            ssmc_ref[j, hv] = _keep_if(live, st["xr"][hv], old[0][hv])
            ssmc_ref[j, 2 + hv] = _keep_if(live, st["xi"][hv], old[1][hv])
        lruc_ref[j] = _keep_if(live, st["h"], old[2])

    for j in range(nb):
        tasks.append(functools.partial(begin, j))
        for tile in range(seq // SUBLANES):
            tasks.append(functools.partial(steps, j, tile))
        tasks.append(functools.partial(end, j))
    return tasks


def _stage_c_tasks(rows, x_ref, y_ref, w, bufs):
    vpu_blocks = _blocks(rows, VPU_TASK_ROWS)
    mxu_blocks = _blocks(rows, MXU_TASK_ROWS)
    chunks = SSM_W // LANES
    st = {}
    groups = {"s5_out": [], "glu_dot": {}, "lru_out": {}, "glu_ew": {}, "out_proj": {}}

    def rows_of(key, start, size):
        parts = [st[key, s] for s, n in vpu_blocks if start <= s < start + size]
        return parts[0] if len(parts) == 1 else jnp.concatenate(parts, axis=0)

    def s5_out(c, start, size):
        pieces = []
        for sl in range(S5_ROWS_PER_CHUNK):
            s = c * S5_ROWS_PER_CHUNK + sl
            half, srow = s // SUBLANES, s % SUBLANES
            pieces.append(bufs.xs[half, _s5_time_rows(srow, start, size), :])
            pieces.append(bufs.xs[2 + half, _s5_time_rows(srow, start, size), :])
        xcat = jnp.concatenate(pieces, axis=1).astype(BF16)
        y = lax.dot_general(xcat, w.cwt[c], (((1,), (1,)), ((), ())), preferred_element_type=F32)
        st["g", c, start] = _gelu(y + _vec(w, "dskip", c * LANES, (c + 1) * LANES) * bufs.us[start:start + size, _lanes(c)])
    for start, size in mxu_blocks:
        for c in range(chunks):
            groups["s5_out"].append(functools.partial(s5_out, c, start, size))

    def lru_out(start, size):
        h = jnp.concatenate([bufs.b[_lru_time_rows(c, start, size), :] for c in range(LRU_CHUNKS)], axis=1)
        st["n_lru", start] = _rmsnorm(h * _gelu(bufs.gl[start:start + size, :]), _vec(w, "glru")).astype(BF16)

    def glu_dot(start, size):
        g = jnp.concatenate([st["g", c, start] for c in range(chunks)], axis=1)
        st["g4", start] = g
        st["gate", start] = _dot(g.astype(BF16), w.wglu[...]) + _vec(w, "bglu")

    def glu_ew(start, size):
        m0 = max(s for s, n in mxu_blocks if s <= start)
        g = st["g4", m0][start - m0:start - m0 + size, :]
        gate = st["gate", m0][start - m0:start - m0 + size, :]
        st["n_ssm", start] = _rmsnorm(g * jax.nn.sigmoid(gate), _vec(w, "gssm")).astype(BF16)

    def out_proj(k, start, size):
        cols = slice(k * MXU_DIM, (k + 1) * MXU_DIM)
        y_ref[start:start + size, cols] = (
            x_ref[start:start + size, cols] + _dot(rows_of("n_lru", start, size), w.wout[0:LRU_W, cols])
            + _dot(rows_of("n_ssm", start, size), w.wout[LRU_W:, cols]))
    for start, size in mxu_blocks:
        inside = [(s, n) for s, n in vpu_blocks if start <= s < start + size]
        groups["glu_dot"][start] = functools.partial(glu_dot, start, size)
        groups["lru_out"][start] = [functools.partial(lru_out, s, n) for s, n in inside]
        groups["glu_ew"][start] = [functools.partial(glu_ew, s, n) for s, n in inside]
        groups["out_proj"][start] = [functools.partial(out_proj, k, start, size) for k in range(D_MODEL // MXU_DIM)]
    return groups


def _stage_c_tail(c_groups, fillers):
    starts = sorted(c_groups["glu_dot"])
    order = _zip_tasks([c_groups["glu_dot"][m] for m in starts], c_groups["lru_out"][starts[0]])
    order += c_groups["glu_ew"][starts[0]]
    for m, nxt in zip(starts, starts[1:] + [None]):
        companions = fillers if nxt is None else c_groups["lru_out"][nxt] + c_groups["glu_ew"][nxt]
        order += _zip_tasks(c_groups["out_proj"][m], companions) if companions else c_groups["out_proj"][m]
    return order


N_WEIGHTS = len(MixerWeights._fields)
N_BUFS = len(ChunkBufs._fields)


def _mixer_plain_kernel(nb, seq, x_ref, conv0_ref, lru0_ref, ssm0_ref, *rest):
    w = MixerWeights(*rest[:N_WEIGHTS])
    y_ref, convc_ref, lruc_ref, ssmc_ref, xp_ref = rest[N_WEIGHTS:N_WEIGHTS + 5]
    bufs = ChunkBufs(*rest[N_WEIGHTS + 5:])
    rows = nb * seq
    convc_ref[...] = conv0_ref[...]
    lruc_ref[...] = lru0_ref[...]
    ssmc_ref[...] = ssm0_ref[...]
    a = _stage_a_tasks(nb, seq, w, convc_ref, xp_ref, bufs, None)
    c = _stage_c_tasks(rows, x_ref, y_ref, w, bufs)
    for task in (_norm_tasks(rows, x_ref, w, bufs.h) + a["in_proj"] + a["conv_gates"] + a["s5_in"]
                 + _stage_l_tasks(nb, seq, w, lruc_ref, ssmc_ref, bufs, None)
                 + c["s5_out"] + _stage_c_tail(c, [])):
        task()


def _mixer_pipelined_kernel(n_chunks, seq, x_next_ref, x_prev_ref, conv0_ref, lru0_ref, ssm0_ref, *rest):
    w = MixerWeights(*rest[:N_WEIGHTS])
    y_ref, convc_ref, lruc_ref, ssmc_ref, xp_ref = rest[N_WEIGHTS:N_WEIGHTS + 5]
    sets = (ChunkBufs(*rest[N_WEIGHTS + 5:N_WEIGHTS + 5 + N_BUFS]),
            ChunkBufs(*rest[N_WEIGHTS + 5 + N_BUFS:]))
    step = pl.program_id(0)

    @pl.when(step == 0)
    def _():
        convc_ref[...] = conv0_ref[...]
        lruc_ref[...] = lru0_ref[...]
        ssmc_ref[...] = ssm0_ref[...]
        for ref in sets[1]:
            ref[...] = jnp.zeros(ref.shape, ref.dtype)
        for task in _norm_tasks(seq, x_prev_ref, w, sets[0].h):
            task()

    def body(write, read):
        a = _stage_a_tasks(1, seq, w, convc_ref, xp_ref, write, step < n_chunks)
        l = _stage_l_tasks(1, seq, w, lruc_ref, ssmc_ref, read, step >= 1)
        c = _stage_c_tasks(seq, x_prev_ref, y_ref, w, read)
        norm_next = _norm_tasks(seq, x_next_ref, w, read.h)
        for task in (_zip_tasks(l, a["in_proj"])
                     + _zip_tasks(c["s5_out"], _zip_tasks(a["conv_gates"], a["s5_in"]))
                     + _stage_c_tail(c, norm_next)):
            task()

    @pl.when(step % 2 == 0)
    def _():
        body(sets[0], sets[1])

    @pl.when(step % 2 == 1)
    def _():
        body(sets[1], sets[0])


def _full_spec(shape):
    zeros = (0,) * len(shape)
    return pl.BlockSpec(shape, lambda i, _z=zeros: _z)


def _layer_spec(shape, layer):
    index = (layer,) + (0,) * (len(shape) - 1)
    return pl.BlockSpec((None,) + tuple(shape[1:]), lambda i, _x=index: _x, pipeline_mode=pl.Buffered(1))


def _chunk_scratch(rows):
    xs_rows = -(-(S5_TIME_PITCH * rows + S5_ROW_PITCH * SUBLANES) // SUBLANES) * SUBLANES
    return [pltpu.VMEM((SSM_SLABS, xs_rows, LANES), F32),
            pltpu.VMEM((rows * LRU_CHUNKS, LANES), F32), pltpu.VMEM((rows * LRU_CHUNKS, LANES), F32),
            pltpu.VMEM((rows, LRU_W), F32), pltpu.VMEM((rows, SSM_W), F32),
            pltpu.VMEM((rows, D_MODEL), BF16)]


def _mixer_call(x2d, states, state_layer, w, layer, nb, seq, pipelined):
    total = x2d.shape[0]
    rows = nb * seq
    conv0, lru0, ssm0 = states
    state_in_specs = [_layer_spec(s.shape, state_layer) for s in states]
    state_specs = [_full_spec(s.shape[1:]) for s in states]
    weight_specs = [_layer_spec(a.shape, layer) for a in w]
    out_shape = (jax.ShapeDtypeStruct((total, D_MODEL), F32),) + tuple(
        jax.ShapeDtypeStruct(s.shape[1:], F32) for s in states)
    conv_scratch = pltpu.VMEM((nb, LRU_CHUNKS, CONV_PITCH * seq + CONV_BASE, LANES), F32)
    scratch = [conv_scratch] + _chunk_scratch(rows)
    params = pltpu.CompilerParams(dimension_semantics=("arbitrary",), vmem_limit_bytes=VMEM_LIMIT_BYTES)
    if not pipelined:
        assert total == rows
        x_spec = pl.BlockSpec((rows, D_MODEL), lambda i: (0, 0))
        return pl.pallas_call(
            functools.partial(_mixer_plain_kernel, nb, seq),
            grid=(1,), in_specs=[x_spec] + state_in_specs + weight_specs,
            out_specs=(x_spec,) + tuple(state_specs), out_shape=out_shape,
            scratch_shapes=scratch, name="mixer_plain", compiler_params=params,
        )(x2d, conv0, lru0, ssm0, *w)
    assert nb == 1 and total % rows == 0
    n_chunks = total // rows
    next_spec = pl.BlockSpec((rows, D_MODEL), lambda i: (jnp.minimum(i + 1, n_chunks - 1), 0))
    prev_spec = pl.BlockSpec((rows, D_MODEL), lambda i: (jnp.maximum(i - 1, 0), 0))
    return pl.pallas_call(
        functools.partial(_mixer_pipelined_kernel, n_chunks, seq),
        grid=(n_chunks + 1,), in_specs=[next_spec, prev_spec] + state_in_specs + weight_specs,
        out_specs=(prev_spec,) + tuple(state_specs), out_shape=out_shape,
        scratch_shapes=scratch + _chunk_scratch(rows), name="mixer", compiler_params=params,
    )(x2d, x2d, conv0, lru0, ssm0, *w)


def _ffn_rows(final_norm, x, gffn_ref, wg_ref, wu_ref, wd_ref, gfin_ref):
    h = _rmsnorm(x, gffn_ref[...]).astype(BF16)
    acc = x
    for c in range(D_FF // FF_CHUNK):
        lo, hi = c * FF_CHUNK, (c + 1) * FF_CHUNK
        gate = _dot(h, wg_ref[:, lo:hi])
        up = _dot(h, wu_ref[:, lo:hi])
        act = (gate * jax.nn.sigmoid(gate) * up).astype(BF16)
        acc = acc + _dot(act, wd_ref[lo:hi, :])
    return _rmsnorm(acc, gfin_ref[...]) if final_norm else acc


def _ffn_kernel(final_norm, main_steps, xa_ref, xb_ref, gffn_ref, wg_ref, wu_ref, wd_ref, gfin_ref,
                oa_ref, ob_ref):
    weights = (gffn_ref, wg_ref, wu_ref, wd_ref, gfin_ref)
    step = pl.program_id(0)

    @pl.when(step < main_steps)
    def _():
        def rows_block(i, carry):
            rows = pl.ds(pl.multiple_of(i * FFN_LOOP_ROWS, FFN_LOOP_ROWS), FFN_LOOP_ROWS)
            oa_ref[rows, :] = _ffn_rows(final_norm, xa_ref[rows, :], *weights)
            return carry
        lax.fori_loop(0, xa_ref.shape[0] // FFN_LOOP_ROWS, rows_block, 0)

    @pl.when(step == main_steps)
    def _():
        ob_ref[...] = _ffn_rows(final_norm, xb_ref[...], *weights)


def _ffn_call(xa, xb, w, layer, gfin, final_norm, tile):
    main_steps = xa.shape[0] // tile
    tile_spec = pl.BlockSpec((tile, D_MODEL), lambda i: (jnp.minimum(i, main_steps - 1), 0))
    in_specs = [tile_spec, _full_spec(xb.shape)]
    in_specs += [_layer_spec(a.shape, layer) for a in w] + [_full_spec(gfin.shape)]
    return pl.pallas_call(
        functools.partial(_ffn_kernel, final_norm, main_steps),
        grid=(main_steps + 1,), in_specs=in_specs,
        out_specs=(tile_spec, _full_spec(xb.shape)),
        out_shape=(jax.ShapeDtypeStruct(xa.shape, F32), jax.ShapeDtypeStruct(xb.shape, F32)), name="ffn",
        compiler_params=pltpu.CompilerParams(dimension_semantics=("arbitrary",),
                                             vmem_limit_bytes=VMEM_LIMIT_BYTES),
    )(xa, xb, *w, gfin)


def _ssm_to_slabs(re, im):
    lead = re.shape[:-2]
    return jnp.concatenate([re.reshape(lead + (2, SUBLANES, LANES)), im.reshape(lead + (2, SUBLANES, LANES))],
                           axis=len(lead))


def _slabs_to_ssm(slabs):
    lead = slabs.shape[:-3]
    return (slabs[..., 0:2, :, :].reshape(lead + (SSM_G, SSM_P)), slabs[..., 2:4, :, :].reshape(lead + (SSM_G, SSM_P)))


def _conv_to_tile(conv):
    pad = [(0, 0)] * (conv.ndim - 2) + [(SUBLANES - (CONV_W - 1), 0), (0, 0)]
    return jnp.pad(conv, pad)


def _lru_to_tile(h):
    lead = h.shape[:-1]
    pad = [(0, 0)] * len(lead) + [(0, SUBLANES - LRU_CHUNKS), (0, 0)]
    return jnp.pad(h.reshape(lead + (LRU_CHUNKS, LANES)), pad)


def _tile_to_lru(tile):
    return tile[..., :LRU_CHUNKS, :].reshape(tile.shape[:-2] + (LRU_W,))


def kernel(x_prompt, x_sample, state_conv, state_lru, state_ssm_re, state_ssm_im, norm_mix, w_in, conv_w, conv_b, lru_wa, lru_ba, lru_wx, lru_bx, lru_lambda, ssm_lambda_re, ssm_lambda_im, ssm_b_re, ssm_b_im, ssm_c_re, ssm_c_im, ssm_d, ssm_log_dt, ssm_w_glu, ssm_b_glu, norm_lru_out, norm_ssm_out, w_out, norm_ffn, w_gate, w_up, w_down, norm_final):
    depth = w_in.shape[0]
    bp, tp, _ = x_prompt.shape
    bs, ts, _ = x_sample.shape
    assert bp == 1

    lbr, lbi, bw, cwt, wgate = _prep(ssm_lambda_re, ssm_lambda_im, ssm_log_dt, ssm_b_re, ssm_b_im,
                                     ssm_c_re, ssm_c_im, lru_wa, lru_wx)

    def rows(v):
        return v.reshape(depth, 1, -1)

    by_name = dict(gmix=norm_mix, convb=conv_b, lam=lru_lambda, ba=lru_ba, bx=lru_bx, dskip=ssm_d,
                   bglu=ssm_b_glu, glru=norm_lru_out, gssm=norm_ssm_out)
    mixer_w = MixerWeights(
        vec=rows(jnp.concatenate([by_name[name] for name, _ in VEC_FIELDS], axis=1)),
        win=w_in.astype(BF16), convw=conv_w, wgate=wgate, lbr=lbr, lbi=lbi, bw=bw, cwt=cwt,
        wglu=ssm_w_glu.astype(BF16), wout=w_out.astype(BF16))
    ffn_w = (rows(norm_ffn), w_gate.astype(BF16), w_up.astype(BF16), w_down.astype(BF16))
    gfin = norm_final.reshape(1, -1)

    yp = x_prompt.reshape(bp * tp, D_MODEL)
    ys = x_sample.reshape(bs * ts, D_MODEL)
    prompt_states = (jnp.zeros((1, bp, SUBLANES, LRU_W), F32), jnp.zeros((1, bp, SUBLANES, LANES), F32),
                     jnp.zeros((1, bp, SSM_SLABS, SUBLANES, LANES), F32))
    sample_states = (_conv_to_tile(state_conv), _lru_to_tile(state_lru), _ssm_to_slabs(state_ssm_re, state_ssm_im))

    prompt_out, sample_out = [], []
    for l in range(depth):
        last = l == depth - 1
        yp, *st_p = _mixer_call(yp, prompt_states, 0, mixer_w, l, 1, PROMPT_CHUNK, True)
        ys, *st_s = _mixer_call(ys, sample_states, l, mixer_w, l, bs, ts, False)
        yp, ys = _ffn_call(yp, ys, ffn_w, l, gfin, last, FFN_TILE)
        prompt_out.append(st_p)
        sample_out.append(st_s)

    def unpack(per_layer):
        conv, lru, ssm = (jnp.stack(v) for v in zip(*per_layer))
        re, im = _slabs_to_ssm(ssm)
        return conv[:, :, SUBLANES - (CONV_W - 1):], _tile_to_lru(lru), re, im

    return (yp.reshape(bp, tp, D_MODEL), ys.reshape(bs, ts, D_MODEL)) + unpack(prompt_out) + unpack(sample_out)
```

```python
import collections
import functools
import math

import jax
import jax.numpy as jnp
from jax import lax
from jax.experimental import pallas as pl
from jax.experimental.pallas import tpu as pltpu

D_MODEL = 1024
LRU_W = 512
LRU_HEADS = 8
LRU_HD = LRU_W // LRU_HEADS
CONV_W = 4
LRU_C = 8.0
SSM_W = 512
SSM_H = 16
SSM_G = SSM_W // SSM_H
SSM_P = 64
IN_W = 2 * LRU_W + SSM_W
D_FF = 2816
EPS = 1e-6

SUBLANES = 8
LANES = 128
MXU_DIM = 256
VMEM_LIMIT_BYTES = 56 * 1024 * 1024

SSM_SLABS = 4
S5_ROWS_PER_CHUNK = LANES // (2 * SSM_H)
S5_TIME_PITCH = 9
S5_ROW_PITCH = 2
LRU_CHUNKS = LRU_W // LANES
CONV_PITCH = 2
CONV_BASE = CONV_PITCH * SUBLANES
GATE_HALVES = LRU_W // MXU_DIM
FF_CHUNK = MXU_DIM
PROMPT_CHUNK = 512
VPU_TASK_ROWS = 512
MXU_TASK_ROWS = 512
FFN_TILE = 1024
BF16 = jnp.bfloat16
F32 = jnp.float32

MixerWeights = collections.namedtuple(
    "MixerWeights",
    "vec win convw wgate lbr lbi bw cwt wglu wout")
VEC_FIELDS = (("gmix", D_MODEL), ("convb", LRU_W), ("lam", LRU_W), ("ba", LRU_W), ("bx", LRU_W),
              ("dskip", SSM_W), ("bglu", SSM_W), ("glru", LRU_W), ("gssm", SSM_W))
VEC_OFFSET = {name: sum(width for _, width in VEC_FIELDS[:i]) for i, (name, _) in enumerate(VEC_FIELDS)}
VEC_WIDTH = dict(VEC_FIELDS)
ChunkBufs = collections.namedtuple("ChunkBufs", "xs a b gl us h")


def _rmsnorm(x, gain):
    var = jnp.mean(x * x, axis=-1, keepdims=True)
    return x * lax.rsqrt(var + EPS) * gain


def _dot(a, b):
    return jnp.dot(a, b, preferred_element_type=F32)


_GELU_K1 = -2.0 * math.sqrt(2.0 / math.pi) * math.log2(math.e)
_GELU_K2 = _GELU_K1 * 0.044715


def _gelu(x):
    return x / (1.0 + jnp.exp2(x * (_GELU_K1 + _GELU_K2 * (x * x))))


def _keep_if(pred, new, old):
    return new if pred is None else jnp.where(pred, new, old)


def _lanes(c):
    return slice(c * LANES, (c + 1) * LANES)


def _vec(w, name, lo=0, hi=None):
    hi = VEC_WIDTH[name] if hi is None else hi
    return w.vec[:, VEC_OFFSET[name] + lo:VEC_OFFSET[name] + hi]


def _prep_kernel(depth, lr_ref, li_ref, ldt_ref, br_ref, bi_ref, cr_ref, ci_ref, wa_ref, wx_ref, *rest):
    field_refs = rest[:len(VEC_FIELDS)]
    lbr_ref, lbi_ref, bw_ref, cwt_ref, wgate_ref, vec_ref = rest[len(VEC_FIELDS):]
    lr = lr_ref[...]
    li = li_ref[...]
    dt = jnp.exp(ldt_ref[...])
    mag = jnp.exp(lr * dt)
    lbr = mag * jnp.cos(li * dt)
    lbi = mag * jnp.sin(li * dt)
    nr, ni = lbr - 1.0, lbi
    den = lr * lr + li * li
    gr = (nr * lr + ni * li) / den
    gi = (ni * lr - nr * li) / den
    bw_ref[...] = jnp.zeros(bw_ref.shape, BF16)
    cwt_ref[...] = jnp.zeros(cwt_ref.shape, BF16)
    wgate_ref[...] = jnp.zeros(wgate_ref.shape, BF16)
    chunks = SSM_W // LANES
    for l in range(depth):
        for g in range(SSM_G):
            s, gg = divmod(g, 2)
            c, sl = divmod(s, S5_ROWS_PER_CHUNK)
            row = slice(l * SSM_G + g, l * SSM_G + g + 1)
            src = slice((l * SSM_G + g) * SSM_H, (l * SSM_G + g + 1) * SSM_H)
            lanes_u = slice((sl * 2 + gg) * SSM_H, (sl * 2 + gg + 1) * SSM_H)
            state_lanes = slice(gg * SSM_P, (gg + 1) * SSM_P)
            lbr_ref[l * 2 + s // SUBLANES, s % SUBLANES:s % SUBLANES + 1, state_lanes] = lbr[row, :]
            lbi_ref[l * 2 + s // SUBLANES, s % SUBLANES:s % SUBLANES + 1, state_lanes] = lbi[row, :]
            br, bi = br_ref[src, :], bi_ref[src, :]
            bb = (gr[row, :] * br - gi[row, :] * bi, gr[row, :] * bi + gi[row, :] * br)
            cc = (cr_ref[src, :], -ci_ref[src, :])
            for reim in range(2):
                col = (sl * 2 + reim) * LANES + gg * SSM_P
                bw_ref[l * chunks + c, lanes_u, col:col + SSM_P] = bb[reim].astype(BF16)
                cwt_ref[l * chunks + c, lanes_u, col:col + SSM_P] = cc[reim].astype(BF16)
        per = LRU_HEADS // GATE_HALVES
        for head in range(LRU_HEADS):
            hf, hh = divmod(head, per)
            src = slice((l * LRU_HEADS + head) * LRU_HD, (l * LRU_HEADS + head + 1) * LRU_HD)
            blk = slice(hh * LRU_HD, (hh + 1) * LRU_HD)
            wgate_ref[l * GATE_HALVES + hf, blk, blk] = wa_ref[src, :].astype(BF16)
            wgate_ref[l * GATE_HALVES + hf, blk, MXU_DIM + hh * LRU_HD:MXU_DIM + (hh + 1) * LRU_HD] = (
                wx_ref[src, :].astype(BF16))
        for (name, width), ref in zip(VEC_FIELDS, field_refs):
            vec_ref[l, :, VEC_OFFSET[name]:VEC_OFFSET[name] + width] = ref[l:l + 1, :]


def _prep(lam_re, lam_im, log_dt, b_re, b_im, c_re, c_im, wa, wx, fields):
    depth = lam_re.shape[0]
    groups = depth * SSM_G
    rows = groups * SSM_H
    br = jnp.swapaxes(b_re, 2, 3).reshape(rows, SSM_P)
    bi = jnp.swapaxes(b_im, 2, 3).reshape(rows, SSM_P)
    chunks = SSM_W // LANES
    vec_total = sum(width for _, width in VEC_FIELDS)
    slab = jax.ShapeDtypeStruct((depth * 2, SUBLANES, LANES), F32)
    out_shape = (slab, slab,
                 jax.ShapeDtypeStruct((depth * chunks, LANES, S5_ROWS_PER_CHUNK * 2 * LANES), BF16),
                 jax.ShapeDtypeStruct((depth * chunks, LANES, S5_ROWS_PER_CHUNK * 2 * LANES), BF16),
                 jax.ShapeDtypeStruct((depth * GATE_HALVES, MXU_DIM, 2 * MXU_DIM), BF16),
                 jax.ShapeDtypeStruct((depth, 1, vec_total), F32))
    lbr, lbi, bw, cwt, wgate, vec = pl.pallas_call(
        functools.partial(_prep_kernel, depth), out_shape=out_shape, name="prep",
    )(lam_re.reshape(groups, SSM_P), lam_im.reshape(groups, SSM_P), log_dt.reshape(groups, 1), br, bi,
      c_re.reshape(rows, SSM_P), c_im.reshape(rows, SSM_P),
      wa.reshape(depth * LRU_W, LRU_HD), wx.reshape(depth * LRU_W, LRU_HD), *fields)
    return (lbr.reshape(depth, 2, SUBLANES, LANES), lbi.reshape(depth, 2, SUBLANES, LANES),
            bw.reshape((depth, chunks) + bw.shape[1:]), cwt.reshape((depth, chunks) + cwt.shape[1:]),
            wgate.reshape((depth, GATE_HALVES) + wgate.shape[1:]), vec)


def _s5_time_rows(srow, row0, count):
    return pl.ds(S5_ROW_PITCH * srow + S5_TIME_PITCH * row0, count, stride=S5_TIME_PITCH)


def _s5_step_rows(r):
    return pl.ds(S5_TIME_PITCH * r, SUBLANES, stride=S5_ROW_PITCH)


def _lru_time_rows(c, row0, count):
    return pl.ds(c + LRU_CHUNKS * row0, count, stride=LRU_CHUNKS)


def _blocks(rows, size):
    size = min(size, rows)
    return [(start, size) for start in range(0, rows, size)]


def _zip_tasks(a, b):
    keyed = [((i + 0.5) / len(a), 0, i, t) for i, t in enumerate(a)]
    keyed += [((i + 0.5) / len(b), 1, i, t) for i, t in enumerate(b)]
    return [t for _, _, _, t in sorted(keyed, key=lambda e: e[:3])]


def _norm_tasks(rows, x_ref, w, h_ref):
    def norm(start, size):
        h_ref[start:start + size, :] = _rmsnorm(x_ref[start:start + size, :], _vec(w, "gmix")).astype(BF16)
    return [functools.partial(norm, start, size) for start, size in _blocks(rows, VPU_TASK_ROWS)]


def _stage_a_tasks(nb, seq, w, convc_ref, xp_ref, bufs, live):
    rows = nb * seq
    vpu_blocks = _blocks(rows, VPU_TASK_ROWS)
    mxu_blocks = _blocks(rows, MXU_TASK_ROWS)
    st = {}
    groups = {"in_proj": [], "conv_gates": [], "s5_in": []}

    def in_proj(k, start, size):
        z = _dot(bufs.h[start:start + size, :], w.win[:, k * MXU_DIM:(k + 1) * MXU_DIM])
        st["z", k, start] = z
        if k in (2, 3):
            bufs.gl[start:start + size, (k - 2) * MXU_DIM:(k - 1) * MXU_DIM] = z
        if k in (4, 5):
            bufs.us[start:start + size, (k - 4) * MXU_DIM:(k - 3) * MXU_DIM] = z
    for k in range(IN_W // MXU_DIM):
        for start, size in mxu_blocks:
            groups["in_proj"].append(functools.partial(in_proj, k, start, size))

    def z_lanes(k0, c):
        zk, off = divmod(c * LANES, MXU_DIM)
        parts = [st["z", k0 + zk, s][:, off:off + LANES] for s, n in mxu_blocks]
        return parts[0] if len(parts) == 1 else jnp.concatenate(parts, axis=0)

    def conv(c):
        xl = z_lanes(0, c)
        cw = w.convw[:, _lanes(c)]
        parts = []
        for j in range(nb):
            xl_j = xl[j * seq:(j + 1) * seq, :]
            xp_ref[j, c, pl.ds(0, SUBLANES, stride=CONV_PITCH), :] = convc_ref[j, :, _lanes(c)]
            xp_ref[j, c, pl.ds(CONV_BASE, seq, stride=CONV_PITCH), :] = xl_j
            acc = _vec(w, "convb", c * LANES, (c + 1) * LANES) + xl_j * cw[CONV_W - 1:CONV_W, :]
            for k in range(CONV_W - 1):
                start = CONV_BASE - CONV_PITCH * (CONV_W - 1 - k)
                acc = acc + xp_ref[j, c, pl.ds(start, seq, stride=CONV_PITCH), :] * cw[k:k + 1, :]
            parts.append(acc)
            convc_ref[j, :, _lanes(c)] = _keep_if(live, xl_j[seq - SUBLANES:, :], convc_ref[j, :, _lanes(c)])
        st["xc", c] = parts[0] if nb == 1 else jnp.concatenate(parts, axis=0)
    for c in range(LRU_CHUNKS):
        groups["conv_gates"].append(functools.partial(conv, c))

    def gates_dot(hf):
        chunks = range(hf * MXU_DIM // LANES, (hf + 1) * MXU_DIM // LANES)
        xc = jnp.concatenate([st["xc", c] for c in chunks], axis=1)
        st["xc2", hf] = xc
        st["pre", hf] = _dot(xc.astype(BF16), w.wgate[hf])

    def gates_ew(hf, start, size):
        lo, hi = hf * MXU_DIM, (hf + 1) * MXU_DIM
        lam = _vec(w, "lam", lo, hi)
        neg_c_softplus = -LRU_C * (jnp.maximum(-lam, 0.0) + jnp.log1p(jnp.exp(-jnp.abs(lam))))
        pre = st["pre", hf][start:start + size, :]
        xc = st["xc2", hf][start:start + size, :]
        r = jax.nn.sigmoid(pre[:, :MXU_DIM] + _vec(w, "ba", lo, hi))
        ig = jax.nn.sigmoid(pre[:, MXU_DIM:] + _vec(w, "bx", lo, hi))
        log_a = neg_c_softplus * r
        a = jnp.exp(log_a)
        one_minus_a2 = -jnp.tanh(log_a) * (a * a + 1.0)
        b = jnp.sqrt(one_minus_a2) * (ig * xc)
        for i in range(MXU_DIM // LANES):
            c = hf * MXU_DIM // LANES + i
            bufs.a[_lru_time_rows(c, start, size), :] = a[:, _lanes(i)]
            bufs.b[_lru_time_rows(c, start, size), :] = b[:, _lanes(i)]
    for hf in range(GATE_HALVES):
        groups["conv_gates"].append(functools.partial(gates_dot, hf))
        for start, size in vpu_blocks:
            groups["conv_gates"].append(functools.partial(gates_ew, hf, start, size))

    def s5_in(c, start, size):
        zk, off = divmod(c * LANES, MXU_DIM)
        us_bf = st["z", 4 + zk, start][:, off:off + LANES].astype(BF16)
        bu = _dot(us_bf, w.bw[c])
        half = (c * S5_ROWS_PER_CHUNK) // SUBLANES
        for sl in range(S5_ROWS_PER_CHUNK):
            srow = (c * S5_ROWS_PER_CHUNK + sl) % SUBLANES
            for reim in range(2):
                col = (sl * 2 + reim) * LANES
                bufs.xs[reim * 2 + half, _s5_time_rows(srow, start, size), :] = bu[:, col:col + LANES]
    for c in range(SSM_W // LANES):
        for start, size in mxu_blocks:
            groups["s5_in"].append(functools.partial(s5_in, c, start, size))
    return groups


def _stage_l_tasks(nb, seq, w, lruc_ref, ssmc_ref, bufs, live):
    st = {}
    tasks = []
    row = lambda: lax.broadcasted_iota(jnp.int32, (SUBLANES, LANES), 0)

    def begin(j):
        st["lb"] = ((w.lbr[0], w.lbr[1]), (w.lbi[0], w.lbi[1]))
        st["xr"] = [ssmc_ref[j, 0], ssmc_ref[j, 1]]
        st["xi"] = [ssmc_ref[j, 2], ssmc_ref[j, 3]]
        st["h"] = lruc_ref[j]
        st["old"] = (tuple(st["xr"]), tuple(st["xi"]), st["h"])

    def steps(j, tile):
        lbr, lbi = st["lb"]
        xr, xi = st["xr"], st["xi"]
        for t in range(tile * SUBLANES, (tile + 1) * SUBLANES):
            rows_t = _s5_step_rows(j * seq + t)
            for hv in range(2):
                nr = lbr[hv] * xr[hv] - lbi[hv] * xi[hv] + bufs.xs[hv, rows_t, :]
                ni = lbr[hv] * xi[hv] + lbi[hv] * xr[hv] + bufs.xs[2 + hv, rows_t, :]
                bufs.xs[hv, rows_t, :] = nr
                bufs.xs[2 + hv, rows_t, :] = ni
                xr[hv], xi[hv] = nr, ni
        first_half = row() < LRU_CHUNKS
        h = st["h"]
        for pair in range(tile * SUBLANES // 2, (tile + 1) * SUBLANES // 2):
            r0 = (j * seq + 2 * pair) * LRU_CHUNKS
            a2 = bufs.a[r0:r0 + SUBLANES, :]
            b2 = bufs.b[r0:r0 + SUBLANES, :]
            h_even = a2 * h + b2
            h_odd = a2 * pltpu.roll(h_even, LRU_CHUNKS, 0) + b2
            bufs.b[r0:r0 + SUBLANES, :] = jnp.where(first_half, h_even, h_odd)
            h = pltpu.roll(h_odd, LRU_CHUNKS, 0)
        st["h"] = h

    def end(j):
        old = st["old"]
        for hv in range(2):
            ssmc_ref[j, hv] = _keep_if(live, st["xr"][hv], old[0][hv])
            ssmc_ref[j, 2 + hv] = _keep_if(live, st["xi"][hv], old[1][hv])
        lruc_ref[j] = _keep_if(live, st["h"], old[2])

    for j in range(nb):
        tasks.append(functools.partial(begin, j))
        for tile in range(seq // SUBLANES):
            tasks.append(functools.partial(steps, j, tile))
        tasks.append(functools.partial(end, j))
    return tasks


def _stage_c_tasks(rows, x_ref, y_ref, w, bufs):
    vpu_blocks = _blocks(rows, VPU_TASK_ROWS)
    mxu_blocks = _blocks(rows, MXU_TASK_ROWS)
    chunks = SSM_W // LANES
    st = {}
    groups = {"s5_out": [], "glu_dot": {}, "lru_out": {}, "glu_ew": {}, "out_proj": {}}

    def rows_of(key, start, size):
        parts = [st[key, s] for s, n in vpu_blocks if start <= s < start + size]
        return parts[0] if len(parts) == 1 else jnp.concatenate(parts, axis=0)

    def s5_out(c, start, size):
        pieces = []
        for sl in range(S5_ROWS_PER_CHUNK):
            s = c * S5_ROWS_PER_CHUNK + sl
            half, srow = s // SUBLANES, s % SUBLANES
            pieces.append(bufs.xs[half, _s5_time_rows(srow, start, size), :])
            pieces.append(bufs.xs[2 + half, _s5_time_rows(srow, start, size), :])
        xcat = jnp.concatenate(pieces, axis=1).astype(BF16)
        y = lax.dot_general(xcat, w.cwt[c], (((1,), (1,)), ((), ())), preferred_element_type=F32)
        st["g", c, start] = _gelu(y + _vec(w, "dskip", c * LANES, (c + 1) * LANES) * bufs.us[start:start + size, _lanes(c)])
    for start, size in mxu_blocks:
        for c in range(chunks):
            groups["s5_out"].append(functools.partial(s5_out, c, start, size))

    def lru_out(start, size):
        h = jnp.concatenate([bufs.b[_lru_time_rows(c, start, size), :] for c in range(LRU_CHUNKS)], axis=1)
        st["n_lru", start] = _rmsnorm(h * _gelu(bufs.gl[start:start + size, :]), _vec(w, "glru")).astype(BF16)

    def glu_dot(start, size):
        g = jnp.concatenate([st["g", c, start] for c in range(chunks)], axis=1)
        st["g4", start] = g
        st["gate", start] = _dot(g.astype(BF16), w.wglu[...]) + _vec(w, "bglu")

    def glu_ew(start, size):
        m0 = max(s for s, n in mxu_blocks if s <= start)
        g = st["g4", m0][start - m0:start - m0 + size, :]
        gate = st["gate", m0][start - m0:start - m0 + size, :]
        st["n_ssm", start] = _rmsnorm(g * jax.nn.sigmoid(gate), _vec(w, "gssm")).astype(BF16)

    def out_proj(k, start, size):
        cols = slice(k * MXU_DIM, (k + 1) * MXU_DIM)
        mix = jnp.concatenate([rows_of("n_lru", start, size), rows_of("n_ssm", start, size)], axis=1)
        y_ref[start:start + size, cols] = x_ref[start:start + size, cols] + _dot(mix, w.wout[:, cols])
    for start, size in mxu_blocks:
        inside = [(s, n) for s, n in vpu_blocks if start <= s < start + size]
        groups["glu_dot"][start] = functools.partial(glu_dot, start, size)
        groups["lru_out"][start] = [functools.partial(lru_out, s, n) for s, n in inside]
        groups["glu_ew"][start] = [functools.partial(glu_ew, s, n) for s, n in inside]
        groups["out_proj"][start] = [functools.partial(out_proj, k, start, size) for k in range(D_MODEL // MXU_DIM)]
    return groups


def _stage_c_tail(c_groups, fillers):
    starts = sorted(c_groups["glu_dot"])
    order = _zip_tasks([c_groups["glu_dot"][m] for m in starts], c_groups["lru_out"][starts[0]])
    order += c_groups["glu_ew"][starts[0]]
    for m, nxt in zip(starts, starts[1:] + [None]):
        companions = fillers if nxt is None else c_groups["lru_out"][nxt] + c_groups["glu_ew"][nxt]
        order += _zip_tasks(c_groups["out_proj"][m], companions) if companions else c_groups["out_proj"][m]
    return order


N_WEIGHTS = len(MixerWeights._fields)
N_BUFS = len(ChunkBufs._fields)


def _mixer_plain_kernel(nb, seq, x_ref, conv0_ref, lru0_ref, ssm0_ref, *rest):
    w = MixerWeights(*rest[:N_WEIGHTS])
    y_ref, convc_ref, lruc_ref, ssmc_ref, xp_ref = rest[N_WEIGHTS:N_WEIGHTS + 5]
    bufs = ChunkBufs(*rest[N_WEIGHTS + 5:])
    rows = nb * seq
    convc_ref[...] = conv0_ref[...]
    lruc_ref[...] = lru0_ref[...]
    ssmc_ref[...] = ssm0_ref[...]
    a = _stage_a_tasks(nb, seq, w, convc_ref, xp_ref, bufs, None)
    c = _stage_c_tasks(rows, x_ref, y_ref, w, bufs)
    for task in (_norm_tasks(rows, x_ref, w, bufs.h) + a["in_proj"] + a["conv_gates"] + a["s5_in"]
                 + _stage_l_tasks(nb, seq, w, lruc_ref, ssmc_ref, bufs, None)
                 + c["s5_out"] + _stage_c_tail(c, [])):
        task()


def _mixer_pipelined_kernel(n_chunks, seq, x_next_ref, x_prev_ref, conv0_ref, lru0_ref, ssm0_ref, *rest):
    w = MixerWeights(*rest[:N_WEIGHTS])
    y_ref, convc_ref, lruc_ref, ssmc_ref, xp_ref = rest[N_WEIGHTS:N_WEIGHTS + 5]
    sets = (ChunkBufs(*rest[N_WEIGHTS + 5:N_WEIGHTS + 5 + N_BUFS]),
            ChunkBufs(*rest[N_WEIGHTS + 5 + N_BUFS:]))
    step = pl.program_id(0)

    @pl.when(step == 0)
    def _():
        convc_ref[...] = conv0_ref[...]
        lruc_ref[...] = lru0_ref[...]
        ssmc_ref[...] = ssm0_ref[...]
        for ref in sets[1]:
            ref[...] = jnp.zeros(ref.shape, ref.dtype)
        for task in _norm_tasks(seq, x_prev_ref, w, sets[0].h):
            task()

    def body(write, read):
        a = _stage_a_tasks(1, seq, w, convc_ref, xp_ref, write, step < n_chunks)
        l = _stage_l_tasks(1, seq, w, lruc_ref, ssmc_ref, read, step >= 1)
        c = _stage_c_tasks(seq, x_prev_ref, y_ref, w, read)
        norm_next = _norm_tasks(seq, x_next_ref, w, read.h)
        for task in (_zip_tasks(l, a["in_proj"])
                     + _zip_tasks(c["s5_out"], _zip_tasks(a["conv_gates"], a["s5_in"]))
                     + _stage_c_tail(c, norm_next)):
            task()

    @pl.when(step % 2 == 0)
    def _():
        body(sets[0], sets[1])

    @pl.when(step % 2 == 1)
    def _():
        body(sets[1], sets[0])


def _full_spec(shape):
    zeros = (0,) * len(shape)
    return pl.BlockSpec(shape, lambda i, _z=zeros: _z)


def _layer_spec(shape, layer):
    index = (layer,) + (0,) * (len(shape) - 1)
    return pl.BlockSpec((None,) + tuple(shape[1:]), lambda i, _x=index: _x, pipeline_mode=pl.Buffered(1))


def _chunk_scratch(rows):
    xs_rows = -(-(S5_TIME_PITCH * rows + S5_ROW_PITCH * SUBLANES) // SUBLANES) * SUBLANES
    return [pltpu.VMEM((SSM_SLABS, xs_rows, LANES), F32),
            pltpu.VMEM((rows * LRU_CHUNKS, LANES), F32), pltpu.VMEM((rows * LRU_CHUNKS, LANES), F32),
            pltpu.VMEM((rows, LRU_W), F32), pltpu.VMEM((rows, SSM_W), F32),
            pltpu.VMEM((rows, D_MODEL), BF16)]


def _mixer_call(x2d, states, state_layer, w, layer, nb, seq, pipelined):
    total = x2d.shape[0]
    rows = nb * seq
    conv0, lru0, ssm0 = states
    state_in_specs = [_layer_spec(s.shape, state_layer) for s in states]
    state_specs = [_full_spec(s.shape[1:]) for s in states]
    weight_specs = [_layer_spec(a.shape, layer) for a in w]
    out_shape = (jax.ShapeDtypeStruct((total, D_MODEL), F32),) + tuple(
        jax.ShapeDtypeStruct(s.shape[1:], F32) for s in states)
    conv_scratch = pltpu.VMEM((nb, LRU_CHUNKS, CONV_PITCH * seq + CONV_BASE, LANES), F32)
    scratch = [conv_scratch] + _chunk_scratch(rows)
    params = pltpu.CompilerParams(dimension_semantics=("arbitrary",), vmem_limit_bytes=VMEM_LIMIT_BYTES)
    if not pipelined:
        assert total == rows
        x_spec = pl.BlockSpec((rows, D_MODEL), lambda i: (0, 0))
        return pl.pallas_call(
            functools.partial(_mixer_plain_kernel, nb, seq),
            grid=(1,), in_specs=[x_spec] + state_in_specs + weight_specs,
            out_specs=(x_spec,) + tuple(state_specs), out_shape=out_shape,
            scratch_shapes=scratch, name="mixer_plain", compiler_params=params,
        )(x2d, conv0, lru0, ssm0, *w)
    assert nb == 1 and total % rows == 0
    n_chunks = total // rows
    next_spec = pl.BlockSpec((rows, D_MODEL), lambda i: (jnp.minimum(i + 1, n_chunks - 1), 0))
    prev_spec = pl.BlockSpec((rows, D_MODEL), lambda i: (jnp.maximum(i - 1, 0), 0))
    return pl.pallas_call(
        functools.partial(_mixer_pipelined_kernel, n_chunks, seq),
        grid=(n_chunks + 1,), in_specs=[next_spec, prev_spec] + state_in_specs + weight_specs,
        out_specs=(prev_spec,) + tuple(state_specs), out_shape=out_shape,
        scratch_shapes=scratch + _chunk_scratch(rows), name="mixer", compiler_params=params,
    )(x2d, x2d, conv0, lru0, ssm0, *w)


def _ffn_rows(final_norm, x, gffn_ref, wg_ref, wu_ref, wd_ref, gfin_ref):
    h = _rmsnorm(x, gffn_ref[...]).astype(BF16)
    acts = []
    for c in range(D_FF // FF_CHUNK):
        lo, hi = c * FF_CHUNK, (c + 1) * FF_CHUNK
        gate = _dot(h, wg_ref[:, lo:hi])
        up = _dot(h, wu_ref[:, lo:hi])
        acts.append((gate * jax.nn.sigmoid(gate) * up).astype(BF16))
    acc = x + _dot(jnp.concatenate(acts, axis=1), wd_ref[...])
    return _rmsnorm(acc, gfin_ref[...]) if final_norm else acc


def _ffn_kernel(final_norm, main_steps, xa_ref, xb_ref, gffn_ref, wg_ref, wu_ref, wd_ref, gfin_ref,
                oa_ref, ob_ref):
    weights = (gffn_ref, wg_ref, wu_ref, wd_ref, gfin_ref)
    step = pl.program_id(0)

    @pl.when(step < main_steps)
    def _():
        oa_ref[...] = _ffn_rows(final_norm, xa_ref[...], *weights)

    @pl.when(step == main_steps)
    def _():
        ob_ref[...] = _ffn_rows(final_norm, xb_ref[...], *weights)


def _ffn_call(xa, xb, w, layer, gfin, final_norm, tile):
    main_steps = xa.shape[0] // tile
    tile_spec = pl.BlockSpec((tile, D_MODEL), lambda i: (jnp.minimum(i, main_steps - 1), 0))
    in_specs = [tile_spec, _full_spec(xb.shape)]
    in_specs += [_layer_spec(a.shape, layer) for a in w] + [_full_spec(gfin.shape)]
    return pl.pallas_call(
        functools.partial(_ffn_kernel, final_norm, main_steps),
        grid=(main_steps + 1,), in_specs=in_specs,
        out_specs=(tile_spec, _full_spec(xb.shape)),
        out_shape=(jax.ShapeDtypeStruct(xa.shape, F32), jax.ShapeDtypeStruct(xb.shape, F32)), name="ffn",
        compiler_params=pltpu.CompilerParams(dimension_semantics=("arbitrary",),
                                             vmem_limit_bytes=VMEM_LIMIT_BYTES),
    )(xa, xb, *w, gfin)


def _ssm_to_slabs(re, im):
    lead = re.shape[:-2]
    return jnp.concatenate([re.reshape(lead + (2, SUBLANES, LANES)), im.reshape(lead + (2, SUBLANES, LANES))],
                           axis=len(lead))


def _slabs_to_ssm(slabs):
    lead = slabs.shape[:-3]
    return (slabs[..., 0:2, :, :].reshape(lead + (SSM_G, SSM_P)), slabs[..., 2:4, :, :].reshape(lead + (SSM_G, SSM_P)))


def _conv_to_tile(conv):
    pad = [(0, 0)] * (conv.ndim - 2) + [(SUBLANES - (CONV_W - 1), 0), (0, 0)]
    return jnp.pad(conv, pad)


def _lru_to_tile(h):
    lead = h.shape[:-1]
    pad = [(0, 0)] * len(lead) + [(0, SUBLANES - LRU_CHUNKS), (0, 0)]
    return jnp.pad(h.reshape(lead + (LRU_CHUNKS, LANES)), pad)


def _tile_to_lru(tile):
    return tile[..., :LRU_CHUNKS, :].reshape(tile.shape[:-2] + (LRU_W,))


def kernel(x_prompt, x_sample, state_conv, state_lru, state_ssm_re, state_ssm_im, norm_mix, w_in, conv_w, conv_b, lru_wa, lru_ba, lru_wx, lru_bx, lru_lambda, ssm_lambda_re, ssm_lambda_im, ssm_b_re, ssm_b_im, ssm_c_re, ssm_c_im, ssm_d, ssm_log_dt, ssm_w_glu, ssm_b_glu, norm_lru_out, norm_ssm_out, w_out, norm_ffn, w_gate, w_up, w_down, norm_final):
    depth = w_in.shape[0]
    bp, tp, _ = x_prompt.shape
    bs, ts, _ = x_sample.shape
    assert bp == 1

    by_name = dict(gmix=norm_mix, convb=conv_b, lam=lru_lambda, ba=lru_ba, bx=lru_bx, dskip=ssm_d,
                   bglu=ssm_b_glu, glru=norm_lru_out, gssm=norm_ssm_out)
    lbr, lbi, bw, cwt, wgate, vec = _prep(ssm_lambda_re, ssm_lambda_im, ssm_log_dt, ssm_b_re, ssm_b_im,
                                          ssm_c_re, ssm_c_im, lru_wa, lru_wx,
                                          [by_name[name] for name, _ in VEC_FIELDS])

    def rows(v):
        return v.reshape(depth, 1, -1)

    mixer_w = MixerWeights(
        vec=vec, win=w_in.astype(BF16), convw=conv_w, wgate=wgate, lbr=lbr, lbi=lbi, bw=bw, cwt=cwt,
        wglu=ssm_w_glu.astype(BF16), wout=w_out.astype(BF16))
    ffn_w = (rows(norm_ffn), w_gate.astype(BF16), w_up.astype(BF16), w_down.astype(BF16))
    gfin = norm_final.reshape(1, -1)

    yp = x_prompt.reshape(bp * tp, D_MODEL)
    ys = x_sample.reshape(bs * ts, D_MODEL)
    prompt_states = (jnp.zeros((1, bp, SUBLANES, LRU_W), F32), jnp.zeros((1, bp, SUBLANES, LANES), F32),
                     jnp.zeros((1, bp, SSM_SLABS, SUBLANES, LANES), F32))
    sample_states = (_conv_to_tile(state_conv), _lru_to_tile(state_lru), _ssm_to_slabs(state_ssm_re, state_ssm_im))

    prompt_out, sample_out = [], []
    for l in range(depth):
        last = l == depth - 1
        yp, *st_p = _mixer_call(yp, prompt_states, 0, mixer_w, l, 1, PROMPT_CHUNK, True)
        ys, *st_s = _mixer_call(ys, sample_states, l, mixer_w, l, bs, ts, False)
        yp, ys = _ffn_call(yp, ys, ffn_w, l, gfin, last, FFN_TILE)
        prompt_out.append(st_p)
        sample_out.append(st_s)

    def unpack(per_layer):
        conv, lru, ssm = (jnp.stack(v) for v in zip(*per_layer))
        re, im = _slabs_to_ssm(ssm)
        return conv[:, :, SUBLANES - (CONV_W - 1):], _tile_to_lru(lru), re, im

    return (yp.reshape(bp, tp, D_MODEL), ys.reshape(bs, ts, D_MODEL)) + unpack(prompt_out) + unpack(sample_out)
```

```python
import collections
import functools
import math

import jax
import jax.numpy as jnp
from jax import lax
from jax.experimental import pallas as pl
from jax.experimental.pallas import tpu as pltpu

D_MODEL = 1024
LRU_W = 512
LRU_HEADS = 8
LRU_HD = LRU_W // LRU_HEADS
CONV_W = 4
LRU_C = 8.0
SSM_W = 512
SSM_H = 16
SSM_G = SSM_W // SSM_H
SSM_P = 64
IN_W = 2 * LRU_W + SSM_W
D_FF = 2816
EPS = 1e-6

SUBLANES = 8
LANES = 128
MXU_DIM = 256
VMEM_LIMIT_BYTES = 60 * 1024 * 1024

SSM_SLABS = 4
S5_ROWS_PER_CHUNK = LANES // (2 * SSM_H)
S5_TIME_PITCH = 9
S5_ROW_PITCH = 2
LRU_CHUNKS = LRU_W // LANES
CONV_PITCH = 2
CONV_BASE = CONV_PITCH * SUBLANES
GATE_HALVES = LRU_W // MXU_DIM
FF_CHUNK = MXU_DIM
PROMPT_CHUNK = 512
VPU_TASK_ROWS = 512
MXU_TASK_ROWS = 512
FFN_TILE = 1024
BF16 = jnp.bfloat16
F32 = jnp.float32

MixerWeights = collections.namedtuple(
    "MixerWeights",
    "vec win convw wgate lbr lbi bw cwt wglu wout")
VEC_FIELDS = (("gmix", D_MODEL), ("convb", LRU_W), ("lam", LRU_W), ("ba", LRU_W), ("bx", LRU_W),
              ("dskip", SSM_W), ("bglu", SSM_W), ("glru", LRU_W), ("gssm", SSM_W))
VEC_OFFSET = {name: sum(width for _, width in VEC_FIELDS[:i]) for i, (name, _) in enumerate(VEC_FIELDS)}
VEC_WIDTH = dict(VEC_FIELDS)
ChunkBufs = collections.namedtuple("ChunkBufs", "xs a b gl us h")


def _rmsnorm(x, gain):
    var = jnp.mean(x * x, axis=-1, keepdims=True)
    return x * lax.rsqrt(var + EPS) * gain


def _dot(a, b):
    return jnp.dot(a, b, preferred_element_type=F32)


_GELU_K1 = -2.0 * math.sqrt(2.0 / math.pi) * math.log2(math.e)
_GELU_K2 = _GELU_K1 * 0.044715


def _gelu(x):
    return x / (1.0 + jnp.exp2(x * (_GELU_K1 + _GELU_K2 * (x * x))))


def _keep_if(pred, new, old):
    return new if pred is None else jnp.where(pred, new, old)


def _lanes(c):
    return slice(c * LANES, (c + 1) * LANES)


def _vec(w, name, lo=0, hi=None):
    hi = VEC_WIDTH[name] if hi is None else hi
    return w.vec[:, VEC_OFFSET[name] + lo:VEC_OFFSET[name] + hi]


def _prep_kernel(depth, lr_ref, li_ref, ldt_ref, br_ref, bi_ref, cr_ref, ci_ref, wa_ref, wx_ref, *rest):
    field_refs = rest[:len(VEC_FIELDS)]
    lbr_ref, lbi_ref, bw_ref, cwt_ref, wgate_ref, vec_ref = rest[len(VEC_FIELDS):]
    lr = lr_ref[...]
    li = li_ref[...]
    dt = jnp.exp(ldt_ref[...])
    mag = jnp.exp(lr * dt)
    lbr = mag * jnp.cos(li * dt)
    lbi = mag * jnp.sin(li * dt)
    nr, ni = lbr - 1.0, lbi
    den = lr * lr + li * li
    gr = (nr * lr + ni * li) / den
    gi = (ni * lr - nr * li) / den
    bw_ref[...] = jnp.zeros(bw_ref.shape, BF16)
    cwt_ref[...] = jnp.zeros(cwt_ref.shape, BF16)
    wgate_ref[...] = jnp.zeros(wgate_ref.shape, BF16)
    chunks = SSM_W // LANES
    for l in range(depth):
        for g in range(SSM_G):
            s, gg = divmod(g, 2)
            c, sl = divmod(s, S5_ROWS_PER_CHUNK)
            row = slice(l * SSM_G + g, l * SSM_G + g + 1)
            src = slice((l * SSM_G + g) * SSM_H, (l * SSM_G + g + 1) * SSM_H)
            lanes_u = slice((sl * 2 + gg) * SSM_H, (sl * 2 + gg + 1) * SSM_H)
            state_lanes = slice(gg * SSM_P, (gg + 1) * SSM_P)
            lbr_ref[l * 2 + s // SUBLANES, s % SUBLANES:s % SUBLANES + 1, state_lanes] = lbr[row, :]
            lbi_ref[l * 2 + s // SUBLANES, s % SUBLANES:s % SUBLANES + 1, state_lanes] = lbi[row, :]
            br, bi = br_ref[src, :], bi_ref[src, :]
            bb = (gr[row, :] * br - gi[row, :] * bi, gr[row, :] * bi + gi[row, :] * br)
            cc = (cr_ref[src, :], -ci_ref[src, :])
            for reim in range(2):
                col = (sl * 2 + reim) * LANES + gg * SSM_P
                bw_ref[l * chunks + c, lanes_u, col:col + SSM_P] = bb[reim].astype(BF16)
                cwt_ref[l * chunks + c, lanes_u, col:col + SSM_P] = cc[reim].astype(BF16)
        per = LRU_HEADS // GATE_HALVES
        for head in range(LRU_HEADS):
            hf, hh = divmod(head, per)
            src = slice((l * LRU_HEADS + head) * LRU_HD, (l * LRU_HEADS + head + 1) * LRU_HD)
            blk = slice(hh * LRU_HD, (hh + 1) * LRU_HD)
            wgate_ref[l * GATE_HALVES + hf, blk, blk] = wa_ref[src, :].astype(BF16)
            wgate_ref[l * GATE_HALVES + hf, blk, MXU_DIM + hh * LRU_HD:MXU_DIM + (hh + 1) * LRU_HD] = (
                wx_ref[src, :].astype(BF16))
        for (name, width), ref in zip(VEC_FIELDS, field_refs):
            vec_ref[l, :, VEC_OFFSET[name]:VEC_OFFSET[name] + width] = ref[l:l + 1, :]


def _prep(lam_re, lam_im, log_dt, b_re, b_im, c_re, c_im, wa, wx, fields):
    depth = lam_re.shape[0]
    groups = depth * SSM_G
    rows = groups * SSM_H
    br = jnp.swapaxes(b_re, 2, 3).reshape(rows, SSM_P)
    bi = jnp.swapaxes(b_im, 2, 3).reshape(rows, SSM_P)
    chunks = SSM_W // LANES
    vec_total = sum(width for _, width in VEC_FIELDS)
    slab = jax.ShapeDtypeStruct((depth * 2, SUBLANES, LANES), F32)
    out_shape = (slab, slab,
                 jax.ShapeDtypeStruct((depth * chunks, LANES, S5_ROWS_PER_CHUNK * 2 * LANES), BF16),
                 jax.ShapeDtypeStruct((depth * chunks, LANES, S5_ROWS_PER_CHUNK * 2 * LANES), BF16),
                 jax.ShapeDtypeStruct((depth * GATE_HALVES, MXU_DIM, 2 * MXU_DIM), BF16),
                 jax.ShapeDtypeStruct((depth, 1, vec_total), F32))
    lbr, lbi, bw, cwt, wgate, vec = pl.pallas_call(
        functools.partial(_prep_kernel, depth), out_shape=out_shape, name="prep",
    )(lam_re.reshape(groups, SSM_P), lam_im.reshape(groups, SSM_P), log_dt.reshape(groups, 1), br, bi,
      c_re.reshape(rows, SSM_P), c_im.reshape(rows, SSM_P),
      wa.reshape(depth * LRU_W, LRU_HD), wx.reshape(depth * LRU_W, LRU_HD), *fields)
    return (lbr.reshape(depth, 2, SUBLANES, LANES), lbi.reshape(depth, 2, SUBLANES, LANES),
            bw.reshape((depth, chunks) + bw.shape[1:]), cwt.reshape((depth, chunks) + cwt.shape[1:]),
            wgate.reshape((depth, GATE_HALVES) + wgate.shape[1:]), vec)


def _s5_time_rows(srow, row0, count):
    return pl.ds(S5_ROW_PITCH * srow + S5_TIME_PITCH * row0, count, stride=S5_TIME_PITCH)


def _s5_step_rows(r):
    return pl.ds(S5_TIME_PITCH * r, SUBLANES, stride=S5_ROW_PITCH)


def _lru_time_rows(c, row0, count):
    return pl.ds(c + LRU_CHUNKS * row0, count, stride=LRU_CHUNKS)


def _blocks(rows, size):
    size = min(size, rows)
    return [(start, size) for start in range(0, rows, size)]


def _zip_tasks(a, b):
    keyed = [((i + 0.5) / len(a), 0, i, t) for i, t in enumerate(a)]
    keyed += [((i + 0.5) / len(b), 1, i, t) for i, t in enumerate(b)]
    return [t for _, _, _, t in sorted(keyed, key=lambda e: e[:3])]


def _norm_tasks(rows, x_ref, w, h_ref):
    def norm(start, size):
        h_ref[start:start + size, :] = _rmsnorm(x_ref[start:start + size, :], _vec(w, "gmix")).astype(BF16)
    return [functools.partial(norm, start, size) for start, size in _blocks(rows, VPU_TASK_ROWS)]


def _stage_a_tasks(nb, seq, w, convc_ref, xp_ref, bufs, live):
    rows = nb * seq
    vpu_blocks = _blocks(rows, VPU_TASK_ROWS)
    mxu_blocks = _blocks(rows, MXU_TASK_ROWS)
    st = {}
    groups = {"in_proj": [], "conv_gates": [], "s5_in": []}

    def in_proj(k, start, size):
        z = _dot(bufs.h[start:start + size, :], w.win[:, k * MXU_DIM:(k + 1) * MXU_DIM])
        st["z", k, start] = z
        if k in (2, 3):
            bufs.gl[start:start + size, (k - 2) * MXU_DIM:(k - 1) * MXU_DIM] = z
        if k in (4, 5):
            bufs.us[start:start + size, (k - 4) * MXU_DIM:(k - 3) * MXU_DIM] = z
    for k in range(IN_W // MXU_DIM):
        for start, size in mxu_blocks:
            groups["in_proj"].append(functools.partial(in_proj, k, start, size))

    def z_lanes(k0, c):
        zk, off = divmod(c * LANES, MXU_DIM)
        parts = [st["z", k0 + zk, s][:, off:off + LANES] for s, n in mxu_blocks]
        return parts[0] if len(parts) == 1 else jnp.concatenate(parts, axis=0)

    def conv(c):
        xl = z_lanes(0, c)
        cw = w.convw[:, _lanes(c)]
        parts = []
        for j in range(nb):
            xl_j = xl[j * seq:(j + 1) * seq, :]
            xp_ref[j, c, pl.ds(0, SUBLANES, stride=CONV_PITCH), :] = convc_ref[j, :, _lanes(c)]
            xp_ref[j, c, pl.ds(CONV_BASE, seq, stride=CONV_PITCH), :] = xl_j
            acc = _vec(w, "convb", c * LANES, (c + 1) * LANES) + xl_j * cw[CONV_W - 1:CONV_W, :]
            for k in range(CONV_W - 1):
                start = CONV_BASE - CONV_PITCH * (CONV_W - 1 - k)
                acc = acc + xp_ref[j, c, pl.ds(start, seq, stride=CONV_PITCH), :] * cw[k:k + 1, :]
            parts.append(acc)
            convc_ref[j, :, _lanes(c)] = _keep_if(live, xl_j[seq - SUBLANES:, :], convc_ref[j, :, _lanes(c)])
        st["xc", c] = parts[0] if nb == 1 else jnp.concatenate(parts, axis=0)
    for c in range(LRU_CHUNKS):
        groups["conv_gates"].append(functools.partial(conv, c))

    def gates_dot(hf):
        chunks = range(hf * MXU_DIM // LANES, (hf + 1) * MXU_DIM // LANES)
        xc = jnp.concatenate([st["xc", c] for c in chunks], axis=1)
        st["xc2", hf] = xc
        st["pre", hf] = _dot(xc.astype(BF16), w.wgate[hf])

    def gates_ew(hf, start, size):
        lo, hi = hf * MXU_DIM, (hf + 1) * MXU_DIM
        lam = _vec(w, "lam", lo, hi)
        neg_c_softplus = -LRU_C * (jnp.maximum(-lam, 0.0) + jnp.log1p(jnp.exp(-jnp.abs(lam))))
        pre = st["pre", hf][start:start + size, :]
        xc = st["xc2", hf][start:start + size, :]
        r = jax.nn.sigmoid(pre[:, :MXU_DIM] + _vec(w, "ba", lo, hi))
        ig = jax.nn.sigmoid(pre[:, MXU_DIM:] + _vec(w, "bx", lo, hi))
        log_a = neg_c_softplus * r
        a = jnp.exp(log_a)
        one_minus_a2 = -jnp.tanh(log_a) * (a * a + 1.0)
        b = jnp.sqrt(one_minus_a2) * (ig * xc)
        for i in range(MXU_DIM // LANES):
            c = hf * MXU_DIM // LANES + i
            bufs.a[_lru_time_rows(c, start, size), :] = a[:, _lanes(i)]
            bufs.b[_lru_time_rows(c, start, size), :] = b[:, _lanes(i)]
    for hf in range(GATE_HALVES):
        groups["conv_gates"].append(functools.partial(gates_dot, hf))
        for start, size in vpu_blocks:
            groups["conv_gates"].append(functools.partial(gates_ew, hf, start, size))

    def s5_in(c, start, size):
        zk, off = divmod(c * LANES, MXU_DIM)
        us_bf = st["z", 4 + zk, start][:, off:off + LANES].astype(BF16)
        bu = _dot(us_bf, w.bw[c])
        half = (c * S5_ROWS_PER_CHUNK) // SUBLANES
        for sl in range(S5_ROWS_PER_CHUNK):
            srow = (c * S5_ROWS_PER_CHUNK + sl) % SUBLANES
            for reim in range(2):
                col = (sl * 2 + reim) * LANES
                bufs.xs[reim * 2 + half, _s5_time_rows(srow, start, size), :] = bu[:, col:col + LANES]
    for c in range(SSM_W // LANES):
        for start, size in mxu_blocks:
            groups["s5_in"].append(functools.partial(s5_in, c, start, size))
    return groups


def _stage_l_tasks(nb, seq, w, lruc_ref, ssmc_ref, bufs, live):
    st = {}
    tasks = []
    row = lambda: lax.broadcasted_iota(jnp.int32, (SUBLANES, LANES), 0)

    def begin(j):
        st["lb"] = ((w.lbr[0], w.lbr[1]), (w.lbi[0], w.lbi[1]))
        st["xr"] = [ssmc_ref[j, 0], ssmc_ref[j, 1]]
        st["xi"] = [ssmc_ref[j, 2], ssmc_ref[j, 3]]
        st["h"] = lruc_ref[j]
        st["old"] = (tuple(st["xr"]), tuple(st["xi"]), st["h"])

    def steps(j, tile):
        lbr, lbi = st["lb"]
        xr, xi = st["xr"], st["xi"]
        for t in range(tile * SUBLANES, (tile + 1) * SUBLANES):
            rows_t = _s5_step_rows(j * seq + t)
            for hv in range(2):
                nr = lbr[hv] * xr[hv] - lbi[hv] * xi[hv] + bufs.xs[hv, rows_t, :]
                ni = lbr[hv] * xi[hv] + lbi[hv] * xr[hv] + bufs.xs[2 + hv, rows_t, :]
                bufs.xs[hv, rows_t, :] = nr
                bufs.xs[2 + hv, rows_t, :] = ni
                xr[hv], xi[hv] = nr, ni
        first_half = row() < LRU_CHUNKS
        h = st["h"]
        for pair in range(tile * SUBLANES // 2, (tile + 1) * SUBLANES // 2):
            r0 = (j * seq + 2 * pair) * LRU_CHUNKS
            a2 = bufs.a[r0:r0 + SUBLANES, :]
            b2 = bufs.b[r0:r0 + SUBLANES, :]
            h_even = a2 * h + b2
            h_odd = a2 * pltpu.roll(h_even, LRU_CHUNKS, 0) + b2
            bufs.b[r0:r0 + SUBLANES, :] = jnp.where(first_half, h_even, h_odd)
            h = pltpu.roll(h_odd, LRU_CHUNKS, 0)
        st["h"] = h

    def end(j):
        old = st["old"]
        for hv in range(2):
            ssmc_ref[j, hv] = _keep_if(live, st["xr"][hv], old[0][hv])
            ssmc_ref[j, 2 + hv] = _keep_if(live, st["xi"][hv], old[1][hv])
        lruc_ref[j] = _keep_if(live, st["h"], old[2])

    for j in range(nb):
        tasks.append(functools.partial(begin, j))
        for tile in range(seq // SUBLANES):
            tasks.append(functools.partial(steps, j, tile))
        tasks.append(functools.partial(end, j))
    return tasks


def _stage_c_tasks(rows, x_ref, y_ref, w, bufs):
    vpu_blocks = _blocks(rows, VPU_TASK_ROWS)
    mxu_blocks = _blocks(rows, MXU_TASK_ROWS)
    chunks = SSM_W // LANES
    st = {}
    groups = {"s5_out": [], "glu_dot": {}, "lru_out": {}, "glu_ew": {}, "out_proj": {}}

    def rows_of(key, start, size):
        parts = [st[key, s] for s, n in vpu_blocks if start <= s < start + size]
        return parts[0] if len(parts) == 1 else jnp.concatenate(parts, axis=0)

    def s5_out(c, start, size):
        pieces = []
        for sl in range(S5_ROWS_PER_CHUNK):
            s = c * S5_ROWS_PER_CHUNK + sl
            half, srow = s // SUBLANES, s % SUBLANES
            pieces.append(bufs.xs[half, _s5_time_rows(srow, start, size), :])
            pieces.append(bufs.xs[2 + half, _s5_time_rows(srow, start, size), :])
        xcat = jnp.concatenate(pieces, axis=1).astype(BF16)
        y = lax.dot_general(xcat, w.cwt[c], (((1,), (1,)), ((), ())), preferred_element_type=F32)
        st["g", c, start] = _gelu(y + _vec(w, "dskip", c * LANES, (c + 1) * LANES) * bufs.us[start:start + size, _lanes(c)])
    for start, size in mxu_blocks:
        for c in range(chunks):
            groups["s5_out"].append(functools.partial(s5_out, c, start, size))

    def lru_out(start, size):
        h = jnp.concatenate([bufs.b[_lru_time_rows(c, start, size), :] for c in range(LRU_CHUNKS)], axis=1)
        st["n_lru", start] = _rmsnorm(h * _gelu(bufs.gl[start:start + size, :]), _vec(w, "glru")).astype(BF16)

    def glu_dot(start, size):
        g = jnp.concatenate([st["g", c, start] for c in range(chunks)], axis=1)
        st["g4", start] = g
        st["gate", start] = _dot(g.astype(BF16), w.wglu[...]) + _vec(w, "bglu")

    def glu_ew(start, size):
        m0 = max(s for s, n in mxu_blocks if s <= start)
        g = st["g4", m0][start - m0:start - m0 + size, :]
        gate = st["gate", m0][start - m0:start - m0 + size, :]
        st["n_ssm", start] = _rmsnorm(g * jax.nn.sigmoid(gate), _vec(w, "gssm")).astype(BF16)

    def out_proj(k, start, size):
        cols = slice(k * MXU_DIM, (k + 1) * MXU_DIM)
        mix = jnp.concatenate([rows_of("n_lru", start, size), rows_of("n_ssm", start, size)], axis=1)
        y_ref[start:start + size, cols] = x_ref[start:start + size, cols] + _dot(mix, w.wout[:, cols])
    for start, size in mxu_blocks:
        inside = [(s, n) for s, n in vpu_blocks if start <= s < start + size]
        groups["glu_dot"][start] = functools.partial(glu_dot, start, size)
        groups["lru_out"][start] = [functools.partial(lru_out, s, n) for s, n in inside]
        groups["glu_ew"][start] = [functools.partial(glu_ew, s, n) for s, n in inside]
        groups["out_proj"][start] = [functools.partial(out_proj, k, start, size) for k in range(D_MODEL // MXU_DIM)]
    return groups


def _stage_c_tail(c_groups, fillers):
    starts = sorted(c_groups["glu_dot"])
    order = _zip_tasks([c_groups["glu_dot"][m] for m in starts], c_groups["lru_out"][starts[0]])
    order += c_groups["glu_ew"][starts[0]]
    for m, nxt in zip(starts, starts[1:] + [None]):
        companions = fillers if nxt is None else c_groups["lru_out"][nxt] + c_groups["glu_ew"][nxt]
        order += _zip_tasks(c_groups["out_proj"][m], companions) if companions else c_groups["out_proj"][m]
    return order


N_WEIGHTS = len(MixerWeights._fields)
N_BUFS = len(ChunkBufs._fields)
N_CAST = 3
CAST_BLOCK_ROWS = {D_MODEL: 32, D_FF: 128}


def _load_states(nb, conv0_ref, lru0_ref, re0_ref, im0_ref, convc_ref, lruc_ref, ssmc_ref):
    convc_ref[...] = jnp.zeros(convc_ref.shape, F32)
    convc_ref[:, SUBLANES - (CONV_W - 1):, :] = conv0_ref[...]
    lruc_ref[...] = jnp.zeros(lruc_ref.shape, F32)
    for j in range(nb):
        for c in range(LRU_CHUNKS):
            lruc_ref[j, c:c + 1, :] = lru0_ref[j:j + 1, _lanes(c)]
        for g in range(SSM_G):
            s, gg = divmod(g, 2)
            dst = (s % SUBLANES, slice(gg * SSM_P, (gg + 1) * SSM_P))
            ssmc_ref[j, s // SUBLANES, dst[0]:dst[0] + 1, dst[1]] = re0_ref[j, g:g + 1, :]
            ssmc_ref[j, 2 + s // SUBLANES, dst[0]:dst[0] + 1, dst[1]] = im0_ref[j, g:g + 1, :]


def _mixer_plain_kernel(nb, seq, x_ref, conv0_ref, lru0_ref, re0_ref, im0_ref, *rest):
    w = MixerWeights(*rest[:N_WEIGHTS])
    y_ref, convc_ref, lruc_ref, ssmc_ref, xp_ref = rest[N_WEIGHTS:N_WEIGHTS + 5]
    bufs = ChunkBufs(*rest[N_WEIGHTS + 5:])
    rows = nb * seq
    _load_states(nb, conv0_ref, lru0_ref, re0_ref, im0_ref, convc_ref, lruc_ref, ssmc_ref)
    a = _stage_a_tasks(nb, seq, w, convc_ref, xp_ref, bufs, None)
    c = _stage_c_tasks(rows, x_ref, y_ref, w, bufs)
    for task in (_norm_tasks(rows, x_ref, w, bufs.h) + a["in_proj"] + a["conv_gates"] + a["s5_in"]
                 + _stage_l_tasks(nb, seq, w, lruc_ref, ssmc_ref, bufs, None)
                 + c["s5_out"] + _stage_c_tail(c, [])):
        task()


def _mixer_pipelined_kernel(n_chunks, seq, x_next_ref, x_prev_ref, *rest):
    w = MixerWeights(*rest[:N_WEIGHTS])
    cast_in = rest[N_WEIGHTS:N_WEIGHTS + N_CAST]
    y_ref, convc_ref, lruc_ref, ssmc_ref = rest[N_WEIGHTS + N_CAST:N_WEIGHTS + N_CAST + 4]
    cast_out = rest[N_WEIGHTS + N_CAST + 4:N_WEIGHTS + 2 * N_CAST + 4]
    xp_ref = rest[N_WEIGHTS + 2 * N_CAST + 4]
    sets = (ChunkBufs(*rest[N_WEIGHTS + 2 * N_CAST + 5:N_WEIGHTS + 2 * N_CAST + 5 + N_BUFS]),
            ChunkBufs(*rest[N_WEIGHTS + 2 * N_CAST + 5 + N_BUFS:]))
    step = pl.program_id(0)

    for src_ref, dst_ref in zip(cast_in, cast_out):
        dst_ref[...] = src_ref[...].astype(BF16)

    @pl.when(step == 0)
    def _():
        for ref in (convc_ref, lruc_ref, ssmc_ref):
            ref[...] = jnp.zeros(ref.shape, ref.dtype)
        for ref in sets[1]:
            ref[...] = jnp.zeros(ref.shape, ref.dtype)
        for task in _norm_tasks(seq, x_prev_ref, w, sets[0].h):
            task()

    def body(write, read):
        a = _stage_a_tasks(1, seq, w, convc_ref, xp_ref, write, step < n_chunks)
        l = _stage_l_tasks(1, seq, w, lruc_ref, ssmc_ref, read, step >= 1)
        c = _stage_c_tasks(seq, x_prev_ref, y_ref, w, read)
        norm_next = _norm_tasks(seq, x_next_ref, w, read.h)
        for task in (_zip_tasks(l, a["in_proj"])
                     + _zip_tasks(c["s5_out"], _zip_tasks(a["conv_gates"], a["s5_in"]))
                     + _stage_c_tail(c, norm_next)):
            task()

    @pl.when(step % 2 == 0)
    def _():
        body(sets[0], sets[1])

    @pl.when(step % 2 == 1)
    def _():
        body(sets[1], sets[0])


def _full_spec(shape):
    zeros = (0,) * len(shape)
    return pl.BlockSpec(shape, lambda i, _z=zeros: _z)


def _layer_spec(shape, layer):
    index = (layer,) + (0,) * (len(shape) - 1)
    return pl.BlockSpec((None,) + tuple(shape[1:]), lambda i, _x=index: _x, pipeline_mode=pl.Buffered(1))


def _chunk_scratch(rows):
    xs_rows = -(-(S5_TIME_PITCH * rows + S5_ROW_PITCH * SUBLANES) // SUBLANES) * SUBLANES
    return [pltpu.VMEM((SSM_SLABS, xs_rows, LANES), F32),
            pltpu.VMEM((rows * LRU_CHUNKS, LANES), F32), pltpu.VMEM((rows * LRU_CHUNKS, LANES), F32),
            pltpu.VMEM((rows, LRU_W), F32), pltpu.VMEM((rows, SSM_W), F32),
            pltpu.VMEM((rows, D_MODEL), BF16)]


def _mixer_call(x2d, states, w, layer, nb, seq, pipelined, to_bf16=()):
    total = x2d.shape[0]
    rows = nb * seq
    tile_shapes = ((nb, SUBLANES, LRU_W), (nb, SUBLANES, LANES), (nb, SSM_SLABS, SUBLANES, LANES))
    state_specs = [_full_spec(s) for s in tile_shapes]
    weight_specs = [_layer_spec(a.shape, layer) for a in w]
    out_shape = (jax.ShapeDtypeStruct((total, D_MODEL), F32),) + tuple(
        jax.ShapeDtypeStruct(s, F32) for s in tile_shapes)
    conv_scratch = pltpu.VMEM((nb, LRU_CHUNKS, CONV_PITCH * seq + CONV_BASE, LANES), F32)
    scratch = [conv_scratch] + _chunk_scratch(rows)
    params = pltpu.CompilerParams(dimension_semantics=("arbitrary",), vmem_limit_bytes=VMEM_LIMIT_BYTES)
    if not pipelined:
        assert total == rows
        x_spec = pl.BlockSpec((rows, D_MODEL), lambda i: (0, 0))
        state_in_specs = [_layer_spec(s.shape, layer) for s in states]
        return pl.pallas_call(
            functools.partial(_mixer_plain_kernel, nb, seq),
            grid=(1,), in_specs=[x_spec] + state_in_specs + weight_specs,
            out_specs=(x_spec,) + tuple(state_specs), out_shape=out_shape,
            scratch_shapes=scratch, name="mixer_plain", compiler_params=params,
        )(x2d, *states, *w), ()
    assert nb == 1 and states is None and total % rows == 0
    n_chunks = total // rows
    next_spec = pl.BlockSpec((rows, D_MODEL), lambda i: (jnp.minimum(i + 1, n_chunks - 1), 0))
    prev_spec = pl.BlockSpec((rows, D_MODEL), lambda i: (jnp.maximum(i - 1, 0), 0))
    cast_in_specs, cast_out_specs, cast_out_shape = [], [], []
    for a in to_bf16:
        block_rows = CAST_BLOCK_ROWS[a.shape[1]]
        last = a.shape[1] // block_rows - 1
        assert a.shape[1] % block_rows == 0 and last <= n_chunks
        cast_in_specs.append(pl.BlockSpec((None, block_rows, a.shape[2]),
                                          lambda i, _l=last: (layer, jnp.minimum(i, _l), 0)))
        cast_out_specs.append(pl.BlockSpec((block_rows, a.shape[2]), lambda i, _l=last: (jnp.minimum(i, _l), 0)))
        cast_out_shape.append(jax.ShapeDtypeStruct(a.shape[1:], BF16))
    outs = pl.pallas_call(
        functools.partial(_mixer_pipelined_kernel, n_chunks, seq),
        grid=(n_chunks + 1,), in_specs=[next_spec, prev_spec] + weight_specs + cast_in_specs,
        out_specs=(prev_spec,) + tuple(state_specs) + tuple(cast_out_specs),
        out_shape=out_shape + tuple(cast_out_shape),
        scratch_shapes=scratch + _chunk_scratch(rows), name="mixer", compiler_params=params,
    )(x2d, x2d, *w, *to_bf16)
    return outs[:4], outs[4:]


def _ffn_rows(final_norm, layer, x, gffn_ref, wg_ref, wu_ref, wd_ref, gfin_ref):
    h = _rmsnorm(x, gffn_ref[layer:layer + 1, :]).astype(BF16)
    acts = []
    for c in range(D_FF // FF_CHUNK):
        lo, hi = c * FF_CHUNK, (c + 1) * FF_CHUNK
        gate = _dot(h, wg_ref[:, lo:hi])
        up = _dot(h, wu_ref[:, lo:hi])
        acts.append((gate * jax.nn.sigmoid(gate) * up).astype(BF16))
    acc = x + _dot(jnp.concatenate(acts, axis=1), wd_ref[...])
    return _rmsnorm(acc, gfin_ref[...]) if final_norm else acc


def _ffn_kernel(final_norm, layer, main_steps, xa_ref, xb_ref, gffn_ref, wg_ref, wu_ref, wd_ref, gfin_ref,
                oa_ref, ob_ref):
    weights = (gffn_ref, wg_ref, wu_ref, wd_ref, gfin_ref)
    step = pl.program_id(0)

    @pl.when(step < main_steps)
    def _():
        oa_ref[...] = _ffn_rows(final_norm, layer, xa_ref[...], *weights)

    @pl.when(step == main_steps)
    def _():
        ob_ref[...] = _ffn_rows(final_norm, layer, xb_ref[...], *weights)


def _ffn_call(xa, xb, w, layer, gfin, final_norm, tile):
    main_steps = xa.shape[0] // tile
    tile_spec = pl.BlockSpec((tile, D_MODEL), lambda i: (jnp.minimum(i, main_steps - 1), 0))
    in_specs = [tile_spec, _full_spec(xb.shape)]
    gffn, matrices = w[0], w[1:]
    resident = [pl.BlockSpec(a.shape, lambda i: (0, 0), pipeline_mode=pl.Buffered(1)) for a in matrices]
    in_specs += [_full_spec(gffn.shape)] + resident + [_full_spec(gfin.shape)]
    return pl.pallas_call(
        functools.partial(_ffn_kernel, final_norm, layer, main_steps),
        grid=(main_steps + 1,), in_specs=in_specs,
        out_specs=(tile_spec, _full_spec(xb.shape)),
        out_shape=(jax.ShapeDtypeStruct(xa.shape, F32), jax.ShapeDtypeStruct(xb.shape, F32)), name="ffn",
        compiler_params=pltpu.CompilerParams(dimension_semantics=("arbitrary",),
                                             vmem_limit_bytes=VMEM_LIMIT_BYTES),
    )(xa, xb, *w, gfin)


def _slabs_to_ssm(slabs):
    lead = slabs.shape[:-3]
    return (slabs[..., 0:2, :, :].reshape(lead + (SSM_G, SSM_P)), slabs[..., 2:4, :, :].reshape(lead + (SSM_G, SSM_P)))


def _tile_to_lru(tile):
    return tile[..., :LRU_CHUNKS, :].reshape(tile.shape[:-2] + (LRU_W,))


def kernel(x_prompt, x_sample, state_conv, state_lru, state_ssm_re, state_ssm_im, norm_mix, w_in, conv_w, conv_b, lru_wa, lru_ba, lru_wx, lru_bx, lru_lambda, ssm_lambda_re, ssm_lambda_im, ssm_b_re, ssm_b_im, ssm_c_re, ssm_c_im, ssm_d, ssm_log_dt, ssm_w_glu, ssm_b_glu, norm_lru_out, norm_ssm_out, w_out, norm_ffn, w_gate, w_up, w_down, norm_final):
    depth = w_in.shape[0]
    bp, tp, _ = x_prompt.shape
    bs, ts, _ = x_sample.shape
    assert bp == 1

    by_name = dict(gmix=norm_mix, convb=conv_b, lam=lru_lambda, ba=lru_ba, bx=lru_bx, dskip=ssm_d,
                   bglu=ssm_b_glu, glru=norm_lru_out, gssm=norm_ssm_out)
    lbr, lbi, bw, cwt, wgate, vec = _prep(ssm_lambda_re, ssm_lambda_im, ssm_log_dt, ssm_b_re, ssm_b_im,
                                          ssm_c_re, ssm_c_im, lru_wa, lru_wx,
                                          [by_name[name] for name, _ in VEC_FIELDS])

    mixer_w = MixerWeights(
        vec=vec, win=w_in.astype(BF16), convw=conv_w, wgate=wgate, lbr=lbr, lbi=lbi, bw=bw, cwt=cwt,
        wglu=ssm_w_glu.astype(BF16), wout=w_out.astype(BF16))
    gfin = norm_final.reshape(1, -1)

    yp = x_prompt.reshape(bp * tp, D_MODEL)
    ys = x_sample.reshape(bs * ts, D_MODEL)
    sample_states = (state_conv, state_lru, state_ssm_re, state_ssm_im)

    prompt_out, sample_out = [], []
    for l in range(depth):
        last = l == depth - 1
        (yp, *st_p), ffn_bf16 = _mixer_call(yp, None, mixer_w, l, 1, PROMPT_CHUNK, True, (w_gate, w_up, w_down))
        (ys, *st_s), _ = _mixer_call(ys, sample_states, mixer_w, l, bs, ts, False)
        yp, ys = _ffn_call(yp, ys, (norm_ffn,) + tuple(ffn_bf16), l, gfin, last, FFN_TILE)
        prompt_out.append(st_p)
        sample_out.append(st_s)

    def unpack(per_layer):
        conv, lru, ssm = (jnp.stack(v) for v in zip(*per_layer))
        re, im = _slabs_to_ssm(ssm)
        return conv[:, :, SUBLANES - (CONV_W - 1):], _tile_to_lru(lru), re, im

    return (yp.reshape(bp, tp, D_MODEL), ys.reshape(bs, ts, D_MODEL)) + unpack(prompt_out) + unpack(sample_out)
```

```python
import collections
import functools
import math

import jax
import jax.numpy as jnp
from jax import lax
from jax.experimental import pallas as pl
from jax.experimental.pallas import tpu as pltpu

D_MODEL = 1024
LRU_W = 512
LRU_HEADS = 8
LRU_HD = LRU_W // LRU_HEADS
CONV_W = 4
LRU_C = 8.0
SSM_W = 512
SSM_H = 16
SSM_G = SSM_W // SSM_H
SSM_P = 64
IN_W = 2 * LRU_W + SSM_W
D_FF = 2816
EPS = 1e-6

SUBLANES = 8
LANES = 128
MXU_DIM = 256
VMEM_LIMIT_BYTES = 60 * 1024 * 1024

SSM_SLABS = 4
S5_ROWS_PER_CHUNK = LANES // (2 * SSM_H)
S5_TIME_PITCH = 9
S5_ROW_PITCH = 2
LRU_CHUNKS = LRU_W // LANES
CONV_PITCH = 2
CONV_BASE = CONV_PITCH * SUBLANES
GATE_HALVES = LRU_W // MXU_DIM
FF_CHUNK = MXU_DIM
PREP_STEPS = 4
PROMPT_CHUNK = 512
VPU_TASK_ROWS = 512
MXU_TASK_ROWS = 512
assert MXU_TASK_ROWS % VPU_TASK_ROWS == 0
FFN_TILE = 1024
BF16 = jnp.bfloat16
F32 = jnp.float32

MixerWeights = collections.namedtuple(
    "MixerWeights",
    "vec win convw wgate lbr lbi bw cwt wglu wout")
VEC_FIELDS = (("gmix", D_MODEL), ("convb", LRU_W), ("lam", LRU_W), ("ba", LRU_W), ("bx", LRU_W),
              ("dskip", SSM_W), ("bglu", SSM_W), ("glru", LRU_W), ("gssm", SSM_W))
VEC_OFFSET = {name: sum(width for _, width in VEC_FIELDS[:i]) for i, (name, _) in enumerate(VEC_FIELDS)}
VEC_WIDTH = dict(VEC_FIELDS)
ChunkBufs = collections.namedtuple("ChunkBufs", "xs a b gl us h")


def _rmsnorm(x, gain):
    var = jnp.mean(x * x, axis=-1, keepdims=True)
    return x * lax.rsqrt(var + EPS) * gain


def _dot(a, b):
    return jnp.dot(a, b, preferred_element_type=F32)


_GELU_K1 = -2.0 * math.sqrt(2.0 / math.pi) * math.log2(math.e)
_GELU_K2 = _GELU_K1 * 0.044715


def _gelu(x):
    return x / (1.0 + jnp.exp2(x * (_GELU_K1 + _GELU_K2 * (x * x))))


def _keep_if(pred, new, old):
    return new if pred is None else jnp.where(pred, new, old)


def _lanes(c):
    return slice(c * LANES, (c + 1) * LANES)


def _vec(w, name, lo=0, hi=None):
    hi = VEC_WIDTH[name] if hi is None else hi
    return w.vec[:, VEC_OFFSET[name] + lo:VEC_OFFSET[name] + hi]


def _prep_kernel(depth, n_cast, lr_ref, li_ref, ldt_ref, br_ref, bi_ref, cr_ref, ci_ref, wa_ref, wx_ref, *rest):
    field_refs = rest[:len(VEC_FIELDS)]
    cast_in = rest[len(VEC_FIELDS):len(VEC_FIELDS) + n_cast]
    lbr_ref, lbi_ref, bw_ref, cwt_ref, wgate_ref, vec_ref = rest[len(VEC_FIELDS) + n_cast:len(VEC_FIELDS) + n_cast + 6]
    cast_out = rest[len(VEC_FIELDS) + n_cast + 6:]
    for src_ref, dst_ref in zip(cast_in, cast_out):
        dst_ref[...] = src_ref[...].astype(BF16)
    pl.when(pl.program_id(0) == 0)(functools.partial(
        _prep_body, depth, lr_ref, li_ref, ldt_ref, br_ref, bi_ref, cr_ref, ci_ref, wa_ref, wx_ref, field_refs,
        lbr_ref, lbi_ref, bw_ref, cwt_ref, wgate_ref, vec_ref))


def _prep_body(depth, lr_ref, li_ref, ldt_ref, br_ref, bi_ref, cr_ref, ci_ref, wa_ref, wx_ref, field_refs,
               lbr_ref, lbi_ref, bw_ref, cwt_ref, wgate_ref, vec_ref):
    lr = lr_ref[...]
    li = li_ref[...]
    diag = (lax.broadcasted_iota(jnp.int32, (SSM_G, SSM_G), 0) == lax.broadcasted_iota(jnp.int32, (SSM_G, SSM_G), 1))
    ldt = jnp.concatenate(
        [jnp.sum(jnp.where(diag, jnp.broadcast_to(ldt_ref[l:l + 1, :], (SSM_G, SSM_G)), 0.0), axis=1, keepdims=True)
         for l in range(depth)], axis=0)
    dt = jnp.exp(ldt)
    mag = jnp.exp(lr * dt)
    lbr = mag * jnp.cos(li * dt)
    lbi = mag * jnp.sin(li * dt)
    nr, ni = lbr - 1.0, lbi
    den = lr * lr + li * li
    gr = (nr * lr + ni * li) / den
    gi = (ni * lr - nr * li) / den
    bw_ref[...] = jnp.zeros(bw_ref.shape, BF16)
    cwt_ref[...] = jnp.zeros(cwt_ref.shape, BF16)
    wgate_ref[...] = jnp.zeros(wgate_ref.shape, BF16)
    chunks = SSM_W // LANES
    for l in range(depth):
        for g in range(SSM_G):
            s, gg = divmod(g, 2)
            c, sl = divmod(s, S5_ROWS_PER_CHUNK)
            row = slice(l * SSM_G + g, l * SSM_G + g + 1)
            src = slice((l * SSM_G + g) * SSM_H, (l * SSM_G + g + 1) * SSM_H)
            lanes_u = slice((sl * 2 + gg) * SSM_H, (sl * 2 + gg + 1) * SSM_H)
            state_lanes = slice(gg * SSM_P, (gg + 1) * SSM_P)
            lbr_ref[l * 2 + s // SUBLANES, s % SUBLANES:s % SUBLANES + 1, state_lanes] = lbr[row, :]
            lbi_ref[l * 2 + s // SUBLANES, s % SUBLANES:s % SUBLANES + 1, state_lanes] = lbi[row, :]
            br, bi = br_ref[src, :], bi_ref[src, :]
            bb = (gr[row, :] * br - gi[row, :] * bi, gr[row, :] * bi + gi[row, :] * br)
            cc = (cr_ref[src, :], -ci_ref[src, :])
            for reim in range(2):
                col = (sl * 2 + reim) * LANES + gg * SSM_P
                bw_ref[l * chunks + c, lanes_u, col:col + SSM_P] = bb[reim].astype(BF16)
                cwt_ref[l * chunks + c, lanes_u, col:col + SSM_P] = cc[reim].astype(BF16)
        per = LRU_HEADS // GATE_HALVES
        for head in range(LRU_HEADS):
            hf, hh = divmod(head, per)
            src = slice((l * LRU_HEADS + head) * LRU_HD, (l * LRU_HEADS + head + 1) * LRU_HD)
            blk = slice(hh * LRU_HD, (hh + 1) * LRU_HD)
            wgate_ref[l * GATE_HALVES + hf, blk, blk] = wa_ref[src, :].astype(BF16)
            wgate_ref[l * GATE_HALVES + hf, blk, MXU_DIM + hh * LRU_HD:MXU_DIM + (hh + 1) * LRU_HD] = (
                wx_ref[src, :].astype(BF16))
        for (name, width), ref in zip(VEC_FIELDS, field_refs):
            vec_ref[l, :, VEC_OFFSET[name]:VEC_OFFSET[name] + width] = ref[l:l + 1, :]


def _prep(lam_re, lam_im, log_dt, b_re, b_im, c_re, c_im, wa, wx, fields, to_bf16):
    depth = lam_re.shape[0]
    groups = depth * SSM_G
    rows = groups * SSM_H
    br = jnp.swapaxes(b_re, 2, 3).reshape(rows, SSM_P)
    bi = jnp.swapaxes(b_im, 2, 3).reshape(rows, SSM_P)
    chunks = SSM_W // LANES
    vec_total = sum(width for _, width in VEC_FIELDS)
    slab = jax.ShapeDtypeStruct((depth * 2, SUBLANES, LANES), F32)
    out_shape = (slab, slab,
                 jax.ShapeDtypeStruct((depth * chunks, LANES, S5_ROWS_PER_CHUNK * 2 * LANES), BF16),
                 jax.ShapeDtypeStruct((depth * chunks, LANES, S5_ROWS_PER_CHUNK * 2 * LANES), BF16),
                 jax.ShapeDtypeStruct((depth * GATE_HALVES, MXU_DIM, 2 * MXU_DIM), BF16),
                 jax.ShapeDtypeStruct((depth, 1, vec_total), F32))
    operands = (lam_re.reshape(groups, SSM_P), lam_im.reshape(groups, SSM_P), log_dt, br, bi,
                c_re.reshape(rows, SSM_P), c_im.reshape(rows, SSM_P),
                wa.reshape(depth * LRU_W, LRU_HD), wx.reshape(depth * LRU_W, LRU_HD), *fields)
    cast_in_specs = [pl.BlockSpec((None, a.shape[1] // PREP_STEPS, a.shape[2]), lambda i: (0, i, 0)) for a in to_bf16]
    cast_out_specs = [pl.BlockSpec((a.shape[1] // PREP_STEPS, a.shape[2]), lambda i: (i, 0)) for a in to_bf16]
    outs = pl.pallas_call(
        functools.partial(_prep_kernel, depth, len(to_bf16)), grid=(PREP_STEPS,),
        in_specs=[_full_spec(a.shape) for a in operands] + cast_in_specs,
        out_specs=[_full_spec(s.shape) for s in out_shape] + cast_out_specs,
        out_shape=out_shape + tuple(jax.ShapeDtypeStruct(a.shape[1:], BF16) for a in to_bf16), name="prep",
        compiler_params=pltpu.CompilerParams(dimension_semantics=("arbitrary",)),
    )(*operands, *to_bf16)
    lbr, lbi, bw, cwt, wgate, vec = outs[:6]
    return (lbr.reshape(depth, 2, SUBLANES, LANES), lbi.reshape(depth, 2, SUBLANES, LANES),
            bw.reshape((depth, chunks) + bw.shape[1:]), cwt.reshape((depth, chunks) + cwt.shape[1:]),
            wgate.reshape((depth, GATE_HALVES) + wgate.shape[1:]), vec, outs[6:])


def _s5_time_rows(srow, row0, count):
    return pl.ds(S5_ROW_PITCH * srow + S5_TIME_PITCH * row0, count, stride=S5_TIME_PITCH)


def _s5_step_rows(r):
    return pl.ds(S5_TIME_PITCH * r, SUBLANES, stride=S5_ROW_PITCH)


def _lru_time_rows(c, row0, count):
    return pl.ds(c + LRU_CHUNKS * row0, count, stride=LRU_CHUNKS)


def _blocks(rows, size):
    size = min(size, rows)
    return [(start, size) for start in range(0, rows, size)]


def _zip_tasks(a, b):
    keyed = [((i + 0.5) / len(a), 0, i, t) for i, t in enumerate(a)]
    keyed += [((i + 0.5) / len(b), 1, i, t) for i, t in enumerate(b)]
    return [t for _, _, _, t in sorted(keyed, key=lambda e: e[:3])]


def _norm_tasks(rows, x_ref, w, h_ref):
    def norm(start, size):
        h_ref[start:start + size, :] = _rmsnorm(x_ref[start:start + size, :], _vec(w, "gmix")).astype(BF16)
    return [functools.partial(norm, start, size) for start, size in _blocks(rows, VPU_TASK_ROWS)]


def _stage_a_tasks(nb, seq, w, convc_ref, xp_ref, bufs, live):
    rows = nb * seq
    vpu_blocks = _blocks(rows, VPU_TASK_ROWS)
    mxu_blocks = _blocks(rows, MXU_TASK_ROWS)
    st = {}
    groups = {"in_proj": [], "conv_gates": [], "s5_in": []}

    def in_proj(k, start, size):
        z = _dot(bufs.h[start:start + size, :], w.win[:, k * MXU_DIM:(k + 1) * MXU_DIM])
        st["z", k, start] = z
        if k in (2, 3):
            bufs.gl[start:start + size, (k - 2) * MXU_DIM:(k - 1) * MXU_DIM] = z
        if k in (4, 5):
            bufs.us[start:start + size, (k - 4) * MXU_DIM:(k - 3) * MXU_DIM] = z
    for k in range(IN_W // MXU_DIM):
        for start, size in mxu_blocks:
            groups["in_proj"].append(functools.partial(in_proj, k, start, size))

    def z_lanes(k0, c):
        zk, off = divmod(c * LANES, MXU_DIM)
        parts = [st["z", k0 + zk, s][:, off:off + LANES] for s, n in mxu_blocks]
        return parts[0] if len(parts) == 1 else jnp.concatenate(parts, axis=0)

    def conv(c):
        xl = z_lanes(0, c)
        cw = w.convw[:, _lanes(c)]
        parts = []
        for j in range(nb):
            xl_j = xl[j * seq:(j + 1) * seq, :]
            xp_ref[j, c, pl.ds(0, SUBLANES, stride=CONV_PITCH), :] = convc_ref[j, :, _lanes(c)]
            xp_ref[j, c, pl.ds(CONV_BASE, seq, stride=CONV_PITCH), :] = xl_j
            acc = _vec(w, "convb", c * LANES, (c + 1) * LANES) + xl_j * cw[CONV_W - 1:CONV_W, :]
            for k in range(CONV_W - 1):
                start = CONV_BASE - CONV_PITCH * (CONV_W - 1 - k)
                acc = acc + xp_ref[j, c, pl.ds(start, seq, stride=CONV_PITCH), :] * cw[k:k + 1, :]
            parts.append(acc)
            convc_ref[j, :, _lanes(c)] = _keep_if(live, xl_j[seq - SUBLANES:, :], convc_ref[j, :, _lanes(c)])
        st["xc", c] = parts[0] if nb == 1 else jnp.concatenate(parts, axis=0)
    for c in range(LRU_CHUNKS):
        groups["conv_gates"].append(functools.partial(conv, c))

    def gates_dot(hf):
        chunks = range(hf * MXU_DIM // LANES, (hf + 1) * MXU_DIM // LANES)
        xc = jnp.concatenate([st["xc", c] for c in chunks], axis=1)
        st["xc2", hf] = xc
        st["pre", hf] = _dot(xc.astype(BF16), w.wgate[hf])

    def gates_ew(hf, start, size):
        lo, hi = hf * MXU_DIM, (hf + 1) * MXU_DIM
        lam = _vec(w, "lam", lo, hi)
        neg_c_softplus = -LRU_C * (jnp.maximum(-lam, 0.0) + jnp.log1p(jnp.exp(-jnp.abs(lam))))
        pre = st["pre", hf][start:start + size, :]
        xc = st["xc2", hf][start:start + size, :]
        r = jax.nn.sigmoid(pre[:, :MXU_DIM] + _vec(w, "ba", lo, hi))
        ig = jax.nn.sigmoid(pre[:, MXU_DIM:] + _vec(w, "bx", lo, hi))
        log_a = neg_c_softplus * r
        a = jnp.exp(log_a)
        one_minus_a2 = -jnp.tanh(log_a) * (a * a + 1.0)
        b = jnp.sqrt(one_minus_a2) * (ig * xc)
        for i in range(MXU_DIM // LANES):
            c = hf * MXU_DIM // LANES + i
            bufs.a[_lru_time_rows(c, start, size), :] = a[:, _lanes(i)]
            bufs.b[_lru_time_rows(c, start, size), :] = b[:, _lanes(i)]
    for hf in range(GATE_HALVES):
        groups["conv_gates"].append(functools.partial(gates_dot, hf))
        for start, size in vpu_blocks:
            groups["conv_gates"].append(functools.partial(gates_ew, hf, start, size))

    def s5_in(c, start, size):
        zk, off = divmod(c * LANES, MXU_DIM)
        us_bf = st["z", 4 + zk, start][:, off:off + LANES].astype(BF16)
        bu = _dot(us_bf, w.bw[c])
        half = (c * S5_ROWS_PER_CHUNK) // SUBLANES
        for sl in range(S5_ROWS_PER_CHUNK):
            srow = (c * S5_ROWS_PER_CHUNK + sl) % SUBLANES
            for reim in range(2):
                col = (sl * 2 + reim) * LANES
                bufs.xs[reim * 2 + half, _s5_time_rows(srow, start, size), :] = bu[:, col:col + LANES]
    for c in range(SSM_W // LANES):
        for start, size in mxu_blocks:
            groups["s5_in"].append(functools.partial(s5_in, c, start, size))
    return groups


def _stage_l_tasks(nb, seq, w, lruc_ref, ssmc_ref, bufs, live):
    st = {}
    tasks = []
    row = lambda: lax.broadcasted_iota(jnp.int32, (SUBLANES, LANES), 0)

    def begin(j):
        st["lb"] = ((w.lbr[0], w.lbr[1]), (w.lbi[0], w.lbi[1]))
        st["xr"] = [ssmc_ref[j, 0], ssmc_ref[j, 1]]
        st["xi"] = [ssmc_ref[j, 2], ssmc_ref[j, 3]]
        st["h"] = lruc_ref[j]
        st["old"] = (tuple(st["xr"]), tuple(st["xi"]), st["h"])

    def steps(j, tile):
        lbr, lbi = st["lb"]
        xr, xi = st["xr"], st["xi"]
        for t in range(tile * SUBLANES, (tile + 1) * SUBLANES):
            rows_t = _s5_step_rows(j * seq + t)
            for hv in range(2):
                nr = lbr[hv] * xr[hv] - lbi[hv] * xi[hv] + bufs.xs[hv, rows_t, :]
                ni = lbr[hv] * xi[hv] + lbi[hv] * xr[hv] + bufs.xs[2 + hv, rows_t, :]
                bufs.xs[hv, rows_t, :] = nr
                bufs.xs[2 + hv, rows_t, :] = ni
                xr[hv], xi[hv] = nr, ni
        first_half = row() < LRU_CHUNKS
        h = st["h"]
        for pair in range(tile * SUBLANES // 2, (tile + 1) * SUBLANES // 2):
            r0 = (j * seq + 2 * pair) * LRU_CHUNKS
            a2 = bufs.a[r0:r0 + SUBLANES, :]
            b2 = bufs.b[r0:r0 + SUBLANES, :]
            h_even = a2 * h + b2
            h_odd = a2 * pltpu.roll(h_even, LRU_CHUNKS, 0) + b2
            bufs.b[r0:r0 + SUBLANES, :] = jnp.where(first_half, h_even, h_odd)
            h = pltpu.roll(h_odd, LRU_CHUNKS, 0)
        st["h"] = h

    def end(j):
        old = st["old"]
        for hv in range(2):
            ssmc_ref[j, hv] = _keep_if(live, st["xr"][hv], old[0][hv])
            ssmc_ref[j, 2 + hv] = _keep_if(live, st["xi"][hv], old[1][hv])
        lruc_ref[j] = _keep_if(live, st["h"], old[2])

    for j in range(nb):
        tasks.append(functools.partial(begin, j))
        for tile in range(seq // SUBLANES):
            tasks.append(functools.partial(steps, j, tile))
        tasks.append(functools.partial(end, j))
    return tasks


def _stage_c_tasks(rows, x_ref, y_ref, w, bufs):
    vpu_blocks = _blocks(rows, VPU_TASK_ROWS)
    mxu_blocks = _blocks(rows, MXU_TASK_ROWS)
    chunks = SSM_W // LANES
    st = {}
    groups = {"s5_out": [], "glu_dot": {}, "lru_out": {}, "glu_ew": {}, "out_proj": {}}

    def rows_of(key, start, size):
        parts = [st[key, s] for s, n in vpu_blocks if start <= s < start + size]
        return parts[0] if len(parts) == 1 else jnp.concatenate(parts, axis=0)

    def s5_out(c, start, size):
        pieces = []
        for sl in range(S5_ROWS_PER_CHUNK):
            s = c * S5_ROWS_PER_CHUNK + sl
            half, srow = s // SUBLANES, s % SUBLANES
            pieces.append(bufs.xs[half, _s5_time_rows(srow, start, size), :])
            pieces.append(bufs.xs[2 + half, _s5_time_rows(srow, start, size), :])
        xcat = jnp.concatenate(pieces, axis=1).astype(BF16)
        y = lax.dot_general(xcat, w.cwt[c], (((1,), (1,)), ((), ())), preferred_element_type=F32)
        st["g", c, start] = _gelu(y + _vec(w, "dskip", c * LANES, (c + 1) * LANES) * bufs.us[start:start + size, _lanes(c)])
    for start, size in mxu_blocks:
        for c in range(chunks):
            groups["s5_out"].append(functools.partial(s5_out, c, start, size))

    def lru_out(start, size):
        h = jnp.concatenate([bufs.b[_lru_time_rows(c, start, size), :] for c in range(LRU_CHUNKS)], axis=1)
        st["n_lru", start] = _rmsnorm(h * _gelu(bufs.gl[start:start + size, :]), _vec(w, "glru")).astype(BF16)

    def glu_dot(start, size):
        g = jnp.concatenate([st["g", c, start] for c in range(chunks)], axis=1)
        st["g4", start] = g
        st["gate", start] = _dot(g.astype(BF16), w.wglu[...]) + _vec(w, "bglu")

    def glu_ew(start, size):
        m0 = max(s for s, n in mxu_blocks if s <= start)
        g = st["g4", m0][start - m0:start - m0 + size, :]
        gate = st["gate", m0][start - m0:start - m0 + size, :]
        st["n_ssm", start] = _rmsnorm(g * jax.nn.sigmoid(gate), _vec(w, "gssm")).astype(BF16)

    def out_proj(k, start, size):
        cols = slice(k * MXU_DIM, (k + 1) * MXU_DIM)
        mix = jnp.concatenate([rows_of("n_lru", start, size), rows_of("n_ssm", start, size)], axis=1)
        y_ref[start:start + size, cols] = x_ref[start:start + size, cols] + _dot(mix, w.wout[:, cols])
    for start, size in mxu_blocks:
        inside = [(s, n) for s, n in vpu_blocks if start <= s < start + size]
        groups["glu_dot"][start] = functools.partial(glu_dot, start, size)
        groups["lru_out"][start] = [functools.partial(lru_out, s, n) for s, n in inside]
        groups["glu_ew"][start] = [functools.partial(glu_ew, s, n) for s, n in inside]
        groups["out_proj"][start] = [functools.partial(out_proj, k, start, size) for k in range(D_MODEL // MXU_DIM)]
    return groups


def _stage_c_tail(c_groups, fillers):
    starts = sorted(c_groups["glu_dot"])
    order = _zip_tasks([c_groups["glu_dot"][m] for m in starts], c_groups["lru_out"][starts[0]])
    order += c_groups["glu_ew"][starts[0]]
    for m, nxt in zip(starts, starts[1:] + [None]):
        companions = fillers if nxt is None else c_groups["lru_out"][nxt] + c_groups["glu_ew"][nxt]
        order += _zip_tasks(c_groups["out_proj"][m], companions) if companions else c_groups["out_proj"][m]
    return order


N_WEIGHTS = len(MixerWeights._fields)
N_BUFS = len(ChunkBufs._fields)
CAST_BLOCK_ROWS = {D_MODEL: 32, D_FF: 128, SSM_W: 16}
UNSTACKED_WEIGHTS = ("win", "wglu", "wout")


def _load_states(nb, conv0_ref, lru0_ref, re0_ref, im0_ref, convc_ref, lruc_ref, ssmc_ref):
    convc_ref[...] = jnp.zeros(convc_ref.shape, F32)
    convc_ref[:, SUBLANES - (CONV_W - 1):, :] = conv0_ref[...]
    lruc_ref[...] = jnp.zeros(lruc_ref.shape, F32)
    for j in range(nb):
        for c in range(LRU_CHUNKS):
            lruc_ref[j, c:c + 1, :] = lru0_ref[j:j + 1, _lanes(c)]
        for g in range(SSM_G):
            s, gg = divmod(g, 2)
            dst = (s % SUBLANES, slice(gg * SSM_P, (gg + 1) * SSM_P))
            ssmc_ref[j, s // SUBLANES, dst[0]:dst[0] + 1, dst[1]] = re0_ref[j, g:g + 1, :]
            ssmc_ref[j, 2 + s // SUBLANES, dst[0]:dst[0] + 1, dst[1]] = im0_ref[j, g:g + 1, :]


def _mixer_plain_kernel(nb, seq, x_ref, conv0_ref, lru0_ref, re0_ref, im0_ref, *rest):
    w = MixerWeights(*rest[:N_WEIGHTS])
    y_ref, convc_ref, lruc_ref, ssmc_ref, xp_ref = rest[N_WEIGHTS:N_WEIGHTS + 5]
    bufs = ChunkBufs(*rest[N_WEIGHTS + 5:])
    rows = nb * seq
    _load_states(nb, conv0_ref, lru0_ref, re0_ref, im0_ref, convc_ref, lruc_ref, ssmc_ref)
    a = _stage_a_tasks(nb, seq, w, convc_ref, xp_ref, bufs, None)
    c = _stage_c_tasks(rows, x_ref, y_ref, w, bufs)
    for task in (_norm_tasks(rows, x_ref, w, bufs.h) + a["in_proj"] + a["conv_gates"] + a["s5_in"]
                 + _stage_l_tasks(nb, seq, w, lruc_ref, ssmc_ref, bufs, None)
                 + c["s5_out"] + _stage_c_tail(c, [])):
        task()


def _mixer_pipelined_kernel(n_chunks, seq, n_cast, x_next_ref, x_prev_ref, *rest):
    w = MixerWeights(*rest[:N_WEIGHTS])
    rest = list(rest[N_WEIGHTS:])
    cast_in, rest = rest[:n_cast], rest[n_cast:]
    (y_ref, convc_ref, lruc_ref, ssmc_ref), rest = rest[:4], rest[4:]
    cast_out, rest = rest[:n_cast], rest[n_cast:]
    xp_ref, rest = rest[0], rest[1:]
    sets = (ChunkBufs(*rest[:N_BUFS]), ChunkBufs(*rest[N_BUFS:]))
    step = pl.program_id(0)

    def cast(src_ref, dst_ref):
        dst_ref[...] = src_ref[...].astype(BF16)
    casts = [functools.partial(cast, s, d) for s, d in zip(cast_in, cast_out)]

    @pl.when(step == 0)
    def _():
        for ref in (convc_ref, lruc_ref, ssmc_ref):
            ref[...] = jnp.zeros(ref.shape, ref.dtype)
        for ref in sets[1]:
            ref[...] = jnp.zeros(ref.shape, ref.dtype)
        for task in _norm_tasks(seq, x_prev_ref, w, sets[0].h):
            task()

    def body(write, read):
        a = _stage_a_tasks(1, seq, w, convc_ref, xp_ref, write, step < n_chunks)
        l = _stage_l_tasks(1, seq, w, lruc_ref, ssmc_ref, read, step >= 1)
        c = _stage_c_tasks(seq, x_prev_ref, y_ref, w, read)
        norm_next = _norm_tasks(seq, x_next_ref, w, read.h)
        for task in (_zip_tasks(l, a["in_proj"])
                     + _zip_tasks(c["s5_out"], _zip_tasks(a["conv_gates"], a["s5_in"]))
                     + _stage_c_tail(c, norm_next + casts)):
            task()

    @pl.when(step % 2 == 0)
    def _():
        body(sets[0], sets[1])

    @pl.when(step % 2 == 1)
    def _():
        body(sets[1], sets[0])


def _full_spec(shape):
    zeros = (0,) * len(shape)
    return pl.BlockSpec(shape, lambda i, _z=zeros: _z)


def _layer_spec(shape, layer):
    index = (layer,) + (0,) * (len(shape) - 1)
    return pl.BlockSpec((None,) + tuple(shape[1:]), lambda i, _x=index: _x, pipeline_mode=pl.Buffered(1))


def _chunk_scratch(rows):
    xs_rows = -(-(S5_TIME_PITCH * rows + S5_ROW_PITCH * SUBLANES) // SUBLANES) * SUBLANES
    return [pltpu.VMEM((SSM_SLABS, xs_rows, LANES), F32),
            pltpu.VMEM((rows * LRU_CHUNKS, LANES), F32), pltpu.VMEM((rows * LRU_CHUNKS, LANES), F32),
            pltpu.VMEM((rows, LRU_W), F32), pltpu.VMEM((rows, SSM_W), F32),
            pltpu.VMEM((rows, D_MODEL), BF16)]


def _mixer_call(x2d, states, w, layer, nb, seq, pipelined, to_bf16=()):
    total = x2d.shape[0]
    rows = nb * seq
    tile_shapes = ((nb, SUBLANES, LRU_W), (nb, SUBLANES, LANES), (nb, SSM_SLABS, SUBLANES, LANES))
    state_specs = [_full_spec(s) for s in tile_shapes]
    weight_specs = [pl.BlockSpec(a.shape, lambda i, _z=(0,) * a.ndim: _z, pipeline_mode=pl.Buffered(1))
                    if name in UNSTACKED_WEIGHTS else _layer_spec(a.shape, layer) for name, a in zip(w._fields, w)]
    out_shape = (jax.ShapeDtypeStruct((total, D_MODEL), F32),) + tuple(
        jax.ShapeDtypeStruct(s, F32) for s in tile_shapes)
    conv_scratch = pltpu.VMEM((nb, LRU_CHUNKS, CONV_PITCH * seq + CONV_BASE, LANES), F32)
    scratch = [conv_scratch] + _chunk_scratch(rows)
    params = pltpu.CompilerParams(dimension_semantics=("arbitrary",), vmem_limit_bytes=VMEM_LIMIT_BYTES)
    if not pipelined:
        assert total == rows
        x_spec = pl.BlockSpec((rows, D_MODEL), lambda i: (0, 0))
        state_in_specs = [_layer_spec(s.shape, layer) for s in states]
        return pl.pallas_call(
            functools.partial(_mixer_plain_kernel, nb, seq),
            grid=(1,), in_specs=[x_spec] + state_in_specs + weight_specs,
            out_specs=(x_spec,) + tuple(state_specs), out_shape=out_shape,
            scratch_shapes=scratch, name="mixer_plain", compiler_params=params,
        )(x2d, *states, *w), ()
    assert nb == 1 and states is None and total % rows == 0
    n_chunks = total // rows
    next_spec = pl.BlockSpec((rows, D_MODEL), lambda i: (jnp.minimum(i + 1, n_chunks - 1), 0))
    prev_spec = pl.BlockSpec((rows, D_MODEL), lambda i: (jnp.maximum(i - 1, 0), 0))
    cast_in_specs, cast_out_specs, cast_out_shape = [], [], []
    for a, a_layer in to_bf16:
        block_rows = CAST_BLOCK_ROWS[a.shape[1]]
        last = a.shape[1] // block_rows - 1
        assert a.shape[1] % block_rows == 0 and last <= n_chunks
        cast_in_specs.append(pl.BlockSpec((None, block_rows, a.shape[2]),
                                          lambda i, _l=last, _a=a_layer: (_a, jnp.minimum(i, _l), 0)))
        cast_out_specs.append(pl.BlockSpec((block_rows, a.shape[2]), lambda i, _l=last: (jnp.minimum(i, _l), 0)))
        cast_out_shape.append(jax.ShapeDtypeStruct(a.shape[1:], BF16))
    outs = pl.pallas_call(
        functools.partial(_mixer_pipelined_kernel, n_chunks, seq, len(to_bf16)),
        grid=(n_chunks + 1,), in_specs=[next_spec, prev_spec] + weight_specs + cast_in_specs,
        out_specs=(prev_spec,) + tuple(state_specs) + tuple(cast_out_specs),
        out_shape=out_shape + tuple(cast_out_shape),
        scratch_shapes=scratch + _chunk_scratch(rows), name="mixer", compiler_params=params,
    )(x2d, x2d, *w, *(a for a, _ in to_bf16))
    return outs[:4], outs[4:]


def _ffn_rows(final_norm, layer, x, gffn_ref, wg_ref, wu_ref, wd_ref, gfin_ref):
    h = _rmsnorm(x, gffn_ref[layer:layer + 1, :]).astype(BF16)
    acts = []
    for c in range(D_FF // FF_CHUNK):
        lo, hi = c * FF_CHUNK, (c + 1) * FF_CHUNK
        gate = _dot(h, wg_ref[:, lo:hi])
        up = _dot(h, wu_ref[:, lo:hi])
        acts.append((gate * jax.nn.sigmoid(gate) * up).astype(BF16))
    acc = x + _dot(jnp.concatenate(acts, axis=1), wd_ref[...])
    return _rmsnorm(acc, gfin_ref[...]) if final_norm else acc


def _ffn_kernel(final_norm, layer, main_steps, xa_ref, xb_ref, gffn_ref, wg_ref, wu_ref, wd_ref, gfin_ref,
                oa_ref, ob_ref):
    weights = (gffn_ref, wg_ref, wu_ref, wd_ref, gfin_ref)
    step = pl.program_id(0)

    @pl.when(step < main_steps)
    def _():
        oa_ref[...] = _ffn_rows(final_norm, layer, xa_ref[...], *weights)

    @pl.when(step == main_steps)
    def _():
        ob_ref[...] = _ffn_rows(final_norm, layer, xb_ref[...], *weights)


def _ffn_call(xa, xb, w, layer, gfin, final_norm, tile):
    main_steps = xa.shape[0] // tile
    tile_spec = pl.BlockSpec((tile, D_MODEL), lambda i: (jnp.minimum(i, main_steps - 1), 0))
    in_specs = [tile_spec, _full_spec(xb.shape)]
    gffn, matrices = w[0], w[1:]
    resident = [pl.BlockSpec(a.shape, lambda i: (0, 0), pipeline_mode=pl.Buffered(1)) for a in matrices]
    in_specs += [_full_spec(gffn.shape)] + resident + [_full_spec(gfin.shape)]
    return pl.pallas_call(
        functools.partial(_ffn_kernel, final_norm, layer, main_steps),
        grid=(main_steps + 1,), in_specs=in_specs,
        out_specs=(tile_spec, _full_spec(xb.shape)),
        out_shape=(jax.ShapeDtypeStruct(xa.shape, F32), jax.ShapeDtypeStruct(xb.shape, F32)), name="ffn",
        compiler_params=pltpu.CompilerParams(dimension_semantics=("arbitrary",),
                                             vmem_limit_bytes=VMEM_LIMIT_BYTES),
    )(xa, xb, *w, gfin)


def _slabs_to_ssm(slabs):
    lead = slabs.shape[:-3]
    return (slabs[..., 0:2, :, :].reshape(lead + (SSM_G, SSM_P)), slabs[..., 2:4, :, :].reshape(lead + (SSM_G, SSM_P)))


def _tile_to_lru(tile):
    return tile[..., :LRU_CHUNKS, :].reshape(tile.shape[:-2] + (LRU_W,))


def kernel(x_prompt, x_sample, state_conv, state_lru, state_ssm_re, state_ssm_im, norm_mix, w_in, conv_w, conv_b, lru_wa, lru_ba, lru_wx, lru_bx, lru_lambda, ssm_lambda_re, ssm_lambda_im, ssm_b_re, ssm_b_im, ssm_c_re, ssm_c_im, ssm_d, ssm_log_dt, ssm_w_glu, ssm_b_glu, norm_lru_out, norm_ssm_out, w_out, norm_ffn, w_gate, w_up, w_down, norm_final):
    depth = w_in.shape[0]
    bp, tp, _ = x_prompt.shape
    bs, ts, _ = x_sample.shape
    assert bp == 1

    by_name = dict(gmix=norm_mix, convb=conv_b, lam=lru_lambda, ba=lru_ba, bx=lru_bx, dskip=ssm_d,
                   bglu=ssm_b_glu, glru=norm_lru_out, gssm=norm_ssm_out)
    f32_mats = dict(win=w_in, wglu=ssm_w_glu, wout=w_out)
    lbr, lbi, bw, cwt, wgate, vec, layer0 = _prep(ssm_lambda_re, ssm_lambda_im, ssm_log_dt, ssm_b_re, ssm_b_im,
                                                  ssm_c_re, ssm_c_im, lru_wa, lru_wx,
                                                  [by_name[name] for name, _ in VEC_FIELDS],
                                                  [f32_mats[name] for name in UNSTACKED_WEIGHTS])

    shared_w = dict(vec=vec, convw=conv_w, wgate=wgate, lbr=lbr, lbi=lbi, bw=bw, cwt=cwt)
    layer_mats = dict(zip(UNSTACKED_WEIGHTS, layer0))
    gfin = norm_final.reshape(1, -1)

    yp = x_prompt.reshape(bp * tp, D_MODEL)
    ys = x_sample.reshape(bs * ts, D_MODEL)
    sample_states = (state_conv, state_lru, state_ssm_re, state_ssm_im)

    prompt_out, sample_out = [], []
    for l in range(depth):
        last = l == depth - 1
        mixer_w = MixerWeights(**shared_w, **layer_mats)
        ahead = [] if last else [(f32_mats[name], l + 1) for name in UNSTACKED_WEIGHTS]
        (yp, *st_p), rounded = _mixer_call(yp, None, mixer_w, l, 1, PROMPT_CHUNK, True,
                                           [(w_gate, l), (w_up, l), (w_down, l)] + ahead)
        (ys, *st_s), _ = _mixer_call(ys, sample_states, mixer_w, l, bs, ts, False)
        yp, ys = _ffn_call(yp, ys, (norm_ffn,) + tuple(rounded[:3]), l, gfin, last, FFN_TILE)
        layer_mats = dict(zip(UNSTACKED_WEIGHTS, rounded[3:]))
        prompt_out.append(st_p)
        sample_out.append(st_s)

    def unpack(per_layer):
        conv, lru, ssm = (jnp.stack(v) for v in zip(*per_layer))
        re, im = _slabs_to_ssm(ssm)
        return conv[:, :, SUBLANES - (CONV_W - 1):], _tile_to_lru(lru), re, im

    return (yp.reshape(bp, tp, D_MODEL), ys.reshape(bs, ts, D_MODEL)) + unpack(prompt_out) + unpack(sample_out)
```
